```python
import math
import numpy as np
import jax
import jax.numpy as jnp
from jax import lax

D_MODEL = 4096
BATCH = 4
SEQ = 2048
DEPTH = 2

CTX_LEN = 256
GRID_W = 64
N_MOD = 9
D_FF = 2 * D_MODEL
EPS = 1e-6

GDN_HEAD_DIM = 128
GDN_V_HEADS = (D_MODEL // 2) // GDN_HEAD_DIM
GDN_QK_HEADS = GDN_V_HEADS // 2
GDN_QK_DIM = GDN_QK_HEADS * GDN_HEAD_DIM
GDN_V_DIM = GDN_V_HEADS * GDN_HEAD_DIM
GDN_CONV = 5
GDN_CHUNK = 64

SC_DIM = D_MODEL // 4
SC_CONV = 3

POOL_WINDOWS = (2, 4, 8, 16)
POOL_DIM = D_MODEL // 4
POOL_GROUP = POOL_DIM // len(POOL_WINDOWS)

MIX_DIM = GDN_V_DIM + SC_DIM + POOL_DIM
PROJ_SIZES = (GDN_QK_DIM, GDN_QK_DIM, GDN_V_DIM, GDN_V_DIM, 4 * GDN_V_HEADS, SC_DIM, SC_DIM, SC_DIM, POOL_DIM)
D_IN = sum(PROJ_SIZES)

kernel_name = 'hybrid_gdn_shortconv_pool_dit_block'


def rmsnorm(x, w):
    xf = x.astype(jnp.float32)
    y = xf * lax.rsqrt(jnp.mean(xf * xf, axis=-1, keepdims=True) + EPS)
    return (y * w.astype(jnp.float32)).astype(x.dtype)


def l2norm(x):
    xf = x.astype(jnp.float32)
    return xf * lax.rsqrt(jnp.sum(xf * xf, axis=-1, keepdims=True) + EPS)


def adaln(x, w, shift, scale):
    return rmsnorm(x, w) * (1 + scale) + shift


def swiglu(h, w_gu, w_down):
    gate, up = jnp.split(h @ w_gu, 2, axis=-1)
    return (jax.nn.silu(gate) * up) @ w_down


def split_proj(p):
    return jnp.split(p, np.cumsum(PROJ_SIZES)[:-1].tolist(), axis=-1)


def depthwise_conv_centred(u, w, axis):
    k = w.shape[0]
    pad = k // 2
    n = u.shape[axis]
    pads = [(0, 0)] * u.ndim
    pads[axis] = (pad, pad)
    up = jnp.pad(u, pads)
    return sum(lax.slice_in_dim(up, j, j + n, axis=axis) * w[j] for j in range(k))


def centred_window_mean(u, window, axis):
    n = u.shape[axis]
    cs = jnp.cumsum(u.astype(jnp.float32), axis=axis)
    zero = jnp.zeros_like(lax.slice_in_dim(cs, 0, 1, axis=axis))
    cs = jnp.concatenate([zero, cs], axis=axis)
    t = jnp.arange(n)
    lo = jnp.clip(t - window // 2, 0, n)
    hi = jnp.clip(t - window // 2 + window, 0, n)
    total = jnp.take(cs, hi, axis=axis) - jnp.take(cs, lo, axis=axis)
    count = (hi - lo).astype(jnp.float32).reshape((n,) + (1,) * (u.ndim - axis - 1))
    return (total / count).astype(u.dtype)


def gated_delta_chunked(q, k, v, g, beta, s0):
    bsz, t_len, heads, _ = q.shape
    dv = v.shape[-1]
    n_chunks = t_len // GDN_CHUNK

    def chunked(a):
        a = jnp.moveaxis(a.astype(jnp.float32), 2, 1)
        return a.reshape((bsz, heads, n_chunks, GDN_CHUNK) + a.shape[3:])

    q, k, v, g, beta = (chunked(a) for a in (q, k, v, g, beta))
    gc = jnp.cumsum(g, axis=-1)
    incl = jnp.tril(jnp.ones((GDN_CHUNK, GDN_CHUNK), bool))
    strict = jnp.tril(jnp.ones((GDN_CHUNK, GDN_CHUNK), bool), -1)
    diff = gc[..., :, None] - gc[..., None, :]
    decay = jnp.where(incl, jnp.exp(jnp.where(incl, diff, 0.0)), 0.0)
    k_beta = k * beta[..., None]
    lmat = jnp.where(strict, jnp.einsum('bhnid,bhnjd->bhnij', k_beta, k) * decay, 0.0)
    eye = jnp.eye(GDN_CHUNK, dtype=jnp.float32)
    tmat = lax.linalg.triangular_solve(lmat + eye, jnp.broadcast_to(eye, lmat.shape),
                                       left_side=True, lower=True, unit_diagonal=True)
    u = jnp.einsum('bhnij,bhnje->bhnie', tmat, v * beta[..., None])
    w = jnp.einsum('bhnij,bhnjd->bhnid', tmat, k_beta * jnp.exp(gc)[..., None])
    attn = jnp.einsum('bhnid,bhnjd->bhnij', q, k) * decay
    q_dec = q * jnp.exp(gc)[..., None]
    g_last = gc[..., -1]
    k_dec = k * jnp.exp(g_last[..., None] - gc)[..., None]

    def step(state, xs):
        u_c, w_c, attn_c, q_c, k_c, gl_c = xs
        v_new = u_c - jnp.einsum('bhid,bhde->bhie', w_c, state)
        out = jnp.einsum('bhid,bhde->bhie', q_c, state) + jnp.einsum('bhij,bhje->bhie', attn_c, v_new)
        state = state * jnp.exp(gl_c)[..., None, None] + jnp.einsum('bhid,bhie->bhde', k_c, v_new)
        return state, out

    xs = tuple(jnp.moveaxis(a, 2, 0) for a in (u, w, attn, q_dec, k_dec, g_last))
    s_final, out = lax.scan(step, s0.astype(jnp.float32), xs)
    out = jnp.moveaxis(out, 0, 2).reshape(bsz, heads, t_len, dv)
    return jnp.moveaxis(out, 1, 2), s_final


def gdn_branch(q, k, v, z, ba, conv_w, a_log, dt_bias, norm_w, init_states):
    dtype = q.dtype
    bsz, t_len, _ = q.shape
    qkv = jax.nn.silu(depthwise_conv_centred(jnp.concatenate([q, k, v], axis=-1), conv_w, axis=1))
    q, k, v = jnp.split(qkv, [GDN_QK_DIM, 2 * GDN_QK_DIM], axis=-1)
    rep = GDN_V_HEADS // GDN_QK_HEADS
    q = jnp.repeat(l2norm(q.reshape(bsz, t_len, GDN_QK_HEADS, GDN_HEAD_DIM)) * GDN_HEAD_DIM ** -0.5, rep, axis=2)
    k = jnp.repeat(l2norm(k.reshape(bsz, t_len, GDN_QK_HEADS, GDN_HEAD_DIM)), rep, axis=2)
    v = v.reshape(bsz, t_len, GDN_V_HEADS, GDN_HEAD_DIM)
    ba = ba.astype(jnp.float32).reshape(bsz, t_len, 2, 2, GDN_V_HEADS)
    beta = jax.nn.sigmoid(ba[:, :, :, 0])
    g = -jnp.exp(a_log) * jax.nn.softplus(ba[:, :, :, 1] + dt_bias)
    if init_states is None:
        zero = jnp.zeros((bsz, GDN_V_HEADS, GDN_HEAD_DIM, GDN_HEAD_DIM), jnp.float32)
        init_states = (zero, zero)
    s_fwd0, s_bwd0 = init_states
    o_f, s_f = gated_delta_chunked(q, k, v, g[:, :, 0], beta[:, :, 0], s_fwd0)
    flip = lambda a: jnp.flip(a, axis=1)
    o_b, s_b = gated_delta_chunked(flip(q), flip(k), flip(v), flip(g[:, :, 1]), flip(beta[:, :, 1]), s_bwd0)
    o = o_f + flip(o_b)
    gate = jax.nn.silu(z.astype(jnp.float32).reshape(bsz, t_len, GDN_V_HEADS, GDN_HEAD_DIM))
    o = rmsnorm(o, norm_w) * gate
    return o.reshape(bsz, t_len, GDN_V_DIM).astype(dtype), (s_f, s_b)


def shortconv_branch(xin, b_gate, c_gate, conv_w, grid_rows):
    bsz, t_len, _ = xin.shape
    v = c_gate * xin
    if grid_rows is None:
        y = depthwise_conv_centred(v, conv_w, axis=1)
    else:
        y = depthwise_conv_centred(v.reshape(bsz, grid_rows, GRID_W, SC_DIM), conv_w, axis=2)
    return b_gate * y.reshape(bsz, t_len, SC_DIM)


def pool_branch(u, pool_w, pool_scale, grid_rows):
    bsz, t_len, _ = u.shape
    ug = u if grid_rows is None else u.reshape(bsz, grid_rows, GRID_W, POOL_DIM)
    groups = jnp.split(ug, len(POOL_WINDOWS), axis=-1)
    outs = [jnp.einsum('...c,cd->...d', centred_window_mean(gu, win, axis=1) - gu, pool_w[i])
            for i, (gu, win) in enumerate(zip(groups, POOL_WINDOWS))]
    y = jnp.concatenate(outs, axis=-1) * pool_scale
    return y.reshape(bsz, t_len, POOL_DIM)


def mixer_output(gdn_y, xin, b_gate, c_gate, pool_in, conv_short, pool_w, pool_scale, w_out, grid_rows):
    y = jnp.concatenate([gdn_y,
                         shortconv_branch(xin, b_gate, c_gate, conv_short, grid_rows),
                         pool_branch(pool_in, pool_w, pool_scale, grid_rows)], axis=-1)
    return y @ w_out


def setup_inputs(seed: int = 0) -> dict:
    key = jax.random.key(seed)
    ks = jax.random.split(key, 24)
    f32 = jnp.float32
    L, D = DEPTH, D_MODEL

    def normal(k, shape, scale=1.0):
        return jax.random.normal(k, shape, f32) * scale

    def gain(k, shape):
        return 1.0 + 0.1 * jax.random.normal(k, shape, f32)

    dt = jnp.exp(jax.random.uniform(ks[12], (L, 2, GDN_V_HEADS), f32, math.log(1e-3), math.log(1e-1)))
    return {
        'x': normal(ks[0], (BATCH, SEQ, D)),
        'c': normal(ks[1], (BATCH, D)),
        'ctx': normal(ks[2], (BATCH, CTX_LEN, D)),
        'c_ctx': normal(ks[3], (D,)),
        'w_mod': normal(ks[4], (L, D, N_MOD * D), 0.5 * D ** -0.5),
        'b_mod': normal(ks[5], (L, N_MOD * D), 0.02),
        'norm_ffn1': gain(ks[6], (L, D)),
        'w_ffn1_gu': normal(ks[7], (L, D, 2 * D_FF), D ** -0.5),
        'w_ffn1_down': normal(ks[8], (L, D_FF, D), D_FF ** -0.5),
        'norm_mix': gain(ks[9], (L, D)),
        'w_in': normal(ks[10], (L, D, D_IN), D ** -0.5),
        'conv_qkv': normal(ks[11], (L, GDN_CONV, 2 * GDN_QK_DIM + GDN_V_DIM), GDN_CONV ** -0.5),
        'a_log': jnp.log(jax.random.uniform(ks[13], (L, 2, GDN_V_HEADS), f32, 1.0, 16.0)),
        'dt_bias': dt + jnp.log(-jnp.expm1(-dt)),
        'gdn_norm': gain(ks[14], (L, GDN_HEAD_DIM)),
        'conv_short': normal(ks[15], (L, SC_CONV, SC_DIM), SC_CONV ** -0.5),
        'pool_w': normal(ks[16], (L, len(POOL_WINDOWS), POOL_GROUP, POOL_GROUP), POOL_GROUP ** -0.5),
        'pool_scale': gain(ks[17], (L, POOL_DIM)),
        'w_out': normal(ks[18], (L, MIX_DIM, D), MIX_DIM ** -0.5),
        'norm_ffn2': gain(ks[19], (L, D)),
        'w_ffn2_gu': normal(ks[20], (L, D, 2 * D_FF), D ** -0.5),
        'w_ffn2_down': normal(ks[21], (L, D_FF, D), D_FF ** -0.5),
        'norm_final': gain(ks[22], (D,)),
    }


def reference(x, c, ctx, c_ctx, w_mod, b_mod, norm_ffn1, w_ffn1_gu, w_ffn1_down, norm_mix, w_in,
              conv_qkv, a_log, dt_bias, gdn_norm, conv_short, pool_w, pool_scale, w_out,
              norm_ffn2, w_ffn2_gu, w_ffn2_down, norm_final):
    rows = x.shape[1] // GRID_W
    for l in range(DEPTH):
        last = l == DEPTH - 1
        mod_x = (jax.nn.silu(c) @ w_mod[l] + b_mod[l]).reshape(-1, N_MOD, 1, D_MODEL)
        mod_c = (jax.nn.silu(c_ctx) @ w_mod[l] + b_mod[l]).reshape(N_MOD, D_MODEL)

        x = x + 0.5 * mod_x[:, 2] * swiglu(adaln(x, norm_ffn1[l], mod_x[:, 0], mod_x[:, 1]), w_ffn1_gu[l], w_ffn1_down[l])
        ctx = ctx + 0.5 * mod_c[2] * swiglu(adaln(ctx, norm_ffn1[l], mod_c[0], mod_c[1]), w_ffn1_gu[l], w_ffn1_down[l])

        pc = split_proj(adaln(ctx, norm_mix[l], mod_c[3], mod_c[4]) @ w_in[l])
        px = split_proj(adaln(x, norm_mix[l], mod_x[:, 3], mod_x[:, 4]) @ w_in[l])
        gdn_c, states_c = gdn_branch(pc[0], pc[1], pc[2], pc[3], pc[4], conv_qkv[l], a_log[l], dt_bias[l],
                                     gdn_norm[l], None)
        gdn_x, _ = gdn_branch(px[0], px[1], px[2], px[3], px[4], conv_qkv[l], a_log[l], dt_bias[l],
                              gdn_norm[l], states_c)
        x = x + mod_x[:, 5] * mixer_output(gdn_x, px[5], px[6], px[7], px[8], conv_short[l], pool_w[l],
                                           pool_scale[l], w_out[l], rows)
        if not last:
            ctx = ctx + mod_c[5] * mixer_output(gdn_c, pc[5], pc[6], pc[7], pc[8], conv_short[l], pool_w[l],
                                                pool_scale[l], w_out[l], None)
            ctx = ctx + 0.5 * mod_c[8] * swiglu(adaln(ctx, norm_ffn2[l], mod_c[6], mod_c[7]), w_ffn2_gu[l], w_ffn2_down[l])

        x = x + 0.5 * mod_x[:, 8] * swiglu(adaln(x, norm_ffn2[l], mod_x[:, 6], mod_x[:, 7]), w_ffn2_gu[l], w_ffn2_down[l])
    return rmsnorm(x, norm_final)
```

```python
import functools
import math

import numpy as np
import jax
import jax.numpy as jnp
from jax import lax
from jax.experimental import pallas as pl
from jax.experimental.pallas import tpu as pltpu

EPS = 1e-6
N_MOD = 9
GRID_W = 64
HEAD_DIM = 128
GDN_CONV = 5
SC_CONV = 3
POOL_WINDOWS = (2, 4, 8, 16)
CHUNK = 128
INV_BLOCK = 16
LANES = 128
SUBLANES = 8
VMEM_LIMIT = 56 * 1024 * 1024

BF16 = jnp.bfloat16
F32 = jnp.float32


def _cparams(*sem):
    return pltpu.CompilerParams(dimension_semantics=sem, vmem_limit_bytes=VMEM_LIMIT)


def _dot(a, b):
    return jnp.dot(a, b, preferred_element_type=F32)


def _dot_hi(a, b):
    return jnp.dot(a, b, preferred_element_type=F32, precision=lax.Precision.HIGHEST)


def _silu(v):
    return v * jax.nn.sigmoid(v)


def _mod_kernel(c_ref, w_ref, b_ref, o_ref):
    a = _silu(c_ref[...]).astype(BF16)
    o_ref[...] = _dot(a, w_ref[...].astype(BF16)) + b_ref[...]


def _modulation(c_all, w_mod, b_mod, tn=512):
    L, D, N = w_mod.shape
    return pl.pallas_call(
        _mod_kernel,
        grid=(L, N // tn),
        in_specs=[pl.BlockSpec((SUBLANES, D), lambda l, j: (0, 0)),
                  pl.BlockSpec((None, D, tn), lambda l, j: (l, 0, j)),
                  pl.BlockSpec((None, 1, tn), lambda l, j: (l, 0, j))],
        out_specs=pl.BlockSpec((None, SUBLANES, tn), lambda l, j: (l, 0, j)),
        out_shape=jax.ShapeDtypeStruct((L, SUBLANES, N), F32),
        compiler_params=_cparams("parallel", "parallel"),
        name="modulation",
    )(c_all, w_mod, b_mod)


def _adaln_mm_kernel(h_ref, nw_ref, mod_ref, *rest, shift_idx, scale_idx, swiglu):
    if swiglu:
        wg_ref, wu_ref, o_ref, hn_ref = rest
    else:
        w_ref, o_ref, hn_ref = rest

    @pl.when(pl.program_id(1) == 0)
    def _():
        h = h_ref[...]
        ms = jnp.mean(h * h, axis=-1, keepdims=True)
        y = h * lax.rsqrt(ms + EPS) * nw_ref[...]
        shift = mod_ref[0, shift_idx:shift_idx + 1, :]
        scale = mod_ref[0, scale_idx:scale_idx + 1, :]
        hn_ref[...] = (y * (1.0 + scale) + shift).astype(BF16)

    hn = hn_ref[...]
    if swiglu:
        g = _dot(hn, wg_ref[...])
        u = _dot(hn, wu_ref[...])
        o_ref[...] = (_silu(g) * u).astype(o_ref.dtype)
    else:
        o_ref[...] = _dot(hn, w_ref[...]).astype(o_ref.dtype)


def _adaln_matmul(h, norm_w, mods, w, *, seq_rows, n_batch, shift_idx, scale_idx, swiglu,
                  out_dtype, tm, tn, name):
    R, D = h.shape
    N = w.shape[1]
    n_out = N // 2 if swiglu else N
    nj = n_out // tn
    mod_map = lambda i, j: (jnp.minimum((i * tm) // seq_rows, n_batch), 0, 0)
    in_specs = [pl.BlockSpec((tm, D), lambda i, j: (i, 0)),
                pl.BlockSpec((1, D), lambda i, j: (0, 0)),
                pl.BlockSpec((1, N_MOD, D), mod_map),
                pl.BlockSpec((D, tn), lambda i, j: (0, j))]
    args = [h, norm_w, mods, w]
    if swiglu:
        in_specs.append(pl.BlockSpec((D, tn), lambda i, j: (0, j + nj)))
        args.append(w)
    return pl.pallas_call(
        functools.partial(_adaln_mm_kernel, shift_idx=shift_idx, scale_idx=scale_idx, swiglu=swiglu),
        grid=(R // tm, nj),
        in_specs=in_specs,
        out_specs=pl.BlockSpec((tm, tn), lambda i, j: (i, j)),
        out_shape=jax.ShapeDtypeStruct((R, n_out), out_dtype),
        scratch_shapes=[pltpu.VMEM((tm, D), BF16)],
        compiler_params=_cparams("parallel", "arbitrary"),
        name=name,
    )(*args)


def _mm_res_kernel(a_ref, w_ref, res_ref, mod_ref, o_ref, *, gate_idx, scale):
    acc = _dot(a_ref[...], w_ref[...])
    gate = mod_ref[0, gate_idx:gate_idx + 1, :]
    o_ref[...] = res_ref[...] + (scale * gate) * acc


def _matmul_residual(a, w, res, mods, *, seq_rows, n_batch, gate_idx, scale, tm, tn, name):
    R, K = a.shape
    D = w.shape[1]
    mod_map = lambda i, j: (jnp.minimum((i * tm) // seq_rows, n_batch), 0, j)
    return pl.pallas_call(
        functools.partial(_mm_res_kernel, gate_idx=gate_idx, scale=scale),
        grid=(R // tm, D // tn),
        in_specs=[pl.BlockSpec((tm, K), lambda i, j: (i, 0)),
                  pl.BlockSpec((K, tn), lambda i, j: (0, j)),
                  pl.BlockSpec((tm, tn), lambda i, j: (i, j)),
                  pl.BlockSpec((1, N_MOD, tn), mod_map)],
        out_specs=pl.BlockSpec((tm, tn), lambda i, j: (i, j)),
        out_shape=jax.ShapeDtypeStruct((R, D), F32),
        compiler_params=_cparams("parallel", "arbitrary"),
        name=name,
    )(a, w, res, mods)


def _rmsnorm_kernel(h_ref, w_ref, o_ref):
    h = h_ref[...]
    ms = jnp.mean(h * h, axis=-1, keepdims=True)
    o_ref[...] = h * lax.rsqrt(ms + EPS) * w_ref[...]


def _final_rmsnorm(h, w, rows, tm):
    D = h.shape[1]
    return pl.pallas_call(
        _rmsnorm_kernel,
        grid=(rows // tm,),
        in_specs=[pl.BlockSpec((tm, D), lambda i: (i, 0)),
                  pl.BlockSpec((1, D), lambda i: (0, 0))],
        out_specs=pl.BlockSpec((tm, D), lambda i: (i, 0)),
        out_shape=jax.ShapeDtypeStruct((rows, D), F32),
        compiler_params=_cparams("parallel"),
        name="final_rmsnorm",
    )(h, w)


def _segment_position(tile_rows, bs_rows, seq, ctx_len):
    row0 = pl.program_id(0) * tile_rows
    in_x = row0 < bs_rows
    off = jnp.where(in_x, lax.rem(row0, seq), lax.rem(row0 - bs_rows, ctx_len))
    seglen = jnp.where(in_x, seq, ctx_len)
    return in_x, off, seglen


def _qkv_conv_kernel(cur_ref, prev_ref, next_ref, w_ref, *rest, kind, tr, bs_rows, seq, ctx_len):
    if kind == 'k':
        o_ref, ot_ref, ext_ref = rest
    else:
        o_ref, ext_ref = rest
    _, off, seglen = _segment_position(tr, bs_rows, seq, ctx_len)
    is_start = off == 0
    is_end = off + tr == seglen
    pad = SUBLANES
    half = GDN_CONV // 2
    ext_ref[0:pad, :] = jnp.where(is_start, 0.0, prev_ref[...])
    ext_ref[pad:pad + tr, :] = cur_ref[...]
    ext_ref[pad + tr:pad + tr + pad, :] = jnp.where(is_end, 0.0, next_ref[...])
    acc = None
    for j in range(GDN_CONV):
        term = ext_ref[pad - half + j:pad - half + j + tr, :] * w_ref[j:j + 1, :]
        acc = term if acc is None else acc + term
    y = _silu(acc)
    tc = y.shape[1]
    if kind == 'v':
        o_ref[...] = y.astype(o_ref.dtype)
        return
    mult = HEAD_DIM ** -0.5 if kind == 'q' else 1.0
    for hd in range(tc // HEAD_DIM):
        yh = y[:, hd * HEAD_DIM:(hd + 1) * HEAD_DIM]
        inv = lax.rsqrt(jnp.sum(yh * yh, axis=-1, keepdims=True) + EPS)
        yn = yh * inv
        if kind == 'q':
            yn = yn * mult
        o_ref[:, hd * HEAD_DIM:(hd + 1) * HEAD_DIM] = yn.astype(o_ref.dtype)
        if kind == 'k':
            ot_ref[hd * HEAD_DIM:(hd + 1) * HEAD_DIM, :] = yn.T.astype(ot_ref.dtype)


def _qkv_conv(px, conv_w, *, kind, col0, width, tr, tc, bs_rows, seq, ctx_len):
    R = px.shape[0]
    cb0 = col0 // tc
    rb = tr // SUBLANES
    last = R // SUBLANES - 1
    in_specs = [pl.BlockSpec((tr, tc), lambda i, j: (i, cb0 + j)),
                pl.BlockSpec((SUBLANES, tc), lambda i, j: (jnp.maximum(i * rb - 1, 0), cb0 + j)),
                pl.BlockSpec((SUBLANES, tc), lambda i, j: (jnp.minimum((i + 1) * rb, last), cb0 + j)),
                pl.BlockSpec((GDN_CONV, tc), lambda i, j: (0, j))]
    out_specs = [pl.BlockSpec((tr, tc), lambda i, j: (i, j))]
    out_shape = [jax.ShapeDtypeStruct((R, width), BF16)]
    if kind == 'k':
        out_specs.append(pl.BlockSpec((tc, tr), lambda i, j: (j, i)))
        out_shape.append(jax.ShapeDtypeStruct((width, R), BF16))
    res = pl.pallas_call(
        functools.partial(_qkv_conv_kernel, kind=kind, tr=tr, bs_rows=bs_rows, seq=seq, ctx_len=ctx_len),
        grid=(R // tr, width // tc),
        in_specs=in_specs,
        out_specs=out_specs,
        out_shape=out_shape,
        scratch_shapes=[pltpu.VMEM((tr + 2 * SUBLANES, tc), F32)],
        compiler_params=_cparams("parallel", "parallel"),
        name="gdn_conv_" + kind,
    )(px, px, px, conv_w)
    return res


def _gate_kernel(ba_ref, alog_ref, dtb_ref, cols_ref, rows_ref, *, tm, n_pairs):
    lane = lax.broadcasted_iota(jnp.int32, (CHUNK, LANES), 1)
    is_beta = (lane & 1) == 0
    is_bwd = (lane & 4) != 0
    ri = lax.broadcasted_iota(jnp.int32, (CHUNK, CHUNK), 0)
    ci = lax.broadcasted_iota(jnp.int32, (CHUNK, CHUNK), 1)
    tri_lo = jnp.where(ci <= ri, 1.0, 0.0)
    tri_up = jnp.where(ci >= ri, 1.0, 0.0)
    neg_a = -jnp.exp(alog_ref[...])
    for c in range(tm // CHUNK):
        sl = slice(c * CHUNK, (c + 1) * CHUNK)
        x = ba_ref[sl, :]
        beta = jax.nn.sigmoid(x)
        y = x + dtb_ref[...]
        softplus = jnp.maximum(y, 0.0) + jnp.log1p(jnp.exp(-jnp.abs(y)))
        g = jnp.where(is_beta, 0.0, neg_a * softplus)
        gc = jnp.where(is_bwd, _dot_hi(tri_up, g), _dot_hi(tri_lo, g))
        full = jnp.where(is_beta, beta, gc)
        for p in range(n_pairs):
            cols_ref[p, sl, :] = full[:, p * 8:(p + 1) * 8]
        rows_ref[:, sl] = full.T[0:n_pairs * 8, :]


def _gdn_gates(ba, alog_row, dtb_row, *, n_pairs, tm):
    R = ba.shape[0]
    return pl.pallas_call(
        functools.partial(_gate_kernel, tm=tm, n_pairs=n_pairs),
        grid=(R // tm,),
        in_specs=[pl.BlockSpec((tm, LANES), lambda i: (i, 0)),
                  pl.BlockSpec((1, LANES), lambda i: (0, 0)),
                  pl.BlockSpec((1, LANES), lambda i: (0, 0))],
        out_specs=[pl.BlockSpec((n_pairs, tm, 8), lambda i: (0, i, 0)),
                   pl.BlockSpec((n_pairs * 8, tm), lambda i: (0, i))],
        out_shape=[jax.ShapeDtypeStruct((n_pairs, R, 8), F32),
                   jax.ShapeDtypeStruct((n_pairs * 8, R), F32)],
        compiler_params=_cparams("parallel"),
        name="gdn_gates",
    )(ba, alog_row, dtb_row)


def _gdn_kernel(qx_ref, qc_ref, kx_ref, kc_ref, ktx_ref, ktc_ref, vx_ref, vc_ref,
                colx_ref, colc_ref, rowx_ref, rowc_ref, z_ref, nw_ref, o_ref,
                t_ref, a_ref, of_ref, ob_ref, s_ref, *, seq, ctx_len, rb):
    C = CHUNK
    HD = HEAD_DIM
    n_xc = seq // C
    n_cc = ctx_len // C
    r = pl.program_id(2)

    ri = lax.broadcasted_iota(jnp.int32, (C, C), 0)
    ci = lax.broadcasted_iota(jnp.int32, (C, C), 1)
    eye = (ri == ci).astype(F32)

    def lane_idx(d, s, kind):
        return d * 4 + s * 2 + kind

    def same_block(size):
        shift = int(math.log2(size))
        return (ri >> shift) == (ci >> shift)

    def unit_triangular_inverse(nmat):
        n0 = jnp.where(same_block(INV_BLOCK), nmat, 0.0)
        inv = eye + n0
        pw = n0
        for _ in range(int(math.log2(INV_BLOCK)) - 1):
            pwb = pw.astype(BF16)
            pw = _dot(pwb, pwb)
            inv = inv + _dot(inv.astype(BF16), pw.astype(BF16))
        size = INV_BLOCK
        while size < C:
            off = jnp.where(same_block(2 * size) & jnp.logical_not(same_block(size)), nmat, 0.0)
            invb = inv.astype(BF16)
            inv = inv + _dot(_dot(invb, off.astype(BF16)).astype(BF16), invb)
            size *= 2
        return inv

    def prepare(q_ref, k_ref, kt_ref, col_ref, row_ref, row0, slot):
        rows = pl.ds(row0, C)
        qk = jnp.concatenate([q_ref[rows, :], k_ref[rows, :]], axis=0)
        gram = _dot(qk, kt_ref[:, rows])
        qkt = gram[0:C, :]
        kkt = gram[C:2 * C, :]
        cols = col_ref[rows, :]
        for d in range(2):
            incl = (ci <= ri) if d == 0 else (ci >= ri)
            strict = (ci < ri) if d == 0 else (ci > ri)
            for s in range(2):
                kb, kg = lane_idx(d, s, 0), lane_idx(d, s, 1)
                beta_col = cols[:, kb:kb + 1]
                gc_col = cols[:, kg:kg + 1]
                gc_row = row_ref[kg:kg + 1, rows]
                decay = jnp.where(incl, jnp.exp(jnp.where(incl, gc_col - gc_row, 0.0)), 0.0)
                nmat = jnp.where(strict, -(kkt * beta_col) * decay, 0.0)
                t_ref[slot, d * 2 + s] = unit_triangular_inverse(nmat).astype(BF16)
                a_ref[slot, d * 2 + s] = (qkt * decay).astype(BF16)

    def chain(d, q_ref, k_ref, kt_ref, v_ref, col_ref, row0, slot, orow0):
        rows = pl.ds(row0, C)
        o_acc = of_ref if d == 0 else ob_ref
        state = s_ref[d]
        qk = jnp.concatenate([q_ref[rows, :], k_ref[rows, :]], axis=0)
        proj = _dot(qk, state.astype(BF16))
        cols = col_ref[rows, :]
        evs, gls = [], []
        for s in range(2):
            kb, kg = lane_idx(d, s, 0), lane_idx(d, s, 1)
            beta_col = cols[:, kb:kb + 1]
            gc_col = cols[:, kg:kg + 1]
            g_last = gc_col[C - 1:C, :] if d == 0 else gc_col[0:1, :]
            e_col = jnp.exp(gc_col)
            hs = slice(s * HD, (s + 1) * HD)
            q_s = proj[0:C, hs]
            k_s = proj[C:2 * C, hs]
            v = v_ref[rows, hs].astype(F32)
            rhs = beta_col * (v - e_col * k_s)
            v_new = _dot(t_ref[slot, d * 2 + s], rhs.astype(BF16))
            out = e_col * q_s + _dot(a_ref[slot, d * 2 + s], v_new.astype(BF16))
            o_acc[pl.ds(orow0, C), hs] = out
            evs.append(jnp.exp(g_last - gc_col) * v_new)
            gls.append(jnp.broadcast_to(jnp.exp(g_last), (HD, HD)))
        ev = jnp.concatenate(evs, axis=1).astype(BF16)
        s_ref[d] = state * jnp.concatenate(gls, axis=1) + _dot(kt_ref[:, rows], ev)

    @pl.when(r == 0)
    def _():
        s_ref[...] = jnp.zeros_like(s_ref)
        for n in range(n_cc):
            prepare(qc_ref, kc_ref, ktc_ref, colc_ref, rowc_ref, n * C, n)

        def prep_body(n, carry):
            prepare(qx_ref, kx_ref, ktx_ref, colx_ref, rowx_ref, pl.multiple_of(n * C, C), n_cc + n)
            return carry
        lax.fori_loop(0, n_xc, prep_body, 0)

        for n in range(n_cc):
            m = n_cc - 1 - n
            chain(0, qc_ref, kc_ref, ktc_ref, vc_ref, colc_ref, n * C, n, seq + n * C)
            chain(1, qc_ref, kc_ref, ktc_ref, vc_ref, colc_ref, m * C, m, seq + m * C)

        def chain_body(n, carry):
            m = n_xc - 1 - n
            rf = pl.multiple_of(n * C, C)
            rbk = pl.multiple_of(m * C, C)
            chain(0, qx_ref, kx_ref, ktx_ref, vx_ref, colx_ref, rf, n_cc + n, rf)
            chain(1, qx_ref, kx_ref, ktx_ref, vx_ref, colx_ref, rbk, n_cc + m, rbk)
            return carry
        lax.fori_loop(0, n_xc, chain_body, 0)

    orow = pl.multiple_of(r * rb, rb)
    o = of_ref[pl.ds(orow, rb), :] + ob_ref[pl.ds(orow, rb), :]
    z = z_ref[...]
    for s in range(2):
        hs = slice(s * HD, (s + 1) * HD)
        oh = o[:, hs]
        ms = jnp.mean(oh * oh, axis=-1, keepdims=True)
        y = oh * lax.rsqrt(ms + EPS) * nw_ref[...]
        o_ref[:, hs] = (y * _silu(z[:, hs])).astype(o_ref.dtype)


def _gdn(qn, kn, knt, vn, cols, rows, px, z_col0, norm_w, *, n_batch, seq, ctx_len, out_cols):
    R = qn.shape[0]
    n_pairs = qn.shape[1] // HEAD_DIM
    C = CHUNK
    rb = min(256, ctx_len)
    nxb, ncb = seq // rb, ctx_len // rb
    cblk0 = n_batch * (seq // ctx_len)
    zb0 = z_col0 // (2 * HEAD_DIM)
    n_slots = (seq + ctx_len) // C

    def row_block(b, r):
        return jnp.where(r < nxb, b * nxb + r, n_batch * nxb + b * ncb + (r - nxb))

    x_rows = lambda w: pl.BlockSpec((seq, w), lambda b, p, r: (b, p))
    c_rows = lambda w: pl.BlockSpec((ctx_len, w), lambda b, p, r: (cblk0 + b, p))
    in_specs = [
        x_rows(HEAD_DIM), c_rows(HEAD_DIM),
        x_rows(HEAD_DIM), c_rows(HEAD_DIM),
        pl.BlockSpec((HEAD_DIM, seq), lambda b, p, r: (p, b)),
        pl.BlockSpec((HEAD_DIM, ctx_len), lambda b, p, r: (p, cblk0 + b)),
        x_rows(2 * HEAD_DIM), c_rows(2 * HEAD_DIM),
        pl.BlockSpec((None, seq, 8), lambda b, p, r: (p, b, 0)),
        pl.BlockSpec((None, ctx_len, 8), lambda b, p, r: (p, cblk0 + b, 0)),
        pl.BlockSpec((8, seq), lambda b, p, r: (p, b)),
        pl.BlockSpec((8, ctx_len), lambda b, p, r: (p, cblk0 + b)),
        pl.BlockSpec((rb, 2 * HEAD_DIM), lambda b, p, r: (row_block(b, r), zb0 + p)),
        pl.BlockSpec((1, HEAD_DIM), lambda b, p, r: (0, 0)),
    ]
    return pl.pallas_call(
        functools.partial(_gdn_kernel, seq=seq, ctx_len=ctx_len, rb=rb),
        grid=(n_batch, n_pairs, nxb + ncb),
        in_specs=in_specs,
        out_specs=pl.BlockSpec((rb, 2 * HEAD_DIM), lambda b, p, r: (row_block(b, r), p)),
        out_shape=jax.ShapeDtypeStruct((R, out_cols), BF16),
        scratch_shapes=[pltpu.VMEM((n_slots, 4, C, C), BF16),
                        pltpu.VMEM((n_slots, 4, C, C), BF16),
                        pltpu.VMEM((seq + ctx_len, 2 * HEAD_DIM), F32),
                        pltpu.VMEM((seq + ctx_len, 2 * HEAD_DIM), F32),
                        pltpu.VMEM((2, HEAD_DIM, 2 * HEAD_DIM), F32)],
        compiler_params=_cparams("parallel", "parallel", "arbitrary"),
        name="gdn_delta_rule",
    )(qn, qn, kn, kn, knt, knt, vn, vn, cols, cols, rows, rows, px, norm_w)


def _shortconv_kernel(x_ref, b_ref, c_ref, w_ref, ymix_ref, o_ref, ext_ref, *, tr, bs_rows, seq, ctx_len):
    del ymix_ref
    in_x, off, _ = _segment_position(tr, bs_rows, seq, ctx_len)
    period = jnp.where(in_x, GRID_W, ctx_len)
    pos = (off + lax.broadcasted_iota(jnp.int32, (tr, 1), 0)) & (period - 1)
    pad = SUBLANES
    v = c_ref[...] * x_ref[...]
    ext_ref[0:pad, :] = jnp.zeros((pad, v.shape[1]), F32)
    ext_ref[pad:pad + tr, :] = v
    ext_ref[pad + tr:pad + tr + pad, :] = jnp.zeros((pad, v.shape[1]), F32)
    left = jnp.where(pos == 0, 0.0, ext_ref[pad - 1:pad - 1 + tr, :])
    right = jnp.where(pos == period - 1, 0.0, ext_ref[pad + 1:pad + 1 + tr, :])
    y = left * w_ref[0:1, :] + v * w_ref[1:2, :] + right * w_ref[2:3, :]
    o_ref[...] = (b_ref[...] * y).astype(o_ref.dtype)


def _shortconv(px, conv_w, ymix, *, x_col0, width, out_col0, tr, tc, bs_rows, seq, ctx_len):
    R = px.shape[0]
    nb = width // tc
    xb0 = x_col0 // tc
    ob0 = out_col0 // tc
    return pl.pallas_call(
        functools.partial(_shortconv_kernel, tr=tr, bs_rows=bs_rows, seq=seq, ctx_len=ctx_len),
        grid=(R // tr, nb),
        in_specs=[pl.BlockSpec((tr, tc), lambda i, j: (i, xb0 + j)),
                  pl.BlockSpec((tr, tc), lambda i, j: (i, xb0 + nb + j)),
                  pl.BlockSpec((tr, tc), lambda i, j: (i, xb0 + 2 * nb + j)),
                  pl.BlockSpec((SC_CONV, tc), lambda i, j: (0, j)),
                  pl.BlockSpec(memory_space=pl.ANY)],
        out_specs=pl.BlockSpec((tr, tc), lambda i, j: (i, ob0 + j)),
        out_shape=jax.ShapeDtypeStruct(ymix.shape, ymix.dtype),
        input_output_aliases={4: 0},
        scratch_shapes=[pltpu.VMEM((tr + 2 * SUBLANES, tc), F32)],
        compiler_params=_cparams("parallel", "parallel"),
        name="short_conv",
    )(px, px, px, conv_w, ymix)


def _pool_kernel(u_ref, w_ref, sc_ref, ymix_ref, o_ref, pa_ref, pb_ref, *, n_tok, unit, pg):
    del ymix_ref
    n_pos = n_tok // unit
    reach = 8 * unit
    pad = -(-reach // SUBLANES) * SUBLANES
    mg = -(-4 * unit // SUBLANES) * SUBLANES
    tp = n_tok + 2 * pad
    zeros_mg = jnp.zeros((mg, pg), F32)
    zeros_pad = jnp.zeros((pad, pg), F32)
    for buf in (pa_ref, pb_ref):
        buf[0:mg, :] = zeros_mg
        buf[mg + tp:mg + tp + mg, :] = zeros_mg
    row = lax.broadcasted_iota(jnp.int32, (n_tok, 1), 0) // unit
    for g, win in enumerate(POOL_WINDOWS):
        cs = slice(g * pg, (g + 1) * pg)
        u = u_ref[:, cs]
        pa_ref[mg:mg + pad, :] = zeros_pad
        pa_ref[mg + pad:mg + pad + n_tok, :] = u
        pa_ref[mg + pad + n_tok:mg + tp, :] = zeros_pad
        src, dst = pa_ref, pb_ref
        dst[mg:mg + tp, :] = src[mg - unit:mg - unit + tp, :] + src[mg:mg + tp, :]
        src, dst = dst, src
        step, w = 1, 2
        while w < win:
            dst[mg:mg + tp, :] = (src[mg - step * unit:mg - step * unit + tp, :]
                                  + src[mg + step * unit:mg + step * unit + tp, :])
            src, dst = dst, src
            step, w = step * 2, w * 2
        total = src[mg + pad:mg + pad + n_tok, :]
        lo = jnp.maximum(row - win // 2, 0)
        hi = jnp.minimum(row - win // 2 + win, n_pos)
        count = (hi - lo).astype(F32)
        dlt = total / count - u
        y = _dot(dlt.astype(BF16), w_ref[g]) * sc_ref[:, cs]
        o_ref[:, cs] = y.astype(o_ref.dtype)


def _pool(px, pool_w, pool_scale, ymix, *, u_col0, out_col0, n_tok, unit, blk0, n_batch, name):
    width = pool_scale.shape[1]
    pg = width // len(POOL_WINDOWS)
    reach = 8 * unit
    pad = -(-reach // SUBLANES) * SUBLANES
    mg = -(-4 * unit // SUBLANES) * SUBLANES
    buf_rows = n_tok + 2 * pad + 2 * mg
    ub = u_col0 // width
    ob = out_col0 // width
    return pl.pallas_call(
        functools.partial(_pool_kernel, n_tok=n_tok, unit=unit, pg=pg),
        grid=(n_batch,),
        in_specs=[pl.BlockSpec((n_tok, width), lambda b: (blk0 + b, ub)),
                  pl.BlockSpec(pool_w.shape, lambda b: (0, 0, 0)),
                  pl.BlockSpec((1, width), lambda b: (0, 0)),
                  pl.BlockSpec(memory_space=pl.ANY)],
        out_specs=pl.BlockSpec((n_tok, width), lambda b: (blk0 + b, ob)),
        out_shape=jax.ShapeDtypeStruct(ymix.shape, ymix.dtype),
        input_output_aliases={3: 0},
        scratch_shapes=[pltpu.VMEM((buf_rows, pg), F32), pltpu.VMEM((buf_rows, pg), F32)],
        compiler_params=_cparams("parallel"),
        name=name,
    )(px, pool_w, pool_scale, ymix)


def _ba_permutation(n_heads):
    perm = np.zeros(4 * n_heads, np.int32)
    for p in range(n_heads // 2):
        for d in range(2):
            for s in range(2):
                for kind in range(2):
                    perm[p * 8 + d * 4 + s * 2 + kind] = d * 2 * n_heads + kind * n_heads + 2 * p + s
    return perm


def _gate_param_row(param, n_heads):
    row = jnp.zeros((LANES,), F32)
    idx, src_d, src_h = [], [], []
    for p in range(n_heads // 2):
        for d in range(2):
            for s in range(2):
                idx.append(p * 8 + d * 4 + s * 2 + 1)
                src_d.append(d)
                src_h.append(2 * p + s)
    row = row.at[np.array(idx)].set(param[np.array(src_d), np.array(src_h)])
    return row.reshape(1, LANES)


def _pick_tile(n, prefs):
    for t in prefs:
        if n % t == 0:
            return t
    raise ValueError(f"no tile for {n}")


def kernel(x, c, ctx, c_ctx, w_mod, b_mod, norm_ffn1, w_ffn1_gu, w_ffn1_down, norm_mix, w_in, conv_qkv, a_log,
           dt_bias, gdn_norm, conv_short, pool_w, pool_scale, w_out, norm_ffn2, w_ffn2_gu, w_ffn2_down, norm_final):
    B, S, D = x.shape
    CL = ctx.shape[1]
    L = w_mod.shape[0]
    n_vheads = a_log.shape[2]
    qk_dim = (n_vheads // 2) * HEAD_DIM
    v_dim = n_vheads * HEAD_DIM
    sc_dim = conv_short.shape[2]
    pool_dim = pool_scale.shape[1]
    n_ba = 4 * n_vheads
    n_pairs = n_vheads // 2
    bs_rows = B * S
    R = bs_rows + B * CL
    assert S % CHUNK == 0 and CL % CHUNK == 0 and S % CL == 0 and n_ba <= LANES

    tm = _pick_tile(math.gcd(S, R), (512, 256, 128))
    tr = min(256, CL)
    tn = 512 if D % 512 == 0 else 256

    z_col0 = 2 * qk_dim + v_dim
    ba_col0 = z_col0 + v_dim
    sc_col0_src = ba_col0 + n_ba
    sc_col0 = ba_col0
    pool_col0 = sc_col0 + 3 * sc_dim
    ba_perm = _ba_permutation(n_vheads)

    h = jnp.concatenate([x.reshape(bs_rows, D), ctx.reshape(B * CL, D)], axis=0)
    c_all = jnp.zeros((SUBLANES, D), F32).at[0:B].set(c).at[B].set(c_ctx)
    mods_all = _modulation(c_all, w_mod, b_mod.reshape(L, 1, N_MOD * D)).reshape(L, SUBLANES, N_MOD, D)

    common = dict(seq_rows=S, n_batch=B)
    seg = dict(bs_rows=bs_rows, seq=S, ctx_len=CL)
    for l in range(L):
        mods = mods_all[l]
        w1gu = w_ffn1_gu[l].astype(BF16)
        w1d = w_ffn1_down[l].astype(BF16)
        w2gu = w_ffn2_gu[l].astype(BF16)
        w2d = w_ffn2_down[l].astype(BF16)
        w_main = jnp.concatenate([w_in[l][:, 0:ba_col0], w_in[l][:, sc_col0_src:]], axis=1).astype(BF16)
        w_ba = jnp.zeros((D, LANES), BF16).at[:, 0:n_ba].set(w_in[l][:, ba_col0 + ba_perm].astype(BF16))
        wo = w_out[l].astype(BF16)

        act = _adaln_matmul(h, norm_ffn1[l].reshape(1, D), mods, w1gu, shift_idx=0, scale_idx=1, swiglu=True,
                            out_dtype=BF16, tm=tm, tn=tn, name="ffn1_up", **common)
        h = _matmul_residual(act, w1d, h, mods, gate_idx=2, scale=0.5, tm=tm, tn=tn, name="ffn1_down", **common)

        nmix = norm_mix[l].reshape(1, D)
        px = _adaln_matmul(h, nmix, mods, w_main, shift_idx=3, scale_idx=4, swiglu=False,
                           out_dtype=F32, tm=tm, tn=tn, name="in_proj", **common)
        ba = _adaln_matmul(h, nmix, mods, w_ba, shift_idx=3, scale_idx=4, swiglu=False,
                           out_dtype=F32, tm=tm, tn=LANES, name="in_proj_ba", **common)

        tc = _pick_tile(qk_dim, (512, 256, 128))
        conv = dict(tr=tr, tc=tc, **seg)
        (qn,) = _qkv_conv(px, conv_qkv[l][:, 0:qk_dim], kind='q', col0=0, width=qk_dim, **conv)
        kn, knt = _qkv_conv(px, conv_qkv[l][:, qk_dim:2 * qk_dim], kind='k', col0=qk_dim, width=qk_dim, **conv)
        (vn,) = _qkv_conv(px, conv_qkv[l][:, 2 * qk_dim:], kind='v', col0=2 * qk_dim, width=v_dim, **conv)
        cols, rows = _gdn_gates(ba, _gate_param_row(a_log[l], n_vheads), _gate_param_row(dt_bias[l], n_vheads),
                                n_pairs=n_pairs, tm=tm)
        ymix = _gdn(qn, kn, knt, vn, cols, rows, px, z_col0, gdn_norm[l].reshape(1, HEAD_DIM),
                    n_batch=B, seq=S, ctx_len=CL, out_cols=v_dim + sc_dim + pool_dim)

        ymix = _shortconv(px, conv_short[l], ymix, x_col0=sc_col0, width=sc_dim, out_col0=v_dim,
                          tr=tr, tc=_pick_tile(sc_dim, (512, 256, 128)), **seg)
        pw = pool_w[l].astype(BF16)
        psc = pool_scale[l].reshape(1, pool_dim)
        ymix = _pool(px, pw, psc, ymix, u_col0=pool_col0, out_col0=v_dim + sc_dim, n_tok=S, unit=GRID_W,
                     blk0=0, n_batch=B, name="pool_latent")
        ymix = _pool(px, pw, psc, ymix, u_col0=pool_col0, out_col0=v_dim + sc_dim, n_tok=CL, unit=1,
                     blk0=bs_rows // CL, n_batch=B, name="pool_context")
        h = _matmul_residual(ymix, wo, h, mods, gate_idx=5, scale=1.0, tm=tm, tn=tn, name="mix_out", **common)

        act = _adaln_matmul(h, norm_ffn2[l].reshape(1, D), mods, w2gu, shift_idx=6, scale_idx=7, swiglu=True,
                            out_dtype=BF16, tm=tm, tn=tn, name="ffn2_up", **common)
        h = _matmul_residual(act, w2d, h, mods, gate_idx=8, scale=0.5, tm=tm, tn=tn, name="ffn2_down", **common)

    out = _final_rmsnorm(h, norm_final.reshape(1, D), bs_rows, tm)
    return out.reshape(B, S, D)
```

```python
import functools
import math

import numpy as np
import jax
import jax.numpy as jnp
from jax import lax
from jax.experimental import pallas as pl
from jax.experimental.pallas import tpu as pltpu

EPS = 1e-6
N_MOD = 9
GRID_W = 64
HEAD_DIM = 128
GDN_CONV = 5
SC_CONV = 3
POOL_WINDOWS = (2, 4, 8, 16)
CHUNK = 128
INV_BLOCK = 16
LANES = 128
SUBLANES = 8
VMEM_LIMIT = 56 * 1024 * 1024

BF16 = jnp.bfloat16
F32 = jnp.float32


def _cparams(*sem):
    return pltpu.CompilerParams(dimension_semantics=sem, vmem_limit_bytes=VMEM_LIMIT)


def _dot(a, b):
    return jnp.dot(a, b, preferred_element_type=F32)


def _dot_hi(a, b):
    return jnp.dot(a, b, preferred_element_type=F32, precision=lax.Precision.HIGHEST)


def _silu(v):
    return v * jax.nn.sigmoid(v)


def _mod_kernel(c_ref, w_ref, b_ref, o_ref):
    a = _silu(c_ref[...]).astype(BF16)
    o_ref[...] = _dot(a, w_ref[...].astype(BF16)) + b_ref[...]


def _modulation(c_all, w_mod, b_mod, tn=512):
    L, D, N = w_mod.shape
    return pl.pallas_call(
        _mod_kernel,
        grid=(L, N // tn),
        in_specs=[pl.BlockSpec((SUBLANES, D), lambda l, j: (0, 0)),
                  pl.BlockSpec((None, D, tn), lambda l, j: (l, 0, j)),
                  pl.BlockSpec((None, 1, tn), lambda l, j: (l, 0, j))],
        out_specs=pl.BlockSpec((None, SUBLANES, tn), lambda l, j: (l, 0, j)),
        out_shape=jax.ShapeDtypeStruct((L, SUBLANES, N), F32),
        compiler_params=_cparams("parallel", "parallel"),
        name="modulation",
    )(c_all, w_mod, b_mod)


def _adaln_kernel(h_ref, nw_ref, mod_ref, o_ref, *, shift_idx, scale_idx):
    h = h_ref[...]
    ms = jnp.mean(h * h, axis=-1, keepdims=True)
    y = h * lax.rsqrt(ms + EPS) * nw_ref[...]
    shift = mod_ref[0, shift_idx:shift_idx + 1, :]
    scale = mod_ref[0, scale_idx:scale_idx + 1, :]
    o_ref[...] = (y * (1.0 + scale) + shift).astype(o_ref.dtype)


def _adaln(h, norm_w, mods, *, n_rows, seq_rows, n_batch, shift_idx, scale_idx, tr, name):
    D = h.shape[1]
    mod_map = lambda i: (jnp.minimum((i * tr) // seq_rows, n_batch), 0, 0)
    return pl.pallas_call(
        functools.partial(_adaln_kernel, shift_idx=shift_idx, scale_idx=scale_idx),
        grid=(n_rows // tr,),
        in_specs=[pl.BlockSpec((tr, D), lambda i: (i, 0)),
                  pl.BlockSpec((1, D), lambda i: (0, 0)),
                  pl.BlockSpec((1, N_MOD, D), mod_map)],
        out_specs=pl.BlockSpec((tr, D), lambda i: (i, 0)),
        out_shape=jax.ShapeDtypeStruct((n_rows, D), BF16),
        compiler_params=_cparams("parallel"),
        name=name,
    )(h, norm_w, mods)


def _mm_kernel(a_ref, *rest, swiglu):
    a = a_ref[...]
    if swiglu:
        wg_ref, wu_ref, o_ref = rest
        g = _dot(a, wg_ref[...])
        u = _dot(a, wu_ref[...])
        o_ref[...] = (_silu(g) * u).astype(o_ref.dtype)
    else:
        w_ref, o_ref = rest
        o_ref[...] = _dot(a, w_ref[...]).astype(o_ref.dtype)


def _matmul(a, w, *, swiglu, out_dtype, tm, tn, name):
    R, K = a.shape
    N = w.shape[1]
    n_out = N // 2 if swiglu else N
    nj = n_out // tn
    in_specs = [pl.BlockSpec((tm, K), lambda i, j: (i, 0)),
                pl.BlockSpec((K, tn), lambda i, j: (0, j))]
    args = [a, w]
    if swiglu:
        in_specs.append(pl.BlockSpec((K, tn), lambda i, j: (0, j + nj)))
        args.append(w)
    return pl.pallas_call(
        functools.partial(_mm_kernel, swiglu=swiglu),
        grid=(R // tm, nj),
        in_specs=in_specs,
        out_specs=pl.BlockSpec((tm, tn), lambda i, j: (i, j)),
        out_shape=jax.ShapeDtypeStruct((R, n_out), out_dtype),
        compiler_params=_cparams("parallel", "parallel"),
        name=name,
    )(*args)


def _mm_res_kernel(a_ref, w_ref, res_ref, mod_ref, o_ref, *, gate_idx, scale):
    acc = _dot(a_ref[...], w_ref[...])
    gate = mod_ref[0, gate_idx:gate_idx + 1, :]
    o_ref[...] = res_ref[...] + (scale * gate) * acc


def _matmul_residual(a, w, res, mods, *, n_rows, seq_rows, n_batch, gate_idx, scale, tm, tn, name):
    K = a.shape[1]
    D = w.shape[1]
    mod_map = lambda i, j: (jnp.minimum((i * tm) // seq_rows, n_batch), 0, j)
    return pl.pallas_call(
        functools.partial(_mm_res_kernel, gate_idx=gate_idx, scale=scale),
        grid=(n_rows // tm, D // tn),
        in_specs=[pl.BlockSpec((tm, K), lambda i, j: (i, 0)),
                  pl.BlockSpec((K, tn), lambda i, j: (0, j)),
                  pl.BlockSpec((tm, tn), lambda i, j: (i, j)),
                  pl.BlockSpec((1, N_MOD, tn), mod_map)],
        out_specs=pl.BlockSpec((tm, tn), lambda i, j: (i, j)),
        out_shape=jax.ShapeDtypeStruct((n_rows, D), F32),
        compiler_params=_cparams("parallel", "parallel"),
        name=name,
    )(a, w, res, mods)


def _rmsnorm_kernel(h_ref, w_ref, o_ref):
    h = h_ref[...]
    ms = jnp.mean(h * h, axis=-1, keepdims=True)
    o_ref[...] = h * lax.rsqrt(ms + EPS) * w_ref[...]


def _final_rmsnorm(h, w, rows, tm):
    D = h.shape[1]
    return pl.pallas_call(
        _rmsnorm_kernel,
        grid=(rows // tm,),
        in_specs=[pl.BlockSpec((tm, D), lambda i: (i, 0)),
                  pl.BlockSpec((1, D), lambda i: (0, 0))],
        out_specs=pl.BlockSpec((tm, D), lambda i: (i, 0)),
        out_shape=jax.ShapeDtypeStruct((rows, D), F32),
        compiler_params=_cparams("parallel"),
        name="final_rmsnorm",
    )(h, w)


def _segment_position(tile_rows, bs_rows, seq, ctx_len):
    row0 = pl.program_id(0) * tile_rows
    in_x = row0 < bs_rows
    off = jnp.where(in_x, lax.rem(row0, seq), lax.rem(row0 - bs_rows, ctx_len))
    seglen = jnp.where(in_x, seq, ctx_len)
    return in_x, off, seglen


def _qkv_conv_kernel(cur_ref, prev_ref, next_ref, w_ref, *rest, kind, tr, bs_rows, seq, ctx_len):
    if kind == 'k':
        o_ref, ot_ref, ext_ref = rest
    else:
        o_ref, ext_ref = rest
    _, off, seglen = _segment_position(tr, bs_rows, seq, ctx_len)
    is_start = off == 0
    is_end = off + tr == seglen
    pad = SUBLANES
    half = GDN_CONV // 2
    ext_ref[0:pad, :] = jnp.where(is_start, 0.0, prev_ref[...])
    ext_ref[pad:pad + tr, :] = cur_ref[...]
    ext_ref[pad + tr:pad + tr + pad, :] = jnp.where(is_end, 0.0, next_ref[...])
    acc = None
    for j in range(GDN_CONV):
        term = ext_ref[pad - half + j:pad - half + j + tr, :] * w_ref[j:j + 1, :]
        acc = term if acc is None else acc + term
    y = _silu(acc)
    tc = y.shape[1]
    if kind == 'v':
        o_ref[...] = y.astype(o_ref.dtype)
        return
    mult = HEAD_DIM ** -0.5 if kind == 'q' else 1.0
    for hd in range(tc // HEAD_DIM):
        yh = y[:, hd * HEAD_DIM:(hd + 1) * HEAD_DIM]
        inv = lax.rsqrt(jnp.sum(yh * yh, axis=-1, keepdims=True) + EPS)
        yn = yh * inv
        if kind == 'q':
            yn = yn * mult
        o_ref[:, hd * HEAD_DIM:(hd + 1) * HEAD_DIM] = yn.astype(o_ref.dtype)
        if kind == 'k':
            ot_ref[hd * HEAD_DIM:(hd + 1) * HEAD_DIM, :] = yn.T.astype(ot_ref.dtype)


def _qkv_conv(px, conv_w, *, kind, col0, width, tr, tc, bs_rows, seq, ctx_len):
    R = px.shape[0]
    cb0 = col0 // tc
    rb = tr // SUBLANES
    last = R // SUBLANES - 1
    in_specs = [pl.BlockSpec((tr, tc), lambda i, j: (i, cb0 + j)),
                pl.BlockSpec((SUBLANES, tc), lambda i, j: (jnp.maximum(i * rb - 1, 0), cb0 + j)),
                pl.BlockSpec((SUBLANES, tc), lambda i, j: (jnp.minimum((i + 1) * rb, last), cb0 + j)),
                pl.BlockSpec((GDN_CONV, tc), lambda i, j: (0, j))]
    out_specs = [pl.BlockSpec((tr, tc), lambda i, j: (i, j))]
    out_shape = [jax.ShapeDtypeStruct((R, width), BF16)]
    if kind == 'k':
        out_specs.append(pl.BlockSpec((tc, tr), lambda i, j: (j, i)))
        out_shape.append(jax.ShapeDtypeStruct((width, R), BF16))
    res = pl.pallas_call(
        functools.partial(_qkv_conv_kernel, kind=kind, tr=tr, bs_rows=bs_rows, seq=seq, ctx_len=ctx_len),
        grid=(R // tr, width // tc),
        in_specs=in_specs,
        out_specs=out_specs,
        out_shape=out_shape,
        scratch_shapes=[pltpu.VMEM((tr + 2 * SUBLANES, tc), F32)],
        compiler_params=_cparams("parallel", "parallel"),
        name="gdn_conv_" + kind,
    )(px, px, px, conv_w)
    return res


def _gate_kernel(ba_ref, alog_ref, dtb_ref, cols_ref, rows_ref, *, tm, n_pairs):
    lane = lax.broadcasted_iota(jnp.int32, (CHUNK, LANES), 1)
    is_beta = (lane & 1) == 0
    is_bwd = (lane & 4) != 0
    ri = lax.broadcasted_iota(jnp.int32, (CHUNK, CHUNK), 0)
    ci = lax.broadcasted_iota(jnp.int32, (CHUNK, CHUNK), 1)
    tri_lo = jnp.where(ci <= ri, 1.0, 0.0)
    tri_up = jnp.where(ci >= ri, 1.0, 0.0)
    neg_a = -jnp.exp(alog_ref[...])
    for c in range(tm // CHUNK):
        sl = slice(c * CHUNK, (c + 1) * CHUNK)
        x = ba_ref[sl, :]
        beta = jax.nn.sigmoid(x)
        y = x + dtb_ref[...]
        softplus = jnp.maximum(y, 0.0) + jnp.log1p(jnp.exp(-jnp.abs(y)))
        g = jnp.where(is_beta, 0.0, neg_a * softplus)
        gc = jnp.where(is_bwd, _dot_hi(tri_up, g), _dot_hi(tri_lo, g))
        full = jnp.where(is_beta, beta, gc)
        for p in range(n_pairs):
            cols_ref[p, sl, :] = full[:, p * 8:(p + 1) * 8]
        rows_ref[:, sl] = full.T[0:n_pairs * 8, :]


def _gdn_gates(ba, alog_row, dtb_row, *, n_pairs, tm):
    R = ba.shape[0]
    return pl.pallas_call(
        functools.partial(_gate_kernel, tm=tm, n_pairs=n_pairs),
        grid=(R // tm,),
        in_specs=[pl.BlockSpec((tm, LANES), lambda i: (i, 0)),
                  pl.BlockSpec((1, LANES), lambda i: (0, 0)),
                  pl.BlockSpec((1, LANES), lambda i: (0, 0))],
        out_specs=[pl.BlockSpec((n_pairs, tm, 8), lambda i: (0, i, 0)),
                   pl.BlockSpec((n_pairs * 8, tm), lambda i: (0, i))],
        out_shape=[jax.ShapeDtypeStruct((n_pairs, R, 8), F32),
                   jax.ShapeDtypeStruct((n_pairs * 8, R), F32)],
        compiler_params=_cparams("parallel"),
        name="gdn_gates",
    )(ba, alog_row, dtb_row)


def _gdn_kernel(qx_ref, qc_ref, kx_ref, kc_ref, ktx_ref, ktc_ref, vx_ref, vc_ref,
                colx_ref, colc_ref, rowx_ref, rowc_ref, z_ref, nw_ref, o_ref,
                t_ref, a_ref, of_ref, ob_ref, s_ref, *, seq, ctx_len, rb):
    C = CHUNK
    HD = HEAD_DIM
    n_xc = seq // C
    n_cc = ctx_len // C
    r = pl.program_id(2)

    ri = lax.broadcasted_iota(jnp.int32, (C, C), 0)
    ci = lax.broadcasted_iota(jnp.int32, (C, C), 1)
    eye = (ri == ci).astype(F32)

    def lane_idx(d, s, kind):
        return d * 4 + s * 2 + kind

    block_diff = ri ^ ci

    def unit_triangular_inverses(nmats):
        base = int(math.log2(INV_BLOCK))
        n0s = [jnp.where((block_diff >> base) == 0, n, 0.0) for n in nmats]
        invs = [eye + n0 for n0 in n0s]
        pws = n0s
        for _ in range(base - 1):
            pwbs = [pw.astype(BF16) for pw in pws]
            pws = [_dot(b, b) for b in pwbs]
            invs = [inv + _dot(inv.astype(BF16), pw.astype(BF16)) for inv, pw in zip(invs, pws)]
        for level in range(base, int(math.log2(C))):
            offs = [jnp.where((block_diff >> level) == 1, n, 0.0).astype(BF16) for n in nmats]
            invbs = [inv.astype(BF16) for inv in invs]
            halves = [_dot(ib, off).astype(BF16) for ib, off in zip(invbs, offs)]
            invs = [inv + _dot(h, ib) for inv, h, ib in zip(invs, halves, invbs)]
        return invs

    def prepare(q_ref, k_ref, kt_ref, col_ref, row_ref, row0, slot):
        rows = pl.ds(row0, C)
        qk = jnp.concatenate([q_ref[rows, :], k_ref[rows, :]], axis=0)
        gram = _dot(qk, kt_ref[:, rows])
        qkt = gram[0:C, :]
        kkt = gram[C:2 * C, :]
        cols = col_ref[rows, :]
        nmats = []
        for d in range(2):
            incl = (ci <= ri) if d == 0 else (ci >= ri)
            strict = (ci < ri) if d == 0 else (ci > ri)
            for s in range(2):
                kb, kg = lane_idx(d, s, 0), lane_idx(d, s, 1)
                beta_col = cols[:, kb:kb + 1]
                gc_col = cols[:, kg:kg + 1]
                gc_row = row_ref[kg:kg + 1, rows]
                decay = jnp.where(incl, jnp.exp(jnp.where(incl, gc_col - gc_row, 0.0)), 0.0)
                nmats.append(jnp.where(strict, -(kkt * beta_col) * decay, 0.0))
                a_ref[slot, d * 2 + s] = (qkt * decay).astype(BF16)
        for idx, inv in enumerate(unit_triangular_inverses(nmats)):
            t_ref[slot, idx] = inv.astype(BF16)

    def chain(d, q_ref, k_ref, kt_ref, v_ref, col_ref, row0, slot, orow0):
        rows = pl.ds(row0, C)
        o_acc = of_ref if d == 0 else ob_ref
        state = s_ref[d]
        qk = jnp.concatenate([q_ref[rows, :], k_ref[rows, :]], axis=0)
        proj = _dot(qk, state.astype(BF16))
        cols = col_ref[rows, :]
        evs, gls = [], []
        for s in range(2):
            kb, kg = lane_idx(d, s, 0), lane_idx(d, s, 1)
            beta_col = cols[:, kb:kb + 1]
            gc_col = cols[:, kg:kg + 1]
            g_last = gc_col[C - 1:C, :] if d == 0 else gc_col[0:1, :]
            e_col = jnp.exp(gc_col)
            hs = slice(s * HD, (s + 1) * HD)
            q_s = proj[0:C, hs]
            k_s = proj[C:2 * C, hs]
            v = v_ref[rows, hs].astype(F32)
            rhs = beta_col * (v - e_col * k_s)
            v_new = _dot(t_ref[slot, d * 2 + s], rhs.astype(BF16))
            out = e_col * q_s + _dot(a_ref[slot, d * 2 + s], v_new.astype(BF16))
            o_acc[pl.ds(orow0, C), hs] = out
            evs.append(jnp.exp(g_last - gc_col) * v_new)
            gls.append(jnp.broadcast_to(jnp.exp(g_last), (HD, HD)))
        ev = jnp.concatenate(evs, axis=1).astype(BF16)
        s_ref[d] = state * jnp.concatenate(gls, axis=1) + _dot(kt_ref[:, rows], ev)

    @pl.when(r == 0)
    def _():
        s_ref[...] = jnp.zeros_like(s_ref)
        for n in range(n_cc):
            prepare(qc_ref, kc_ref, ktc_ref, colc_ref, rowc_ref, n * C, n)

        def prep_body(n, carry):
            prepare(qx_ref, kx_ref, ktx_ref, colx_ref, rowx_ref, pl.multiple_of(n * C, C), n_cc + n)
            return carry
        lax.fori_loop(0, n_xc, prep_body, 0)

        for n in range(n_cc):
            m = n_cc - 1 - n
            chain(0, qc_ref, kc_ref, ktc_ref, vc_ref, colc_ref, n * C, n, seq + n * C)
            chain(1, qc_ref, kc_ref, ktc_ref, vc_ref, colc_ref, m * C, m, seq + m * C)

        def chain_body(n, carry):
            m = n_xc - 1 - n
            rf = pl.multiple_of(n * C, C)
            rbk = pl.multiple_of(m * C, C)
            chain(0, qx_ref, kx_ref, ktx_ref, vx_ref, colx_ref, rf, n_cc + n, rf)
            chain(1, qx_ref, kx_ref, ktx_ref, vx_ref, colx_ref, rbk, n_cc + m, rbk)
            return carry
        lax.fori_loop(0, n_xc, chain_body, 0)

    orow = pl.multiple_of(r * rb, rb)
    o = of_ref[pl.ds(orow, rb), :] + ob_ref[pl.ds(orow, rb), :]
    z = z_ref[...]
    for s in range(2):
        hs = slice(s * HD, (s + 1) * HD)
        oh = o[:, hs]
        ms = jnp.mean(oh * oh, axis=-1, keepdims=True)
        y = oh * lax.rsqrt(ms + EPS) * nw_ref[...]
        o_ref[:, hs] = (y * _silu(z[:, hs])).astype(o_ref.dtype)


def _gdn(qn, kn, knt, vn, cols, rows, px, z_col0, norm_w, *, n_batch, seq, ctx_len, out_cols):
    R = qn.shape[0]
    n_pairs = qn.shape[1] // HEAD_DIM
    C = CHUNK
    rb = min(256, ctx_len)
    nxb, ncb = seq // rb, ctx_len // rb
    cblk0 = n_batch * (seq // ctx_len)
    zb0 = z_col0 // (2 * HEAD_DIM)
    n_slots = (seq + ctx_len) // C

    def row_block(b, r):
        return jnp.where(r < nxb, b * nxb + r, n_batch * nxb + b * ncb + (r - nxb))

    x_rows = lambda w: pl.BlockSpec((seq, w), lambda b, p, r: (b, p))
    c_rows = lambda w: pl.BlockSpec((ctx_len, w), lambda b, p, r: (cblk0 + b, p))
    in_specs = [
        x_rows(HEAD_DIM), c_rows(HEAD_DIM),
        x_rows(HEAD_DIM), c_rows(HEAD_DIM),
        pl.BlockSpec((HEAD_DIM, seq), lambda b, p, r: (p, b)),
        pl.BlockSpec((HEAD_DIM, ctx_len), lambda b, p, r: (p, cblk0 + b)),
        x_rows(2 * HEAD_DIM), c_rows(2 * HEAD_DIM),
        pl.BlockSpec((None, seq, 8), lambda b, p, r: (p, b, 0)),
        pl.BlockSpec((None, ctx_len, 8), lambda b, p, r: (p, cblk0 + b, 0)),
        pl.BlockSpec((8, seq), lambda b, p, r: (p, b)),
        pl.BlockSpec((8, ctx_len), lambda b, p, r: (p, cblk0 + b)),
        pl.BlockSpec((rb, 2 * HEAD_DIM), lambda b, p, r: (row_block(b, r), zb0 + p)),
        pl.BlockSpec((1, HEAD_DIM), lambda b, p, r: (0, 0)),
    ]
    return pl.pallas_call(
        functools.partial(_gdn_kernel, seq=seq, ctx_len=ctx_len, rb=rb),
        grid=(n_batch, n_pairs, nxb + ncb),
        in_specs=in_specs,
        out_specs=pl.BlockSpec((rb, 2 * HEAD_DIM), lambda b, p, r: (row_block(b, r), p)),
        out_shape=jax.ShapeDtypeStruct((R, out_cols), BF16),
        scratch_shapes=[pltpu.VMEM((n_slots, 4, C, C), BF16),
                        pltpu.VMEM((n_slots, 4, C, C), BF16),
                        pltpu.VMEM((seq + ctx_len, 2 * HEAD_DIM), F32),
                        pltpu.VMEM((seq + ctx_len, 2 * HEAD_DIM), F32),
                        pltpu.VMEM((2, HEAD_DIM, 2 * HEAD_DIM), F32)],
        compiler_params=_cparams("parallel", "parallel", "arbitrary"),
        name="gdn_delta_rule",
    )(qn, qn, kn, kn, knt, knt, vn, vn, cols, cols, rows, rows, px, norm_w)


def _shortconv_kernel(x_ref, b_ref, c_ref, w_ref, ymix_ref, o_ref, ext_ref, *, tr, bs_rows, seq, ctx_len):
    del ymix_ref
    in_x, off, _ = _segment_position(tr, bs_rows, seq, ctx_len)
    period = jnp.where(in_x, GRID_W, ctx_len)
    pos = (off + lax.broadcasted_iota(jnp.int32, (tr, 1), 0)) & (period - 1)
    pad = SUBLANES
    v = c_ref[...] * x_ref[...]
    ext_ref[0:pad, :] = jnp.zeros((pad, v.shape[1]), F32)
    ext_ref[pad:pad + tr, :] = v
    ext_ref[pad + tr:pad + tr + pad, :] = jnp.zeros((pad, v.shape[1]), F32)
    left = jnp.where(pos == 0, 0.0, ext_ref[pad - 1:pad - 1 + tr, :])
    right = jnp.where(pos == period - 1, 0.0, ext_ref[pad + 1:pad + 1 + tr, :])
    y = left * w_ref[0:1, :] + v * w_ref[1:2, :] + right * w_ref[2:3, :]
    o_ref[...] = (b_ref[...] * y).astype(o_ref.dtype)


def _shortconv(px, conv_w, ymix, *, x_col0, width, out_col0, tr, tc, bs_rows, seq, ctx_len):
    R = px.shape[0]
    nb = width // tc
    xb0 = x_col0 // tc
    ob0 = out_col0 // tc
    return pl.pallas_call(
        functools.partial(_shortconv_kernel, tr=tr, bs_rows=bs_rows, seq=seq, ctx_len=ctx_len),
        grid=(R // tr, nb),
        in_specs=[pl.BlockSpec((tr, tc), lambda i, j: (i, xb0 + j)),
                  pl.BlockSpec((tr, tc), lambda i, j: (i, xb0 + nb + j)),
                  pl.BlockSpec((tr, tc), lambda i, j: (i, xb0 + 2 * nb + j)),
                  pl.BlockSpec((SC_CONV, tc), lambda i, j: (0, j)),
                  pl.BlockSpec(memory_space=pl.ANY)],
        out_specs=pl.BlockSpec((tr, tc), lambda i, j: (i, ob0 + j)),
        out_shape=jax.ShapeDtypeStruct(ymix.shape, ymix.dtype),
        input_output_aliases={4: 0},
        scratch_shapes=[pltpu.VMEM((tr + 2 * SUBLANES, tc), F32)],
        compiler_params=_cparams("parallel", "parallel"),
        name="short_conv",
    )(px, px, px, conv_w, ymix)


def _pool_kernel(u_ref, w_ref, sc_ref, ymix_ref, o_ref, pa_ref, pb_ref, *, n_tok, unit, pg):
    del ymix_ref
    n_pos = n_tok // unit
    reach = 8 * unit
    pad = -(-reach // SUBLANES) * SUBLANES
    mg = -(-4 * unit // SUBLANES) * SUBLANES
    tp = n_tok + 2 * pad
    zeros_mg = jnp.zeros((mg, pg), F32)
    zeros_pad = jnp.zeros((pad, pg), F32)
    for buf in (pa_ref, pb_ref):
        buf[0:mg, :] = zeros_mg
        buf[mg + tp:mg + tp + mg, :] = zeros_mg
    row = lax.broadcasted_iota(jnp.int32, (n_tok, 1), 0) // unit
    for g, win in enumerate(POOL_WINDOWS):
        cs = slice(g * pg, (g + 1) * pg)
        u = u_ref[:, cs]
        pa_ref[mg:mg + pad, :] = zeros_pad
        pa_ref[mg + pad:mg + pad + n_tok, :] = u
        pa_ref[mg + pad + n_tok:mg + tp, :] = zeros_pad
        src, dst = pa_ref, pb_ref
        dst[mg:mg + tp, :] = src[mg - unit:mg - unit + tp, :] + src[mg:mg + tp, :]
        src, dst = dst, src
        step, w = 1, 2
        while w < win:
            dst[mg:mg + tp, :] = (src[mg - step * unit:mg - step * unit + tp, :]
                                  + src[mg + step * unit:mg + step * unit + tp, :])
            src, dst = dst, src
            step, w = step * 2, w * 2
        total = src[mg + pad:mg + pad + n_tok, :]
        lo = jnp.maximum(row - win // 2, 0)
        hi = jnp.minimum(row - win // 2 + win, n_pos)
        count = (hi - lo).astype(F32)
        dlt = total / count - u
        y = _dot(dlt.astype(BF16), w_ref[g]) * sc_ref[:, cs]
        o_ref[:, cs] = y.astype(o_ref.dtype)


def _pool(px, pool_w, pool_scale, ymix, *, u_col0, out_col0, n_tok, unit, blk0, n_batch, name):
    width = pool_scale.shape[1]
    pg = width // len(POOL_WINDOWS)
    reach = 8 * unit
    pad = -(-reach // SUBLANES) * SUBLANES
    mg = -(-4 * unit // SUBLANES) * SUBLANES
    buf_rows = n_tok + 2 * pad + 2 * mg
    ub = u_col0 // width
    ob = out_col0 // width
    return pl.pallas_call(
        functools.partial(_pool_kernel, n_tok=n_tok, unit=unit, pg=pg),
        grid=(n_batch,),
        in_specs=[pl.BlockSpec((n_tok, width), lambda b: (blk0 + b, ub)),
                  pl.BlockSpec(pool_w.shape, lambda b: (0, 0, 0)),
                  pl.BlockSpec((1, width), lambda b: (0, 0)),
                  pl.BlockSpec(memory_space=pl.ANY)],
        out_specs=pl.BlockSpec((n_tok, width), lambda b: (blk0 + b, ob)),
        out_shape=jax.ShapeDtypeStruct(ymix.shape, ymix.dtype),
        input_output_aliases={3: 0},
        scratch_shapes=[pltpu.VMEM((buf_rows, pg), F32), pltpu.VMEM((buf_rows, pg), F32)],
        compiler_params=_cparams("parallel"),
        name=name,
    )(px, pool_w, pool_scale, ymix)


def _ba_permutation(n_heads):
    perm = np.zeros(4 * n_heads, np.int32)
    for p in range(n_heads // 2):
        for d in range(2):
            for s in range(2):
                for kind in range(2):
                    perm[p * 8 + d * 4 + s * 2 + kind] = d * 2 * n_heads + kind * n_heads + 2 * p + s
    return perm


def _gate_param_row(param, n_heads):
    row = jnp.zeros((LANES,), F32)
    idx, src_d, src_h = [], [], []
    for p in range(n_heads // 2):
        for d in range(2):
            for s in range(2):
                idx.append(p * 8 + d * 4 + s * 2 + 1)
                src_d.append(d)
                src_h.append(2 * p + s)
    row = row.at[np.array(idx)].set(param[np.array(src_d), np.array(src_h)])
    return row.reshape(1, LANES)


def _pick_tile(n, prefs):
    for t in prefs:
        if n % t == 0:
            return t
    raise ValueError(f"no tile for {n}")


def kernel(x, c, ctx, c_ctx, w_mod, b_mod, norm_ffn1, w_ffn1_gu, w_ffn1_down, norm_mix, w_in, conv_qkv, a_log,
           dt_bias, gdn_norm, conv_short, pool_w, pool_scale, w_out, norm_ffn2, w_ffn2_gu, w_ffn2_down, norm_final):
    B, S, D = x.shape
    CL = ctx.shape[1]
    L = w_mod.shape[0]
    n_vheads = a_log.shape[2]
    qk_dim = (n_vheads // 2) * HEAD_DIM
    v_dim = n_vheads * HEAD_DIM
    sc_dim = conv_short.shape[2]
    pool_dim = pool_scale.shape[1]
    n_ba = 4 * n_vheads
    n_pairs = n_vheads // 2
    bs_rows = B * S
    R = bs_rows + B * CL
    assert S % CHUNK == 0 and CL % CHUNK == 0 and S % CL == 0 and n_ba <= LANES

    tm = _pick_tile(math.gcd(S, R), (1024, 512, 256, 128))
    tg = _pick_tile(math.gcd(S, R), (512, 256, 128))
    tr = min(256, CL)
    tn = 512 if D % 512 == 0 else 256
    tn_ff = 256

    z_col0 = 2 * qk_dim + v_dim
    ba_col0 = z_col0 + v_dim
    sc_col0_src = ba_col0 + n_ba
    sc_col0 = ba_col0
    pool_col0 = sc_col0 + 3 * sc_dim
    ba_perm = _ba_permutation(n_vheads)

    h = jnp.concatenate([x.reshape(bs_rows, D), ctx.reshape(B * CL, D)], axis=0)
    c_all = jnp.zeros((SUBLANES, D), F32).at[0:B].set(c).at[B].set(c_ctx)
    mods_all = _modulation(c_all, w_mod, b_mod.reshape(L, 1, N_MOD * D)).reshape(L, SUBLANES, N_MOD, D)

    common = dict(seq_rows=S, n_batch=B)
    seg = dict(bs_rows=bs_rows, seq=S, ctx_len=CL)
    for l in range(L):
        n_live = bs_rows if l == L - 1 else R
        mods = mods_all[l]
        w1gu = w_ffn1_gu[l].astype(BF16)
        w1d = w_ffn1_down[l].astype(BF16)
        w2gu = w_ffn2_gu[l].astype(BF16)
        w2d = w_ffn2_down[l].astype(BF16)
        w_main = jnp.concatenate([w_in[l][:, 0:ba_col0], w_in[l][:, sc_col0_src:]], axis=1).astype(BF16)
        w_ba = jnp.zeros((D, LANES), BF16).at[:, 0:n_ba].set(w_in[l][:, ba_col0 + ba_perm].astype(BF16))
        wo = w_out[l].astype(BF16)

        hn = _adaln(h, norm_ffn1[l].reshape(1, D), mods, n_rows=R, shift_idx=0, scale_idx=1, tr=tr,
                    name="ffn1_adaln", **common)
        act = _matmul(hn, w1gu, swiglu=True, out_dtype=BF16, tm=tm, tn=tn, name="ffn1_up")
        h = _matmul_residual(act, w1d, h, mods, n_rows=R, gate_idx=2, scale=0.5, tm=tm, tn=tn_ff,
                             name="ffn1_down", **common)

        hn = _adaln(h, norm_mix[l].reshape(1, D), mods, n_rows=R, shift_idx=3, scale_idx=4, tr=tr,
                    name="mix_adaln", **common)
        px = _matmul(hn, w_main, swiglu=False, out_dtype=F32, tm=tm, tn=tn, name="in_proj")
        ba = _matmul(hn, w_ba, swiglu=False, out_dtype=F32, tm=tm, tn=LANES, name="in_proj_ba")

        tc = _pick_tile(qk_dim, (512, 256, 128))
        conv = dict(tr=tr, tc=tc, **seg)
        (qn,) = _qkv_conv(px, conv_qkv[l][:, 0:qk_dim], kind='q', col0=0, width=qk_dim, **conv)
        kn, knt = _qkv_conv(px, conv_qkv[l][:, qk_dim:2 * qk_dim], kind='k', col0=qk_dim, width=qk_dim, **conv)
        (vn,) = _qkv_conv(px, conv_qkv[l][:, 2 * qk_dim:], kind='v', col0=2 * qk_dim, width=v_dim, **conv)
        cols, rows = _gdn_gates(ba, _gate_param_row(a_log[l], n_vheads), _gate_param_row(dt_bias[l], n_vheads),
                                n_pairs=n_pairs, tm=tg)
        ymix = _gdn(qn, kn, knt, vn, cols, rows, px, z_col0, gdn_norm[l].reshape(1, HEAD_DIM),
                    n_batch=B, seq=S, ctx_len=CL, out_cols=v_dim + sc_dim + pool_dim)

        ymix = _shortconv(px, conv_short[l], ymix, x_col0=sc_col0, width=sc_dim, out_col0=v_dim,
                          tr=tr, tc=_pick_tile(sc_dim, (512, 256, 128)), **seg)
        pw = pool_w[l].astype(BF16)
        psc = pool_scale[l].reshape(1, pool_dim)
        ymix = _pool(px, pw, psc, ymix, u_col0=pool_col0, out_col0=v_dim + sc_dim, n_tok=S, unit=GRID_W,
                     blk0=0, n_batch=B, name="pool_latent")
        ymix = _pool(px, pw, psc, ymix, u_col0=pool_col0, out_col0=v_dim + sc_dim, n_tok=CL, unit=1,
                     blk0=bs_rows // CL, n_batch=B, name="pool_context")
        h = _matmul_residual(ymix, wo, h, mods, n_rows=n_live, gate_idx=5, scale=1.0, tm=tm, tn=tn,
                             name="mix_out", **common)

        hn = _adaln(h, norm_ffn2[l].reshape(1, D), mods, n_rows=n_live, shift_idx=6, scale_idx=7, tr=tr,
                    name="ffn2_adaln", **common)
        act = _matmul(hn, w2gu, swiglu=True, out_dtype=BF16, tm=tm, tn=tn, name="ffn2_up")
        h = _matmul_residual(act, w2d, h, mods, n_rows=n_live, gate_idx=8, scale=0.5, tm=tm, tn=tn_ff,
                             name="ffn2_down", **common)

    out = _final_rmsnorm(h, norm_final.reshape(1, D), bs_rows, tr)
    return out.reshape(B, S, D)
```

```python
import functools
import math

import numpy as np
import jax
import jax.numpy as jnp
from jax import lax
from jax.experimental import pallas as pl
from jax.experimental.pallas import tpu as pltpu

EPS = 1e-6
N_MOD = 9
GRID_W = 64
HEAD_DIM = 128
GDN_CONV = 5
SC_CONV = 3
POOL_WINDOWS = (2, 4, 8, 16)
CHUNK = 128
INV_BLOCK = 16
LANES = 128
SUBLANES = 8
BF16_SUBLANES = 16
VMEM_LIMIT = 56 * 1024 * 1024

BF16 = jnp.bfloat16
F32 = jnp.float32


def _cparams(*sem):
    return pltpu.CompilerParams(dimension_semantics=sem, vmem_limit_bytes=VMEM_LIMIT)


def _dot(a, b):
    return jnp.dot(a, b, preferred_element_type=F32)


def _dot_hi(a, b):
    return jnp.dot(a, b, preferred_element_type=F32, precision=lax.Precision.HIGHEST)


def _silu(v):
    return v * jax.nn.sigmoid(v)


def _mod_kernel(c_ref, w_ref, b_ref, o_ref):
    a = _silu(c_ref[...]).astype(BF16)
    o_ref[...] = _dot(a, w_ref[...].astype(BF16)) + b_ref[...]


def _modulation(c_all, w_mod, b_mod, tn=512):
    L, D, N = w_mod.shape
    return pl.pallas_call(
        _mod_kernel,
        grid=(L, N // tn),
        in_specs=[pl.BlockSpec((SUBLANES, D), lambda l, j: (0, 0)),
                  pl.BlockSpec((None, D, tn), lambda l, j: (l, 0, j)),
                  pl.BlockSpec((None, 1, tn), lambda l, j: (l, 0, j))],
        out_specs=pl.BlockSpec((None, SUBLANES, tn), lambda l, j: (l, 0, j)),
        out_shape=jax.ShapeDtypeStruct((L, SUBLANES, N), F32),
        compiler_params=_cparams("parallel", "parallel"),
        name="modulation",
    )(c_all, w_mod, b_mod)


def _adaln_kernel(h_ref, nw_ref, mod_ref, o_ref, *, shift_idx, scale_idx):
    h = h_ref[...]
    ms = jnp.mean(h * h, axis=-1, keepdims=True)
    y = h * lax.rsqrt(ms + EPS) * nw_ref[...]
    shift = mod_ref[0, shift_idx:shift_idx + 1, :]
    scale = mod_ref[0, scale_idx:scale_idx + 1, :]
    o_ref[...] = (y * (1.0 + scale) + shift).astype(o_ref.dtype)


def _adaln(h, norm_w, mods, *, n_rows, seq_rows, n_batch, shift_idx, scale_idx, tr, name):
    D = h.shape[1]
    mod_map = lambda i: (jnp.minimum((i * tr) // seq_rows, n_batch), 0, 0)
    return pl.pallas_call(
        functools.partial(_adaln_kernel, shift_idx=shift_idx, scale_idx=scale_idx),
        grid=(n_rows // tr,),
        in_specs=[pl.BlockSpec((tr, D), lambda i: (i, 0)),
                  pl.BlockSpec((1, D), lambda i: (0, 0)),
                  pl.BlockSpec((1, N_MOD, D), mod_map)],
        out_specs=pl.BlockSpec((tr, D), lambda i: (i, 0)),
        out_shape=jax.ShapeDtypeStruct((n_rows, D), BF16),
        compiler_params=_cparams("parallel"),
        name=name,
    )(h, norm_w, mods)


def _cast_row_block(rows, steps):
    for rps in range(BF16_SUBLANES, rows + 1, BF16_SUBLANES):
        if rows % rps == 0 and rows // rps <= steps:
            return rps
    raise ValueError(f"cannot spread {rows} rows over {steps} steps")


def _cast_plan(casts, n_i, nj):
    in_specs, out_specs, out_shapes, args, range_list = [], [], [], [], []
    for src, layer, ranges in casts:
        _, rows, cols = src.shape
        rps = _cast_row_block(rows, n_i * nj)
        last = rows // rps - 1
        out_cols = sum(b - a for a, b in ranges)
        block_map = lambda i, j, last=last: (jnp.minimum(i * nj + j, last), 0)
        src_map = lambda i, j, last=last, layer=layer: (layer, jnp.minimum(i * nj + j, last), 0)
        in_specs.append(pl.BlockSpec((None, rps, cols), src_map))
        out_specs.append(pl.BlockSpec((rps, out_cols), block_map))
        out_shapes.append(jax.ShapeDtypeStruct((rows, out_cols), BF16))
        args.append(src)
        range_list.append(tuple(ranges))
    return in_specs, out_specs, out_shapes, args, range_list


def _run_casts(src_refs, dst_refs, range_list):
    for src_ref, dst_ref, ranges in zip(src_refs, dst_refs, range_list):
        off = 0
        for a, b in ranges:
            dst_ref[:, off:off + b - a] = src_ref[:, a:b].astype(dst_ref.dtype)
            off += b - a


def _mm_kernel(a_ref, *rest, swiglu, cast_ranges):
    nc = len(cast_ranges)
    n_w = 2 if swiglu else 1
    w_refs, src_refs = rest[:n_w], rest[n_w:n_w + nc]
    o_ref, dst_refs = rest[n_w + nc], rest[n_w + nc + 1:]
    a = a_ref[...]
    if swiglu:
        g = _dot(a, w_refs[0][...])
        u = _dot(a, w_refs[1][...])
        o_ref[...] = (_silu(g) * u).astype(o_ref.dtype)
    else:
        o_ref[...] = _dot(a, w_refs[0][...]).astype(o_ref.dtype)
    _run_casts(src_refs, dst_refs, cast_ranges)


def _matmul(a, w, *, swiglu, out_dtype, tm, tn, name, casts=()):
    R, K = a.shape
    N = w.shape[1]
    n_out = N // 2 if swiglu else N
    nj = n_out // tn
    in_specs = [pl.BlockSpec((tm, K), lambda i, j: (i, 0)),
                pl.BlockSpec((K, tn), lambda i, j: (0, j))]
    args = [a, w]
    if swiglu:
        in_specs.append(pl.BlockSpec((K, tn), lambda i, j: (0, j + nj)))
        args.append(w)
    c_in, c_out, c_shapes, c_args, c_ranges = _cast_plan(casts, R // tm, nj)
    res = pl.pallas_call(
        functools.partial(_mm_kernel, swiglu=swiglu, cast_ranges=c_ranges),
        grid=(R // tm, nj),
        in_specs=in_specs + c_in,
        out_specs=[pl.BlockSpec((tm, tn), lambda i, j: (i, j))] + c_out,
        out_shape=[jax.ShapeDtypeStruct((R, n_out), out_dtype)] + c_shapes,
        compiler_params=_cparams("arbitrary", "arbitrary"),
        name=name,
    )(*args, *c_args)
    return res[0], res[1:]


def _mm_res_kernel(a_ref, w_ref, res_ref, mod_ref, *rest, gate_idx, scale, cast_ranges):
    nc = len(cast_ranges)
    src_refs, o_ref, dst_refs = rest[:nc], rest[nc], rest[nc + 1:]
    acc = _dot(a_ref[...], w_ref[...])
    gate = mod_ref[0, gate_idx:gate_idx + 1, :]
    o_ref[...] = res_ref[...] + (scale * gate) * acc
    _run_casts(src_refs, dst_refs, cast_ranges)


def _matmul_residual(a, w, res, mods, *, n_rows, seq_rows, n_batch, gate_idx, scale, tm, tn, name, casts=()):
    K = a.shape[1]
    D = w.shape[1]
    mod_map = lambda i, j: (jnp.minimum((i * tm) // seq_rows, n_batch), 0, j)
    c_in, c_out, c_shapes, c_args, c_ranges = _cast_plan(casts, n_rows // tm, D // tn)
    out = pl.pallas_call(
        functools.partial(_mm_res_kernel, gate_idx=gate_idx, scale=scale, cast_ranges=c_ranges),
        grid=(n_rows // tm, D // tn),
        in_specs=[pl.BlockSpec((tm, K), lambda i, j: (i, 0)),
                  pl.BlockSpec((K, tn), lambda i, j: (0, j)),
                  pl.BlockSpec((tm, tn), lambda i, j: (i, j)),
                  pl.BlockSpec((1, N_MOD, tn), mod_map)] + c_in,
        out_specs=[pl.BlockSpec((tm, tn), lambda i, j: (i, j))] + c_out,
        out_shape=[jax.ShapeDtypeStruct((n_rows, D), F32)] + c_shapes,
        compiler_params=_cparams("arbitrary", "arbitrary"),
        name=name,
    )(a, w, res, mods, *c_args)
    return out[0], out[1:]


def _rmsnorm_kernel(h_ref, w_ref, o_ref):
    h = h_ref[...]
    ms = jnp.mean(h * h, axis=-1, keepdims=True)
    o_ref[...] = h * lax.rsqrt(ms + EPS) * w_ref[...]


def _final_rmsnorm(h, w, rows, tm):
    D = h.shape[1]
    return pl.pallas_call(
        _rmsnorm_kernel,
        grid=(rows // tm,),
        in_specs=[pl.BlockSpec((tm, D), lambda i: (i, 0)),
                  pl.BlockSpec((1, D), lambda i: (0, 0))],
        out_specs=pl.BlockSpec((tm, D), lambda i: (i, 0)),
        out_shape=jax.ShapeDtypeStruct((rows, D), F32),
        compiler_params=_cparams("parallel"),
        name="final_rmsnorm",
    )(h, w)


def _segment_position(tile_rows, bs_rows, seq, ctx_len):
    row0 = pl.program_id(0) * tile_rows
    in_x = row0 < bs_rows
    off = jnp.where(in_x, lax.rem(row0, seq), lax.rem(row0 - bs_rows, ctx_len))
    seglen = jnp.where(in_x, seq, ctx_len)
    return in_x, off, seglen


def _qkv_conv_kernel(cur_ref, prev_ref, next_ref, w_ref, *rest, kind, tr, bs_rows, seq, ctx_len):
    if kind == 'k':
        o_ref, ot_ref, ext_ref = rest
    else:
        o_ref, ext_ref = rest
    _, off, seglen = _segment_position(tr, bs_rows, seq, ctx_len)
    is_start = off == 0
    is_end = off + tr == seglen
    pad = SUBLANES
    half = GDN_CONV // 2
    ext_ref[0:pad, :] = jnp.where(is_start, 0.0, prev_ref[...])
    ext_ref[pad:pad + tr, :] = cur_ref[...]
    ext_ref[pad + tr:pad + tr + pad, :] = jnp.where(is_end, 0.0, next_ref[...])
    acc = None
    for j in range(GDN_CONV):
        term = ext_ref[pad - half + j:pad - half + j + tr, :] * w_ref[j:j + 1, :]
        acc = term if acc is None else acc + term
    y = _silu(acc)
    tc = y.shape[1]
    if kind == 'v':
        o_ref[...] = y.astype(o_ref.dtype)
        return
    mult = HEAD_DIM ** -0.5 if kind == 'q' else 1.0
    for hd in range(tc // HEAD_DIM):
        yh = y[:, hd * HEAD_DIM:(hd + 1) * HEAD_DIM]
        inv = lax.rsqrt(jnp.sum(yh * yh, axis=-1, keepdims=True) + EPS)
        yn = yh * inv
        if kind == 'q':
            yn = yn * mult
        o_ref[:, hd * HEAD_DIM:(hd + 1) * HEAD_DIM] = yn.astype(o_ref.dtype)
        if kind == 'k':
            ot_ref[hd * HEAD_DIM:(hd + 1) * HEAD_DIM, :] = yn.T.astype(ot_ref.dtype)


def _qkv_conv(px, conv_w, *, kind, col0, width, tr, tc, bs_rows, seq, ctx_len):
    R = px.shape[0]
    cb0 = col0 // tc
    rb = tr // SUBLANES
    last = R // SUBLANES - 1
    in_specs = [pl.BlockSpec((tr, tc), lambda i, j: (i, cb0 + j)),
                pl.BlockSpec((SUBLANES, tc), lambda i, j: (jnp.maximum(i * rb - 1, 0), cb0 + j)),
                pl.BlockSpec((SUBLANES, tc), lambda i, j: (jnp.minimum((i + 1) * rb, last), cb0 + j)),
                pl.BlockSpec((GDN_CONV, tc), lambda i, j: (0, j))]
    out_specs = [pl.BlockSpec((tr, tc), lambda i, j: (i, j))]
    out_shape = [jax.ShapeDtypeStruct((R, width), BF16)]
    if kind == 'k':
        out_specs.append(pl.BlockSpec((tc, tr), lambda i, j: (j, i)))
        out_shape.append(jax.ShapeDtypeStruct((width, R), BF16))
    res = pl.pallas_call(
        functools.partial(_qkv_conv_kernel, kind=kind, tr=tr, bs_rows=bs_rows, seq=seq, ctx_len=ctx_len),
        grid=(R // tr, width // tc),
        in_specs=in_specs,
        out_specs=out_specs,
        out_shape=out_shape,
        scratch_shapes=[pltpu.VMEM((tr + 2 * SUBLANES, tc), F32)],
        compiler_params=_cparams("parallel", "parallel"),
        name="gdn_conv_" + kind,
    )(px, px, px, conv_w)
    return res


def _gate_kernel(ba_ref, alog_ref, dtb_ref, cols_ref, rows_ref, *, tm, n_pairs):
    lane = lax.broadcasted_iota(jnp.int32, (CHUNK, LANES), 1)
    is_beta = (lane & 1) == 0
    is_bwd = (lane & 4) != 0
    ri = lax.broadcasted_iota(jnp.int32, (CHUNK, CHUNK), 0)
    ci = lax.broadcasted_iota(jnp.int32, (CHUNK, CHUNK), 1)
    tri_lo = jnp.where(ci <= ri, 1.0, 0.0)
    tri_up = jnp.where(ci >= ri, 1.0, 0.0)
    neg_a = -jnp.exp(alog_ref[...])
    for c in range(tm // CHUNK):
        sl = slice(c * CHUNK, (c + 1) * CHUNK)
        x = ba_ref[sl, :]
        beta = jax.nn.sigmoid(x)
        y = x + dtb_ref[...]
        softplus = jnp.maximum(y, 0.0) + jnp.log1p(jnp.exp(-jnp.abs(y)))
        g = jnp.where(is_beta, 0.0, neg_a * softplus)
        gc = jnp.where(is_bwd, _dot_hi(tri_up, g), _dot_hi(tri_lo, g))
        full = jnp.where(is_beta, beta, gc)
        for p in range(n_pairs):
            cols_ref[p, sl, :] = full[:, p * 8:(p + 1) * 8]
        rows_ref[:, sl] = full.T[0:n_pairs * 8, :]


def _gdn_gates(ba, alog_row, dtb_row, *, n_pairs, tm):
    R = ba.shape[0]
    return pl.pallas_call(
        functools.partial(_gate_kernel, tm=tm, n_pairs=n_pairs),
        grid=(R // tm,),
        in_specs=[pl.BlockSpec((tm, LANES), lambda i: (i, 0)),
                  pl.BlockSpec((1, LANES), lambda i: (0, 0)),
                  pl.BlockSpec((1, LANES), lambda i: (0, 0))],
        out_specs=[pl.BlockSpec((n_pairs, tm, 8), lambda i: (0, i, 0)),
                   pl.BlockSpec((n_pairs * 8, tm), lambda i: (0, i))],
        out_shape=[jax.ShapeDtypeStruct((n_pairs, R, 8), F32),
                   jax.ShapeDtypeStruct((n_pairs * 8, R), F32)],
        compiler_params=_cparams("parallel"),
        name="gdn_gates",
    )(ba, alog_row, dtb_row)


def _gdn_kernel(qx_ref, qc_ref, kx_ref, kc_ref, ktx_ref, ktc_ref, vx_ref, vc_ref,
                colx_ref, colc_ref, rowx_ref, rowc_ref, z_ref, nw_ref, o_ref,
                t_ref, a_ref, of_ref, ob_ref, sf_ref, sb_ref, *, seq, ctx_len, rb, n_pair, prep_chunks):
    C = CHUNK
    HD = HEAD_DIM
    n_xc = seq // C
    n_cc = ctx_len // C
    r = pl.program_id(2)
    x_refs = (qx_ref, kx_ref, ktx_ref, vx_ref, colx_ref, rowx_ref)
    c_refs = (qc_ref, kc_ref, ktc_ref, vc_ref, colc_ref, rowc_ref)

    ri = lax.broadcasted_iota(jnp.int32, (C, C), 0)
    ci = lax.broadcasted_iota(jnp.int32, (C, C), 1)
    eye = (ri == ci).astype(F32)

    def lane_idx(d, s, kind):
        return d * 4 + s * 2 + kind

    block_diff = ri ^ ci

    def unit_triangular_inverses(nmats):
        base = int(math.log2(INV_BLOCK))
        n0s = [jnp.where((block_diff >> base) == 0, n, 0.0) for n in nmats]
        invs = [eye + n0 for n0 in n0s]
        pws = n0s
        for _ in range(base - 1):
            pwbs = [pw.astype(BF16) for pw in pws]
            pws = [_dot(b, b) for b in pwbs]
            invs = [inv + _dot(inv.astype(BF16), pw.astype(BF16)) for inv, pw in zip(invs, pws)]
        for level in range(base, int(math.log2(C))):
            offs = [jnp.where((block_diff >> level) == 1, n, 0.0).astype(BF16) for n in nmats]
            invbs = [inv.astype(BF16) for inv in invs]
            halves = [_dot(ib, off).astype(BF16) for ib, off in zip(invbs, offs)]
            invs = [inv + _dot(h, ib) for inv, h, ib in zip(invs, halves, invbs)]
        return invs

    def prepare(chunks):
        nmats, dests = [], []
        for refs, row0, slot in chunks:
            q_ref, k_ref, kt_ref, _, col_ref, row_ref = refs
            rows = pl.ds(row0, C)
            for u in range(n_pair):
                hs = slice(u * HD, (u + 1) * HD)
                qk = jnp.concatenate([q_ref[rows, hs], k_ref[rows, hs]], axis=0)
                gram = _dot(qk, kt_ref[hs, rows])
                qkt = gram[0:C, :]
                kkt = gram[C:2 * C, :]
                cols = col_ref[u, rows, :]
                for d in range(2):
                    incl = (ci <= ri) if d == 0 else (ci >= ri)
                    strict = (ci < ri) if d == 0 else (ci > ri)
                    for s in range(2):
                        kb, kg = lane_idx(d, s, 0), lane_idx(d, s, 1)
                        beta_col = cols[:, kb:kb + 1]
                        gc_col = cols[:, kg:kg + 1]
                        gc_row = row_ref[u * 8 + kg:u * 8 + kg + 1, rows]
                        decay = jnp.where(incl, jnp.exp(jnp.where(incl, gc_col - gc_row, 0.0)), 0.0)
                        nmats.append(jnp.where(strict, -(kkt * beta_col) * decay, 0.0))
                        a_ref[slot, u * 4 + d * 2 + s] = (qkt * decay).astype(BF16)
                        dests.append((slot, u * 4 + d * 2 + s))
        for (slot, idx), inv in zip(dests, unit_triangular_inverses(nmats)):
            t_ref[slot, idx] = inv.astype(BF16)

    def chain(tasks):
        units = [(d, u, refs, pl.ds(row0, C), slot, orow0)
                 for d, refs, row0, slot, orow0 in tasks for u in range(n_pair)]
        states, projs = [], []
        for d, u, refs, rows, _, _ in units:
            hs = slice(u * HD, (u + 1) * HD)
            state = (sf_ref if d == 0 else sb_ref)[u]
            qk = jnp.concatenate([refs[0][rows, hs], refs[1][rows, hs]], axis=0)
            states.append(state)
            projs.append(_dot(qk, state.astype(BF16)))
        scalars, rhss = [], []
        for (d, u, refs, rows, _, _), proj in zip(units, projs):
            cols = refs[4][u, rows, :]
            for s in range(2):
                kb, kg = lane_idx(d, s, 0), lane_idx(d, s, 1)
                beta_col = cols[:, kb:kb + 1]
                gc_col = cols[:, kg:kg + 1]
                g_last = gc_col[C - 1:C, :] if d == 0 else gc_col[0:1, :]
                e_col = jnp.exp(gc_col)
                vs = slice((2 * u + s) * HD, (2 * u + s + 1) * HD)
                v = refs[3][rows, vs].astype(F32)
                k_s = proj[C:2 * C, s * HD:(s + 1) * HD]
                rhss.append((beta_col * (v - e_col * k_s)).astype(BF16))
                scalars.append((gc_col, e_col, g_last))
        v_news = []
        for i, (d, u, _, _, slot, _) in enumerate(units):
            for s in range(2):
                v_news.append(_dot(t_ref[slot, u * 4 + d * 2 + s], rhss[2 * i + s]))
        for i, (d, u, refs, rows, slot, orow0) in enumerate(units):
            o_acc = of_ref if d == 0 else ob_ref
            evs, gls = [], []
            for s in range(2):
                gc_col, e_col, g_last = scalars[2 * i + s]
                v_new = v_news[2 * i + s]
                q_s = projs[i][0:C, s * HD:(s + 1) * HD]
                out = e_col * q_s + _dot(a_ref[slot, u * 4 + d * 2 + s], v_new.astype(BF16))
                o_acc[pl.ds(orow0, C), (2 * u + s) * HD:(2 * u + s + 1) * HD] = out
                evs.append(jnp.exp(g_last - gc_col) * v_new)
                gls.append(jnp.broadcast_to(jnp.exp(g_last), (HD, HD)))
            ev = jnp.concatenate(evs, axis=1).astype(BF16)
            new_state = states[i] * jnp.concatenate(gls, axis=1) + _dot(refs[2][u * HD:(u + 1) * HD, rows], ev)
            (sf_ref if d == 0 else sb_ref)[u] = new_state

    @pl.when(r == 0)
    def _():
        sf_ref[...] = jnp.zeros_like(sf_ref)
        sb_ref[...] = jnp.zeros_like(sb_ref)
        prepare([(c_refs, n * C, n) for n in range(n_cc)])

        def prep_body(it, carry):
            n0 = it * prep_chunks
            prepare([(x_refs, pl.multiple_of((n0 + j) * C, C), n_cc + n0 + j) for j in range(prep_chunks)])
            return carry
        lax.fori_loop(0, n_xc // prep_chunks, prep_body, 0)

        for n in range(n_cc):
            m = n_cc - 1 - n
            chain([(0, c_refs, n * C, n, seq + n * C), (1, c_refs, m * C, m, seq + m * C)])

        def chain_body(n, carry):
            m = n_xc - 1 - n
            rf = pl.multiple_of(n * C, C)
            rbk = pl.multiple_of(m * C, C)
            chain([(0, x_refs, rf, n_cc + n, rf), (1, x_refs, rbk, n_cc + m, rbk)])
            return carry
        lax.fori_loop(0, n_xc, chain_body, 0)

    orow = pl.multiple_of(r * rb, rb)
    o = of_ref[pl.ds(orow, rb), :] + ob_ref[pl.ds(orow, rb), :]
    z = z_ref[...]
    for hd in range(2 * n_pair):
        hs = slice(hd * HD, (hd + 1) * HD)
        oh = o[:, hs]
        ms = jnp.mean(oh * oh, axis=-1, keepdims=True)
        y = oh * lax.rsqrt(ms + EPS) * nw_ref[...]
        o_ref[:, hs] = (y * _silu(z[:, hs])).astype(o_ref.dtype)


def _gdn(qn, kn, knt, vn, cols, rows, px, z_col0, norm_w, *, n_batch, seq, ctx_len, out_cols):
    R = qn.shape[0]
    n_pairs = qn.shape[1] // HEAD_DIM
    C = CHUNK
    rb = min(256, ctx_len)
    nxb, ncb = seq // rb, ctx_len // rb
    cblk0 = n_batch * (seq // ctx_len)
    zb0 = z_col0 // (2 * HEAD_DIM)
    n_slots = (seq + ctx_len) // C
    n_pair = 2 if n_pairs % 2 == 0 else 1
    prep_chunks = 2 if (seq // C) % 2 == 0 else 1
    qw = n_pair * HEAD_DIM
    vw = 2 * qw
    assert z_col0 % vw == 0

    def row_block(b, r):
        return jnp.where(r < nxb, b * nxb + r, n_batch * nxb + b * ncb + (r - nxb))

    x_rows = lambda w: pl.BlockSpec((seq, w), lambda b, p, r: (b, p))
    c_rows = lambda w: pl.BlockSpec((ctx_len, w), lambda b, p, r: (cblk0 + b, p))
    in_specs = [
        x_rows(qw), c_rows(qw),
        x_rows(qw), c_rows(qw),
        pl.BlockSpec((qw, seq), lambda b, p, r: (p, b)),
        pl.BlockSpec((qw, ctx_len), lambda b, p, r: (p, cblk0 + b)),
        x_rows(vw), c_rows(vw),
        pl.BlockSpec((n_pair, seq, 8), lambda b, p, r: (p, b, 0)),
        pl.BlockSpec((n_pair, ctx_len, 8), lambda b, p, r: (p, cblk0 + b, 0)),
        pl.BlockSpec((n_pair * 8, seq), lambda b, p, r: (p, b)),
        pl.BlockSpec((n_pair * 8, ctx_len), lambda b, p, r: (p, cblk0 + b)),
        pl.BlockSpec((rb, vw), lambda b, p, r: (row_block(b, r), z_col0 // vw + p)),
        pl.BlockSpec((1, HEAD_DIM), lambda b, p, r: (0, 0)),
    ]
    return pl.pallas_call(
        functools.partial(_gdn_kernel, seq=seq, ctx_len=ctx_len, rb=rb, n_pair=n_pair, prep_chunks=prep_chunks),
        grid=(n_batch, n_pairs // n_pair, nxb + ncb),
        in_specs=in_specs,
        out_specs=pl.BlockSpec((rb, vw), lambda b, p, r: (row_block(b, r), p)),
        out_shape=jax.ShapeDtypeStruct((R, out_cols), BF16),
        scratch_shapes=[pltpu.VMEM((n_slots, n_pair * 4, C, C), BF16),
                        pltpu.VMEM((n_slots, n_pair * 4, C, C), BF16),
                        pltpu.VMEM((seq + ctx_len, vw), F32),
                        pltpu.VMEM((seq + ctx_len, vw), F32),
                        pltpu.VMEM((n_pair, HEAD_DIM, 2 * HEAD_DIM), F32),
                        pltpu.VMEM((n_pair, HEAD_DIM, 2 * HEAD_DIM), F32)],
        compiler_params=_cparams("parallel", "parallel", "arbitrary"),
        name="gdn_delta_rule",
    )(qn, qn, kn, kn, knt, knt, vn, vn, cols, cols, rows, rows, px, norm_w)


def _shortconv_kernel(x_ref, b_ref, c_ref, w_ref, ymix_ref, o_ref, ext_ref, *, tr, bs_rows, seq, ctx_len):
    del ymix_ref
    in_x, off, _ = _segment_position(tr, bs_rows, seq, ctx_len)
    period = jnp.where(in_x, GRID_W, ctx_len)
    pos = (off + lax.broadcasted_iota(jnp.int32, (tr, 1), 0)) & (period - 1)
    pad = SUBLANES
    v = c_ref[...] * x_ref[...]
    ext_ref[0:pad, :] = jnp.zeros((pad, v.shape[1]), F32)
    ext_ref[pad:pad + tr, :] = v
    ext_ref[pad + tr:pad + tr + pad, :] = jnp.zeros((pad, v.shape[1]), F32)
    left = jnp.where(pos == 0, 0.0, ext_ref[pad - 1:pad - 1 + tr, :])
    right = jnp.where(pos == period - 1, 0.0, ext_ref[pad + 1:pad + 1 + tr, :])
    y = left * w_ref[0:1, :] + v * w_ref[1:2, :] + right * w_ref[2:3, :]
    o_ref[...] = (b_ref[...] * y).astype(o_ref.dtype)


def _shortconv(px, conv_w, ymix, *, x_col0, width, out_col0, tr, tc, bs_rows, seq, ctx_len):
    R = px.shape[0]
    nb = width // tc
    xb0 = x_col0 // tc
    ob0 = out_col0 // tc
    return pl.pallas_call(
        functools.partial(_shortconv_kernel, tr=tr, bs_rows=bs_rows, seq=seq, ctx_len=ctx_len),
        grid=(R // tr, nb),
        in_specs=[pl.BlockSpec((tr, tc), lambda i, j: (i, xb0 + j)),
                  pl.BlockSpec((tr, tc), lambda i, j: (i, xb0 + nb + j)),
                  pl.BlockSpec((tr, tc), lambda i, j: (i, xb0 + 2 * nb + j)),
                  pl.BlockSpec((SC_CONV, tc), lambda i, j: (0, j)),
                  pl.BlockSpec(memory_space=pl.ANY)],
        out_specs=pl.BlockSpec((tr, tc), lambda i, j: (i, ob0 + j)),
        out_shape=jax.ShapeDtypeStruct(ymix.shape, ymix.dtype),
        input_output_aliases={4: 0},
        scratch_shapes=[pltpu.VMEM((tr + 2 * SUBLANES, tc), F32)],
        compiler_params=_cparams("parallel", "parallel"),
        name="short_conv",
    )(px, px, px, conv_w, ymix)


def _pool_kernel(u_ref, w_ref, sc_ref, ymix_ref, o_ref, pa_ref, pb_ref, *, n_tok, unit, pg):
    del ymix_ref
    n_pos = n_tok // unit
    reach = 8 * unit
    pad = -(-reach // SUBLANES) * SUBLANES
    mg = -(-4 * unit // SUBLANES) * SUBLANES
    tp = n_tok + 2 * pad
    zeros_mg = jnp.zeros((mg, pg), F32)
    zeros_pad = jnp.zeros((pad, pg), F32)
    for buf in (pa_ref, pb_ref):
        buf[0:mg, :] = zeros_mg
        buf[mg + tp:mg + tp + mg, :] = zeros_mg
    row = lax.broadcasted_iota(jnp.int32, (n_tok, 1), 0) // unit
    for g, win in enumerate(POOL_WINDOWS):
        cs = slice(g * pg, (g + 1) * pg)
        u = u_ref[:, cs]
        pa_ref[mg:mg + pad, :] = zeros_pad
        pa_ref[mg + pad:mg + pad + n_tok, :] = u
        pa_ref[mg + pad + n_tok:mg + tp, :] = zeros_pad
        src, dst = pa_ref, pb_ref
        dst[mg:mg + tp, :] = src[mg - unit:mg - unit + tp, :] + src[mg:mg + tp, :]
        src, dst = dst, src
        step, w = 1, 2
        while w < win:
            dst[mg:mg + tp, :] = (src[mg - step * unit:mg - step * unit + tp, :]
                                  + src[mg + step * unit:mg + step * unit + tp, :])
            src, dst = dst, src
            step, w = step * 2, w * 2
        total = src[mg + pad:mg + pad + n_tok, :]
        lo = jnp.maximum(row - win // 2, 0)
        hi = jnp.minimum(row - win // 2 + win, n_pos)
        count = (hi - lo).astype(F32)
        dlt = total / count - u
        y = _dot(dlt.astype(BF16), w_ref[g]) * sc_ref[:, cs]
        o_ref[:, cs] = y.astype(o_ref.dtype)


def _pool(px, pool_w, pool_scale, ymix, *, u_col0, out_col0, n_tok, unit, blk0, n_batch, name):
    width = pool_scale.shape[1]
    pg = width // len(POOL_WINDOWS)
    reach = 8 * unit
    pad = -(-reach // SUBLANES) * SUBLANES
    mg = -(-4 * unit // SUBLANES) * SUBLANES
    buf_rows = n_tok + 2 * pad + 2 * mg
    ub = u_col0 // width
    ob = out_col0 // width
    return pl.pallas_call(
        functools.partial(_pool_kernel, n_tok=n_tok, unit=unit, pg=pg),
        grid=(n_batch,),
        in_specs=[pl.BlockSpec((n_tok, width), lambda b: (blk0 + b, ub)),
                  pl.BlockSpec(pool_w.shape, lambda b: (0, 0, 0)),
                  pl.BlockSpec((1, width), lambda b: (0, 0)),
                  pl.BlockSpec(memory_space=pl.ANY)],
        out_specs=pl.BlockSpec((n_tok, width), lambda b: (blk0 + b, ob)),
        out_shape=jax.ShapeDtypeStruct(ymix.shape, ymix.dtype),
        input_output_aliases={3: 0},
        scratch_shapes=[pltpu.VMEM((buf_rows, pg), F32), pltpu.VMEM((buf_rows, pg), F32)],
        compiler_params=_cparams("parallel"),
        name=name,
    )(px, pool_w, pool_scale, ymix)


def _ba_permutation(n_heads):
    perm = np.zeros(4 * n_heads, np.int32)
    for p in range(n_heads // 2):
        for d in range(2):
            for s in range(2):
                for kind in range(2):
                    perm[p * 8 + d * 4 + s * 2 + kind] = d * 2 * n_heads + kind * n_heads + 2 * p + s
    return perm


def _gate_param_row(param, n_heads):
    row = jnp.zeros((LANES,), F32)
    idx, src_d, src_h = [], [], []
    for p in range(n_heads // 2):
        for d in range(2):
            for s in range(2):
                idx.append(p * 8 + d * 4 + s * 2 + 1)
                src_d.append(d)
                src_h.append(2 * p + s)
    row = row.at[np.array(idx)].set(param[np.array(src_d), np.array(src_h)])
    return row.reshape(1, LANES)


def _pick_tile(n, prefs):
    for t in prefs:
        if n % t == 0:
            return t
    raise ValueError(f"no tile for {n}")


def kernel(x, c, ctx, c_ctx, w_mod, b_mod, norm_ffn1, w_ffn1_gu, w_ffn1_down, norm_mix, w_in, conv_qkv, a_log,
           dt_bias, gdn_norm, conv_short, pool_w, pool_scale, w_out, norm_ffn2, w_ffn2_gu, w_ffn2_down, norm_final):
    B, S, D = x.shape
    CL = ctx.shape[1]
    L = w_mod.shape[0]
    n_vheads = a_log.shape[2]
    qk_dim = (n_vheads // 2) * HEAD_DIM
    v_dim = n_vheads * HEAD_DIM
    sc_dim = conv_short.shape[2]
    pool_dim = pool_scale.shape[1]
    n_ba = 4 * n_vheads
    n_pairs = n_vheads // 2
    bs_rows = B * S
    R = bs_rows + B * CL
    assert S % CHUNK == 0 and CL % CHUNK == 0 and S % CL == 0 and n_ba <= LANES

    tm = _pick_tile(math.gcd(S, R), (1024, 512, 256, 128))
    tg = _pick_tile(math.gcd(S, R), (512, 256, 128))
    tr = min(256, CL)
    tn = 512 if D % 512 == 0 else 256
    tn_ff = 256

    z_col0 = 2 * qk_dim + v_dim
    ba_col0 = z_col0 + v_dim
    sc_col0_src = ba_col0 + n_ba
    sc_col0 = ba_col0
    pool_col0 = sc_col0 + 3 * sc_dim
    ba_perm = _ba_permutation(n_vheads)

    h = jnp.concatenate([x.reshape(bs_rows, D), ctx.reshape(B * CL, D)], axis=0)
    c_all = jnp.zeros((SUBLANES, D), F32).at[0:B].set(c).at[B].set(c_ctx)
    mods_all = _modulation(c_all, w_mod, b_mod.reshape(L, 1, N_MOD * D)).reshape(L, SUBLANES, N_MOD, D)

    common = dict(seq_rows=S, n_batch=B)
    seg = dict(bs_rows=bs_rows, seq=S, ctx_len=CL)
    whole = lambda w: [(0, w.shape[2])]
    w1gu = w_ffn1_gu[0].astype(BF16)
    for l in range(L):
        n_live = bs_rows if l == L - 1 else R
        mods = mods_all[l]
        w_ba = jnp.zeros((D, LANES), BF16).at[:, 0:n_ba].set(w_in[l][:, ba_col0 + ba_perm].astype(BF16))

        hn = _adaln(h, norm_ffn1[l].reshape(1, D), mods, n_rows=R, shift_idx=0, scale_idx=1, tr=tr,
                    name="ffn1_adaln", **common)
        act, (w1d,) = _matmul(hn, w1gu, swiglu=True, out_dtype=BF16, tm=tm, tn=tn, name="ffn1_up",
                              casts=[(w_ffn1_down, l, whole(w_ffn1_down))])
        h, (w_main,) = _matmul_residual(act, w1d, h, mods, n_rows=R, gate_idx=2, scale=0.5, tm=tm, tn=tn_ff,
                                        name="ffn1_down", **common,
                                        casts=[(w_in, l, [(0, ba_col0), (sc_col0_src, w_in.shape[2])])])

        hn = _adaln(h, norm_mix[l].reshape(1, D), mods, n_rows=R, shift_idx=3, scale_idx=4, tr=tr,
                    name="mix_adaln", **common)
        px, (w2gu, wo) = _matmul(hn, w_main, swiglu=False, out_dtype=F32, tm=tm, tn=tn, name="in_proj",
                                 casts=[(w_ffn2_gu, l, whole(w_ffn2_gu)), (w_out, l, whole(w_out))])
        ba, _ = _matmul(hn, w_ba, swiglu=False, out_dtype=F32, tm=tm, tn=LANES, name="in_proj_ba")

        tc = _pick_tile(qk_dim, (512, 256, 128))
        conv = dict(tr=tr, tc=tc, **seg)
        (qn,) = _qkv_conv(px, conv_qkv[l][:, 0:qk_dim], kind='q', col0=0, width=qk_dim, **conv)
        kn, knt = _qkv_conv(px, conv_qkv[l][:, qk_dim:2 * qk_dim], kind='k', col0=qk_dim, width=qk_dim, **conv)
        (vn,) = _qkv_conv(px, conv_qkv[l][:, 2 * qk_dim:], kind='v', col0=2 * qk_dim, width=v_dim, **conv)
        cols, rows = _gdn_gates(ba, _gate_param_row(a_log[l], n_vheads), _gate_param_row(dt_bias[l], n_vheads),
                                n_pairs=n_pairs, tm=tg)
        ymix = _gdn(qn, kn, knt, vn, cols, rows, px, z_col0, gdn_norm[l].reshape(1, HEAD_DIM),
                    n_batch=B, seq=S, ctx_len=CL, out_cols=v_dim + sc_dim + pool_dim)

        ymix = _shortconv(px, conv_short[l], ymix, x_col0=sc_col0, width=sc_dim, out_col0=v_dim,
                          tr=tr, tc=_pick_tile(sc_dim, (512, 256, 128)), **seg)
        pw = pool_w[l].astype(BF16)
        psc = pool_scale[l].reshape(1, pool_dim)
        ymix = _pool(px, pw, psc, ymix, u_col0=pool_col0, out_col0=v_dim + sc_dim, n_tok=S, unit=GRID_W,
                     blk0=0, n_batch=B, name="pool_latent")
        ymix = _pool(px, pw, psc, ymix, u_col0=pool_col0, out_col0=v_dim + sc_dim, n_tok=CL, unit=1,
                     blk0=bs_rows // CL, n_batch=B, name="pool_context")
        h, _ = _matmul_residual(ymix, wo, h, mods, n_rows=n_live, gate_idx=5, scale=1.0, tm=tm, tn=tn,
                                name="mix_out", **common)

        hn = _adaln(h, norm_ffn2[l].reshape(1, D), mods, n_rows=n_live, shift_idx=6, scale_idx=7, tr=tr,
                    name="ffn2_adaln", **common)
        act, (w2d,) = _matmul(hn, w2gu, swiglu=True, out_dtype=BF16, tm=tm, tn=tn, name="ffn2_up",
                              casts=[(w_ffn2_down, l, whole(w_ffn2_down))])
        next_casts = [(w_ffn1_gu, l + 1, whole(w_ffn1_gu))] if l + 1 < L else []
        h, nxt = _matmul_residual(act, w2d, h, mods, n_rows=n_live, gate_idx=8, scale=0.5, tm=tm, tn=tn_ff,
                                  name="ffn2_down", **common, casts=next_casts)
        if nxt:
            w1gu = nxt[0]

    out = _final_rmsnorm(h, norm_final.reshape(1, D), bs_rows, tr)
    return out.reshape(B, S, D)
```

```python
import functools
import math

import numpy as np
import jax
import jax.numpy as jnp
from jax import lax
from jax.experimental import pallas as pl
from jax.experimental.pallas import tpu as pltpu

EPS = 1e-6
N_MOD = 9
GRID_W = 64
HEAD_DIM = 128
GDN_CONV = 5
SC_CONV = 3
POOL_WINDOWS = (2, 4, 8, 16)
CHUNK = 128
INV_BLOCK = 16
LANES = 128
SUBLANES = 8
BF16_SUBLANES = 16
VMEM_LIMIT = 56 * 1024 * 1024

BF16 = jnp.bfloat16
F32 = jnp.float32


def _cparams(*sem):
    return pltpu.CompilerParams(dimension_semantics=sem, vmem_limit_bytes=VMEM_LIMIT)


def _dot(a, b):
    return jnp.dot(a, b, preferred_element_type=F32)


def _dot_hi(a, b):
    return jnp.dot(a, b, preferred_element_type=F32, precision=lax.Precision.HIGHEST)


def _silu(v):
    return v * jax.nn.sigmoid(v)


def _mod_kernel(c_ref, w_ref, b_ref, o_ref):
    a = _silu(c_ref[...]).astype(BF16)
    o_ref[...] = _dot(a, w_ref[...].astype(BF16)) + b_ref[...]


def _modulation(c_all, w_mod, b_mod, tn=512):
    L, D, N = w_mod.shape
    return pl.pallas_call(
        _mod_kernel,
        grid=(L, N // tn),
        in_specs=[pl.BlockSpec((SUBLANES, D), lambda l, j: (0, 0)),
                  pl.BlockSpec((None, D, tn), lambda l, j: (l, 0, j)),
                  pl.BlockSpec((None, 1, tn), lambda l, j: (l, 0, j))],
        out_specs=pl.BlockSpec((None, SUBLANES, tn), lambda l, j: (l, 0, j)),
        out_shape=jax.ShapeDtypeStruct((L, SUBLANES, N), F32),
        compiler_params=_cparams("parallel", "parallel"),
        name="modulation",
    )(c_all, w_mod, b_mod)


def _adaln_kernel(h_ref, nw_ref, mod_ref, o_ref, *, shift_idx, scale_idx):
    h = h_ref[...]
    ms = jnp.mean(h * h, axis=-1, keepdims=True)
    y = h * lax.rsqrt(ms + EPS) * nw_ref[...]
    shift = mod_ref[0, shift_idx:shift_idx + 1, :]
    scale = mod_ref[0, scale_idx:scale_idx + 1, :]
    o_ref[...] = (y * (1.0 + scale) + shift).astype(o_ref.dtype)


def _adaln(h, norm_w, mods, *, n_rows, seq_rows, n_batch, shift_idx, scale_idx, tr, name):
    D = h.shape[1]
    mod_map = lambda i: (jnp.minimum((i * tr) // seq_rows, n_batch), 0, 0)
    return pl.pallas_call(
        functools.partial(_adaln_kernel, shift_idx=shift_idx, scale_idx=scale_idx),
        grid=(n_rows // tr,),
        in_specs=[pl.BlockSpec((tr, D), lambda i: (i, 0)),
                  pl.BlockSpec((1, D), lambda i: (0, 0)),
                  pl.BlockSpec((1, N_MOD, D), mod_map)],
        out_specs=pl.BlockSpec((tr, D), lambda i: (i, 0)),
        out_shape=jax.ShapeDtypeStruct((n_rows, D), BF16),
        compiler_params=_cparams("parallel"),
        name=name,
    )(h, norm_w, mods)


def _cast_row_block(rows, steps):
    for rps in range(BF16_SUBLANES, rows + 1, BF16_SUBLANES):
        if rows % rps == 0 and rows // rps <= steps:
            return rps
    raise ValueError(f"cannot spread {rows} rows over {steps} steps")


def _cast_plan(casts, n_i, nj):
    in_specs, out_specs, out_shapes, args, plans = [], [], [], [], []
    for src, layer, ranges, tile in casts:
        _, rows, cols = src.shape
        rps = _cast_row_block(rows, n_i * nj)
        last = rows // rps - 1
        out_cols = sum(b - a for a, b in ranges)
        src_map = lambda i, j, last=last, layer=layer: (layer, jnp.minimum(i * nj + j, last), 0)
        in_specs.append(pl.BlockSpec((None, rps, cols), src_map))
        if tile is None:
            out_specs.append(pl.BlockSpec((rps, out_cols), lambda i, j, last=last: (jnp.minimum(i * nj + j, last), 0)))
            out_shapes.append(jax.ShapeDtypeStruct((rows, out_cols), BF16))
            pieces = tuple((None, a, b) for a, b in ranges)
        else:
            assert all((b - a) % tile == 0 for a, b in ranges)
            nt = out_cols // tile
            out_specs.append(pl.BlockSpec((nt, rps, tile),
                                          lambda i, j, last=last: (0, jnp.minimum(i * nj + j, last), 0)))
            out_shapes.append(jax.ShapeDtypeStruct((nt, rows, tile), BF16))
            pieces = tuple((None, s, s + tile) for a, b in ranges for s in range(a, b, tile))
            pieces = tuple((t, a, b) for t, (_, a, b) in enumerate(pieces))
        args.append(src)
        plans.append(pieces)
    return in_specs, out_specs, out_shapes, args, plans


def _run_casts(src_refs, dst_refs, plans):
    for src_ref, dst_ref, pieces in zip(src_refs, dst_refs, plans):
        off = 0
        for t, a, b in pieces:
            if t is None:
                dst_ref[:, off:off + b - a] = src_ref[:, a:b].astype(dst_ref.dtype)
                off += b - a
            else:
                dst_ref[t] = src_ref[:, a:b].astype(dst_ref.dtype)


def _mm_kernel(a_ref, *rest, swiglu, cast_ranges):
    nc = len(cast_ranges)
    n_w = 2 if swiglu else 1
    w_refs, src_refs = rest[:n_w], rest[n_w:n_w + nc]
    o_ref, dst_refs = rest[n_w + nc], rest[n_w + nc + 1:]
    a = a_ref[...]
    if swiglu:
        g = _dot(a, w_refs[0][...])
        u = _dot(a, w_refs[1][...])
        o_ref[...] = (_silu(g) * u).astype(o_ref.dtype)
    else:
        o_ref[...] = _dot(a, w_refs[0][...]).astype(o_ref.dtype)
    _run_casts(src_refs, dst_refs, cast_ranges)


def _matmul(a, w, *, swiglu, out_dtype, tm, name, casts=()):
    R, K = a.shape
    nt, _, tn = w.shape
    nj = nt // 2 if swiglu else nt
    n_out = nj * tn
    in_specs = [pl.BlockSpec((tm, K), lambda i, j: (i, 0)),
                pl.BlockSpec((None, K, tn), lambda i, j: (j, 0, 0))]
    args = [a, w]
    if swiglu:
        in_specs.append(pl.BlockSpec((None, K, tn), lambda i, j: (j + nj, 0, 0)))
        args.append(w)
    c_in, c_out, c_shapes, c_args, c_ranges = _cast_plan(casts, R // tm, nj)
    res = pl.pallas_call(
        functools.partial(_mm_kernel, swiglu=swiglu, cast_ranges=c_ranges),
        grid=(R // tm, nj),
        in_specs=in_specs + c_in,
        out_specs=[pl.BlockSpec((tm, tn), lambda i, j: (i, j))] + c_out,
        out_shape=[jax.ShapeDtypeStruct((R, n_out), out_dtype)] + c_shapes,
        compiler_params=_cparams("arbitrary", "arbitrary"),
        name=name,
    )(*args, *c_args)
    return res[0], res[1:]


def _mm_res_kernel(a_ref, w_ref, res_ref, mod_ref, *rest, gate_idx, scale, cast_ranges):
    nc = len(cast_ranges)
    src_refs, o_ref, dst_refs = rest[:nc], rest[nc], rest[nc + 1:]
    acc = _dot(a_ref[...], w_ref[...])
    gate = mod_ref[0, gate_idx:gate_idx + 1, :]
    o_ref[...] = res_ref[...] + (scale * gate) * acc
    _run_casts(src_refs, dst_refs, cast_ranges)


def _matmul_residual(a, w, res, mods, *, n_rows, seq_rows, n_batch, gate_idx, scale, tm, name, casts=()):
    K = a.shape[1]
    nt, _, tn = w.shape
    D = nt * tn
    mod_map = lambda i, j: (jnp.minimum((i * tm) // seq_rows, n_batch), 0, j)
    c_in, c_out, c_shapes, c_args, c_ranges = _cast_plan(casts, n_rows // tm, D // tn)
    out = pl.pallas_call(
        functools.partial(_mm_res_kernel, gate_idx=gate_idx, scale=scale, cast_ranges=c_ranges),
        grid=(n_rows // tm, D // tn),
        in_specs=[pl.BlockSpec((tm, K), lambda i, j: (i, 0)),
                  pl.BlockSpec((None, K, tn), lambda i, j: (j, 0, 0)),
                  pl.BlockSpec((tm, tn), lambda i, j: (i, j)),
                  pl.BlockSpec((1, N_MOD, tn), mod_map)] + c_in,
        out_specs=[pl.BlockSpec((tm, tn), lambda i, j: (i, j))] + c_out,
        out_shape=[jax.ShapeDtypeStruct((n_rows, D), F32)] + c_shapes,
        compiler_params=_cparams("arbitrary", "arbitrary"),
        name=name,
    )(a, w, res, mods, *c_args)
    return out[0], out[1:]


def _rmsnorm_kernel(h_ref, w_ref, o_ref):
    h = h_ref[...]
    ms = jnp.mean(h * h, axis=-1, keepdims=True)
    o_ref[...] = h * lax.rsqrt(ms + EPS) * w_ref[...]


def _final_rmsnorm(h, w, rows, tm):
    D = h.shape[1]
    return pl.pallas_call(
        _rmsnorm_kernel,
        grid=(rows // tm,),
        in_specs=[pl.BlockSpec((tm, D), lambda i: (i, 0)),
                  pl.BlockSpec((1, D), lambda i: (0, 0))],
        out_specs=pl.BlockSpec((tm, D), lambda i: (i, 0)),
        out_shape=jax.ShapeDtypeStruct((rows, D), F32),
        compiler_params=_cparams("parallel"),
        name="final_rmsnorm",
    )(h, w)


def _segment_position(tile_rows, bs_rows, seq, ctx_len):
    row0 = pl.program_id(0) * tile_rows
    in_x = row0 < bs_rows
    off = jnp.where(in_x, lax.rem(row0, seq), lax.rem(row0 - bs_rows, ctx_len))
    seglen = jnp.where(in_x, seq, ctx_len)
    return in_x, off, seglen


def _qkv_conv_kernel(cur_ref, prev_ref, next_ref, w_ref, *rest, kind, tr, bs_rows, seq, ctx_len):
    if kind == 'k':
        o_ref, ot_ref, ext_ref = rest
    else:
        o_ref, ext_ref = rest
    _, off, seglen = _segment_position(tr, bs_rows, seq, ctx_len)
    is_start = off == 0
    is_end = off + tr == seglen
    pad = SUBLANES
    half = GDN_CONV // 2
    ext_ref[0:pad, :] = jnp.where(is_start, 0.0, prev_ref[...])
    ext_ref[pad:pad + tr, :] = cur_ref[...]
    ext_ref[pad + tr:pad + tr + pad, :] = jnp.where(is_end, 0.0, next_ref[...])
    ext = ext_ref[...]
    n_ext = tr + 2 * pad
    acc = None
    for j in range(GDN_CONV):
        shifted = ext if j == half else pltpu.roll(ext, (half - j) % n_ext, axis=0)
        term = shifted[pad:pad + tr, :] * w_ref[j:j + 1, :]
        acc = term if acc is None else acc + term
    y = _silu(acc)
    tc = y.shape[1]
    if kind == 'v':
        o_ref[...] = y.astype(o_ref.dtype)
        return
    mult = HEAD_DIM ** -0.5 if kind == 'q' else 1.0
    for hd in range(tc // HEAD_DIM):
        yh = y[:, hd * HEAD_DIM:(hd + 1) * HEAD_DIM]
        inv = lax.rsqrt(jnp.sum(yh * yh, axis=-1, keepdims=True) + EPS)
        yn = yh * inv
        if kind == 'q':
            yn = yn * mult
        o_ref[:, hd * HEAD_DIM:(hd + 1) * HEAD_DIM] = yn.astype(o_ref.dtype)
        if kind == 'k':
            ot_ref[hd * HEAD_DIM:(hd + 1) * HEAD_DIM, :] = yn.T.astype(ot_ref.dtype)


def _qkv_conv(px, conv_w, *, kind, col0, width, tr, tc, bs_rows, seq, ctx_len):
    R = px.shape[0]
    cb0 = col0 // tc
    rb = tr // SUBLANES
    last = R // SUBLANES - 1
    in_specs = [pl.BlockSpec((tr, tc), lambda i, j: (i, cb0 + j)),
                pl.BlockSpec((SUBLANES, tc), lambda i, j: (jnp.maximum(i * rb - 1, 0), cb0 + j)),
                pl.BlockSpec((SUBLANES, tc), lambda i, j: (jnp.minimum((i + 1) * rb, last), cb0 + j)),
                pl.BlockSpec((GDN_CONV, tc), lambda i, j: (0, j))]
    out_specs = [pl.BlockSpec((tr, tc), lambda i, j: (i, j))]
    out_shape = [jax.ShapeDtypeStruct((R, width), BF16)]
    if kind == 'k':
        out_specs.append(pl.BlockSpec((tc, tr), lambda i, j: (j, i)))
        out_shape.append(jax.ShapeDtypeStruct((width, R), BF16))
    res = pl.pallas_call(
        functools.partial(_qkv_conv_kernel, kind=kind, tr=tr, bs_rows=bs_rows, seq=seq, ctx_len=ctx_len),
        grid=(R // tr, width // tc),
        in_specs=in_specs,
        out_specs=out_specs,
        out_shape=out_shape,
        scratch_shapes=[pltpu.VMEM((tr + 2 * SUBLANES, tc), F32)],
        compiler_params=_cparams("parallel", "parallel"),
        name="gdn_conv_" + kind,
    )(px, px, px, conv_w)
    return res


def _gate_kernel(ba_ref, alog_ref, dtb_ref, cols_ref, rows_ref, *, tm, n_pairs):
    lane = lax.broadcasted_iota(jnp.int32, (CHUNK, LANES), 1)
    is_beta = (lane & 1) == 0
    is_bwd = (lane & 4) != 0
    ri = lax.broadcasted_iota(jnp.int32, (CHUNK, CHUNK), 0)
    ci = lax.broadcasted_iota(jnp.int32, (CHUNK, CHUNK), 1)
    tri_lo = jnp.where(ci <= ri, 1.0, 0.0)
    tri_up = jnp.where(ci >= ri, 1.0, 0.0)
    neg_a = -jnp.exp(alog_ref[...])
    for c in range(tm // CHUNK):
        sl = slice(c * CHUNK, (c + 1) * CHUNK)
        x = ba_ref[sl, :]
        beta = jax.nn.sigmoid(x)
        y = x + dtb_ref[...]
        softplus = jnp.maximum(y, 0.0) + jnp.log1p(jnp.exp(-jnp.abs(y)))
        g = jnp.where(is_beta, 0.0, neg_a * softplus)
        gc = jnp.where(is_bwd, _dot_hi(tri_up, g), _dot_hi(tri_lo, g))
        full = jnp.where(is_beta, beta, gc)
        for p in range(n_pairs):
            cols_ref[p, sl, :] = full[:, p * 8:(p + 1) * 8]
        rows_ref[:, sl] = full.T[0:n_pairs * 8, :]


def _gdn_gates(ba, alog_row, dtb_row, *, n_pairs, tm):
    R = ba.shape[0]
    return pl.pallas_call(
        functools.partial(_gate_kernel, tm=tm, n_pairs=n_pairs),
        grid=(R // tm,),
        in_specs=[pl.BlockSpec((tm, LANES), lambda i: (i, 0)),
                  pl.BlockSpec((1, LANES), lambda i: (0, 0)),
                  pl.BlockSpec((1, LANES), lambda i: (0, 0))],
        out_specs=[pl.BlockSpec((n_pairs, tm, 8), lambda i: (0, i, 0)),
                   pl.BlockSpec((n_pairs * 8, tm), lambda i: (0, i))],
        out_shape=[jax.ShapeDtypeStruct((n_pairs, R, 8), F32),
                   jax.ShapeDtypeStruct((n_pairs * 8, R), F32)],
        compiler_params=_cparams("parallel"),
        name="gdn_gates",
    )(ba, alog_row, dtb_row)


def _gdn_kernel(qx_ref, qc_ref, kx_ref, kc_ref, ktx_ref, ktc_ref, vx_ref, vc_ref,
                colx_ref, colc_ref, rowx_ref, rowc_ref, z_ref, nw_ref, o_ref,
                t_ref, a_ref, of_ref, ob_ref, sf_ref, sb_ref, *, seq, ctx_len, rb, n_pair, prep_chunks):
    C = CHUNK
    HD = HEAD_DIM
    n_xc = seq // C
    n_cc = ctx_len // C
    r = pl.program_id(2)
    x_refs = (qx_ref, kx_ref, ktx_ref, vx_ref, colx_ref, rowx_ref)
    c_refs = (qc_ref, kc_ref, ktc_ref, vc_ref, colc_ref, rowc_ref)

    ri = lax.broadcasted_iota(jnp.int32, (C, C), 0)
    ci = lax.broadcasted_iota(jnp.int32, (C, C), 1)
    eye = (ri == ci).astype(F32)

    def lane_idx(d, s, kind):
        return d * 4 + s * 2 + kind

    block_diff = ri ^ ci

    def unit_triangular_inverses(nmats):
        base = int(math.log2(INV_BLOCK))
        n0s = [jnp.where((block_diff >> base) == 0, n, 0.0) for n in nmats]
        invs = [eye + n0 for n0 in n0s]
        pws = n0s
        for _ in range(base - 1):
            pwbs = [pw.astype(BF16) for pw in pws]
            pws = [_dot(b, b) for b in pwbs]
            invs = [inv + _dot(inv.astype(BF16), pw.astype(BF16)) for inv, pw in zip(invs, pws)]
        for level in range(base, int(math.log2(C))):
            offs = [jnp.where((block_diff >> level) == 1, n, 0.0).astype(BF16) for n in nmats]
            invbs = [inv.astype(BF16) for inv in invs]
            halves = [_dot(ib, off).astype(BF16) for ib, off in zip(invbs, offs)]
            invs = [inv + _dot(h, ib) for inv, h, ib in zip(invs, halves, invbs)]
        return invs

    def prepare(chunks):
        nmats, dests = [], []
        for refs, row0, slot in chunks:
            q_ref, k_ref, kt_ref, _, col_ref, row_ref = refs
            rows = pl.ds(row0, C)
            for u in range(n_pair):
                hs = slice(u * HD, (u + 1) * HD)
                qk = jnp.concatenate([q_ref[rows, hs], k_ref[rows, hs]], axis=0)
                gram = _dot(qk, kt_ref[hs, rows])
                qkt = gram[0:C, :]
                kkt = gram[C:2 * C, :]
                cols = col_ref[u, rows, :]
                for d in range(2):
                    incl = (ci <= ri) if d == 0 else (ci >= ri)
                    strict = (ci < ri) if d == 0 else (ci > ri)
                    for s in range(2):
                        kb, kg = lane_idx(d, s, 0), lane_idx(d, s, 1)
                        beta_col = cols[:, kb:kb + 1]
                        gc_col = cols[:, kg:kg + 1]
                        gc_row = row_ref[u * 8 + kg:u * 8 + kg + 1, rows]
                        decay = jnp.where(incl, jnp.exp(jnp.where(incl, gc_col - gc_row, 0.0)), 0.0)
                        nmats.append(jnp.where(strict, -(kkt * beta_col) * decay, 0.0))
                        a_ref[slot, u * 4 + d * 2 + s] = (qkt * decay).astype(BF16)
                        dests.append((slot, u * 4 + d * 2 + s))
        for (slot, idx), inv in zip(dests, unit_triangular_inverses(nmats)):
            t_ref[slot, idx] = inv.astype(BF16)

    def chain(tasks):
        units = [(d, u, refs, pl.ds(row0, C), slot, orow0)
                 for d, refs, row0, slot, orow0 in tasks for u in range(n_pair)]
        states, projs = [], []
        for d, u, refs, rows, _, _ in units:
            hs = slice(u * HD, (u + 1) * HD)
            state = (sf_ref if d == 0 else sb_ref)[u]
            qk = jnp.concatenate([refs[0][rows, hs], refs[1][rows, hs]], axis=0)
            states.append(state)
            projs.append(_dot(qk, state.astype(BF16)))
        scalars, rhss = [], []
        for (d, u, refs, rows, _, _), proj in zip(units, projs):
            cols = refs[4][u, rows, :]
            for s in range(2):
                kb, kg = lane_idx(d, s, 0), lane_idx(d, s, 1)
                beta_col = cols[:, kb:kb + 1]
                gc_col = cols[:, kg:kg + 1]
                g_last = gc_col[C - 1:C, :] if d == 0 else gc_col[0:1, :]
                e_col = jnp.exp(gc_col)
                vs = slice((2 * u + s) * HD, (2 * u + s + 1) * HD)
                v = refs[3][rows, vs].astype(F32)
                k_s = proj[C:2 * C, s * HD:(s + 1) * HD]
                rhss.append((beta_col * (v - e_col * k_s)).astype(BF16))
                scalars.append((gc_col, e_col, g_last))
        v_news = []
        for i, (d, u, _, _, slot, _) in enumerate(units):
            for s in range(2):
                v_news.append(_dot(t_ref[slot, u * 4 + d * 2 + s], rhss[2 * i + s]))
        for i, (d, u, refs, rows, slot, orow0) in enumerate(units):
            o_acc = of_ref if d == 0 else ob_ref
            evs, gls = [], []
            for s in range(2):
                gc_col, e_col, g_last = scalars[2 * i + s]
                v_new = v_news[2 * i + s]
                q_s = projs[i][0:C, s * HD:(s + 1) * HD]
                out = e_col * q_s + _dot(a_ref[slot, u * 4 + d * 2 + s], v_new.astype(BF16))
                o_acc[pl.ds(orow0, C), (2 * u + s) * HD:(2 * u + s + 1) * HD] = out
                evs.append(jnp.exp(g_last - gc_col) * v_new)
                gls.append(jnp.broadcast_to(jnp.exp(g_last), (HD, HD)))
            ev = jnp.concatenate(evs, axis=1).astype(BF16)
            new_state = states[i] * jnp.concatenate(gls, axis=1) + _dot(refs[2][u * HD:(u + 1) * HD, rows], ev)
            (sf_ref if d == 0 else sb_ref)[u] = new_state

    @pl.when(r == 0)
    def _():
        sf_ref[...] = jnp.zeros_like(sf_ref)
        sb_ref[...] = jnp.zeros_like(sb_ref)
        prepare([(c_refs, n * C, n) for n in range(n_cc)])

        def prep_body(it, carry):
            n0 = it * prep_chunks
            prepare([(x_refs, pl.multiple_of((n0 + j) * C, C), n_cc + n0 + j) for j in range(prep_chunks)])
            return carry
        lax.fori_loop(0, n_xc // prep_chunks, prep_body, 0)

        for n in range(n_cc):
            m = n_cc - 1 - n
            chain([(0, c_refs, n * C, n, seq + n * C), (1, c_refs, m * C, m, seq + m * C)])

        def chain_body(n, carry):
            m = n_xc - 1 - n
            rf = pl.multiple_of(n * C, C)
            rbk = pl.multiple_of(m * C, C)
            chain([(0, x_refs, rf, n_cc + n, rf), (1, x_refs, rbk, n_cc + m, rbk)])
            return carry
        lax.fori_loop(0, n_xc, chain_body, 0)

    orow = pl.multiple_of(r * rb, rb)
    o = of_ref[pl.ds(orow, rb), :] + ob_ref[pl.ds(orow, rb), :]
    z = z_ref[...]
    for hd in range(2 * n_pair):
        hs = slice(hd * HD, (hd + 1) * HD)
        oh = o[:, hs]
        ms = jnp.mean(oh * oh, axis=-1, keepdims=True)
        y = oh * lax.rsqrt(ms + EPS) * nw_ref[...]
        o_ref[:, hs] = (y * _silu(z[:, hs])).astype(o_ref.dtype)


def _gdn(qn, kn, knt, vn, cols, rows, px, z_col0, norm_w, *, n_batch, seq, ctx_len, out_cols):
    R = qn.shape[0]
    n_pairs = qn.shape[1] // HEAD_DIM
    C = CHUNK
    rb = min(256, ctx_len)
    nxb, ncb = seq // rb, ctx_len // rb
    cblk0 = n_batch * (seq // ctx_len)
    zb0 = z_col0 // (2 * HEAD_DIM)
    n_slots = (seq + ctx_len) // C
    n_pair = 2 if n_pairs % 2 == 0 else 1
    prep_chunks = 2 if (seq // C) % 2 == 0 else 1
    qw = n_pair * HEAD_DIM
    vw = 2 * qw
    assert z_col0 % vw == 0

    def row_block(b, r):
        return jnp.where(r < nxb, b * nxb + r, n_batch * nxb + b * ncb + (r - nxb))

    x_rows = lambda w: pl.BlockSpec((seq, w), lambda b, p, r: (b, p))
    c_rows = lambda w: pl.BlockSpec((ctx_len, w), lambda b, p, r: (cblk0 + b, p))
    in_specs = [
        x_rows(qw), c_rows(qw),
        x_rows(qw), c_rows(qw),
        pl.BlockSpec((qw, seq), lambda b, p, r: (p, b)),
        pl.BlockSpec((qw, ctx_len), lambda b, p, r: (p, cblk0 + b)),
        x_rows(vw), c_rows(vw),
        pl.BlockSpec((n_pair, seq, 8), lambda b, p, r: (p, b, 0)),
        pl.BlockSpec((n_pair, ctx_len, 8), lambda b, p, r: (p, cblk0 + b, 0)),
        pl.BlockSpec((n_pair * 8, seq), lambda b, p, r: (p, b)),
        pl.BlockSpec((n_pair * 8, ctx_len), lambda b, p, r: (p, cblk0 + b)),
        pl.BlockSpec((rb, vw), lambda b, p, r: (row_block(b, r), z_col0 // vw + p)),
        pl.BlockSpec((1, HEAD_DIM), lambda b, p, r: (0, 0)),
    ]
    return pl.pallas_call(
        functools.partial(_gdn_kernel, seq=seq, ctx_len=ctx_len, rb=rb, n_pair=n_pair, prep_chunks=prep_chunks),
        grid=(n_batch, n_pairs // n_pair, nxb + ncb),
        in_specs=in_specs,
        out_specs=pl.BlockSpec((rb, vw), lambda b, p, r: (row_block(b, r), p)),
        out_shape=jax.ShapeDtypeStruct((R, out_cols), BF16),
        scratch_shapes=[pltpu.VMEM((n_slots, n_pair * 4, C, C), BF16),
                        pltpu.VMEM((n_slots, n_pair * 4, C, C), BF16),
                        pltpu.VMEM((seq + ctx_len, vw), F32),
                        pltpu.VMEM((seq + ctx_len, vw), F32),
                        pltpu.VMEM((n_pair, HEAD_DIM, 2 * HEAD_DIM), F32),
                        pltpu.VMEM((n_pair, HEAD_DIM, 2 * HEAD_DIM), F32)],
        compiler_params=_cparams("parallel", "parallel", "arbitrary"),
        name="gdn_delta_rule",
    )(qn, qn, kn, kn, knt, knt, vn, vn, cols, cols, rows, rows, px, norm_w)


def _shortconv_kernel(x_ref, b_ref, c_ref, w_ref, ymix_ref, o_ref, *, tr, bs_rows, seq, ctx_len):
    del ymix_ref
    in_x, off, _ = _segment_position(tr, bs_rows, seq, ctx_len)
    period = jnp.where(in_x, GRID_W, ctx_len)
    pos = (off + lax.broadcasted_iota(jnp.int32, (tr, 1), 0)) & (period - 1)
    v = c_ref[...] * x_ref[...]
    left = jnp.where(pos == 0, 0.0, pltpu.roll(v, 1, axis=0))
    right = jnp.where(pos == period - 1, 0.0, pltpu.roll(v, tr - 1, axis=0))
    y = left * w_ref[0:1, :] + v * w_ref[1:2, :] + right * w_ref[2:3, :]
    o_ref[...] = (b_ref[...] * y).astype(o_ref.dtype)


def _shortconv(px, conv_w, ymix, *, x_col0, width, out_col0, tr, tc, bs_rows, seq, ctx_len):
    R = px.shape[0]
    nb = width // tc
    xb0 = x_col0 // tc
    ob0 = out_col0 // tc
    return pl.pallas_call(
        functools.partial(_shortconv_kernel, tr=tr, bs_rows=bs_rows, seq=seq, ctx_len=ctx_len),
        grid=(R // tr, nb),
        in_specs=[pl.BlockSpec((tr, tc), lambda i, j: (i, xb0 + j)),
                  pl.BlockSpec((tr, tc), lambda i, j: (i, xb0 + nb + j)),
                  pl.BlockSpec((tr, tc), lambda i, j: (i, xb0 + 2 * nb + j)),
                  pl.BlockSpec((SC_CONV, tc), lambda i, j: (0, j)),
                  pl.BlockSpec(memory_space=pl.ANY)],
        out_specs=pl.BlockSpec((tr, tc), lambda i, j: (i, ob0 + j)),
        out_shape=jax.ShapeDtypeStruct(ymix.shape, ymix.dtype),
        input_output_aliases={4: 0},
        compiler_params=_cparams("parallel", "parallel"),
        name="short_conv",
    )(px, px, px, conv_w, ymix)


def _pool_kernel(u_ref, w_ref, sc_ref, ymix_ref, o_ref, pa_ref, pb_ref, *, n_tok, unit, pg):
    del ymix_ref
    n_pos = n_tok // unit
    reach = 8 * unit
    pad = -(-reach // SUBLANES) * SUBLANES
    mg = -(-4 * unit // SUBLANES) * SUBLANES
    tp = n_tok + 2 * pad
    zeros_mg = jnp.zeros((mg, pg), F32)
    zeros_pad = jnp.zeros((pad, pg), F32)
    for buf in (pa_ref, pb_ref):
        buf[0:mg, :] = zeros_mg
        buf[mg + tp:mg + tp + mg, :] = zeros_mg
    row = lax.broadcasted_iota(jnp.int32, (n_tok, 1), 0) // unit
    for g, win in enumerate(POOL_WINDOWS):
        cs = slice(g * pg, (g + 1) * pg)
        u = u_ref[:, cs]
        pa_ref[mg:mg + pad, :] = zeros_pad
        pa_ref[mg + pad:mg + pad + n_tok, :] = u
        pa_ref[mg + pad + n_tok:mg + tp, :] = zeros_pad
        src, dst = pa_ref, pb_ref
        dst[mg:mg + tp, :] = src[mg - unit:mg - unit + tp, :] + src[mg:mg + tp, :]
        src, dst = dst, src
        step, w = 1, 2
        while w < win:
            dst[mg:mg + tp, :] = (src[mg - step * unit:mg - step * unit + tp, :]
                                  + src[mg + step * unit:mg + step * unit + tp, :])
            src, dst = dst, src
            step, w = step * 2, w * 2
        total = src[mg + pad:mg + pad + n_tok, :]
        lo = jnp.maximum(row - win // 2, 0)
        hi = jnp.minimum(row - win // 2 + win, n_pos)
        count = (hi - lo).astype(F32)
        dlt = total / count - u
        y = _dot(dlt.astype(BF16), w_ref[g]) * sc_ref[:, cs]
        o_ref[:, cs] = y.astype(o_ref.dtype)


def _pool(px, pool_w, pool_scale, ymix, *, u_col0, out_col0, n_tok, unit, blk0, n_batch, name):
    width = pool_scale.shape[1]
    pg = width // len(POOL_WINDOWS)
    reach = 8 * unit
    pad = -(-reach // SUBLANES) * SUBLANES
    mg = -(-4 * unit // SUBLANES) * SUBLANES
    buf_rows = n_tok + 2 * pad + 2 * mg
    ub = u_col0 // width
    ob = out_col0 // width
    return pl.pallas_call(
        functools.partial(_pool_kernel, n_tok=n_tok, unit=unit, pg=pg),
        grid=(n_batch,),
        in_specs=[pl.BlockSpec((n_tok, width), lambda b: (blk0 + b, ub)),
                  pl.BlockSpec(pool_w.shape, lambda b: (0, 0, 0)),
                  pl.BlockSpec((1, width), lambda b: (0, 0)),
                  pl.BlockSpec(memory_space=pl.ANY)],
        out_specs=pl.BlockSpec((n_tok, width), lambda b: (blk0 + b, ob)),
        out_shape=jax.ShapeDtypeStruct(ymix.shape, ymix.dtype),
        input_output_aliases={3: 0},
        scratch_shapes=[pltpu.VMEM((buf_rows, pg), F32), pltpu.VMEM((buf_rows, pg), F32)],
        compiler_params=_cparams("parallel"),
        name=name,
    )(px, pool_w, pool_scale, ymix)


def _ba_permutation(n_heads):
    perm = np.zeros(4 * n_heads, np.int32)
    for p in range(n_heads // 2):
        for d in range(2):
            for s in range(2):
                for kind in range(2):
                    perm[p * 8 + d * 4 + s * 2 + kind] = d * 2 * n_heads + kind * n_heads + 2 * p + s
    return perm


def _gate_param_row(param, n_heads):
    row = jnp.zeros((LANES,), F32)
    idx, src_d, src_h = [], [], []
    for p in range(n_heads // 2):
        for d in range(2):
            for s in range(2):
                idx.append(p * 8 + d * 4 + s * 2 + 1)
                src_d.append(d)
                src_h.append(2 * p + s)
    row = row.at[np.array(idx)].set(param[np.array(src_d), np.array(src_h)])
    return row.reshape(1, LANES)


def _pick_tile(n, prefs):
    for t in prefs:
        if n % t == 0:
            return t
    raise ValueError(f"no tile for {n}")


def kernel(x, c, ctx, c_ctx, w_mod, b_mod, norm_ffn1, w_ffn1_gu, w_ffn1_down, norm_mix, w_in, conv_qkv, a_log,
           dt_bias, gdn_norm, conv_short, pool_w, pool_scale, w_out, norm_ffn2, w_ffn2_gu, w_ffn2_down, norm_final):
    B, S, D = x.shape
    CL = ctx.shape[1]
    L = w_mod.shape[0]
    n_vheads = a_log.shape[2]
    qk_dim = (n_vheads // 2) * HEAD_DIM
    v_dim = n_vheads * HEAD_DIM
    sc_dim = conv_short.shape[2]
    pool_dim = pool_scale.shape[1]
    n_ba = 4 * n_vheads
    n_pairs = n_vheads // 2
    bs_rows = B * S
    R = bs_rows + B * CL
    assert S % CHUNK == 0 and CL % CHUNK == 0 and S % CL == 0 and n_ba <= LANES

    tm = _pick_tile(math.gcd(S, R), (1024, 512, 256, 128))
    tg = _pick_tile(math.gcd(S, R), (512, 256, 128))
    tr = min(256, CL)
    tn = 512 if D % 512 == 0 else 256
    tn_ff = 256
    tc_qk = _pick_tile(qk_dim, (1024, 512, 256, 128))
    tc_v = math.gcd(v_dim, 2 * tc_qk)

    z_col0 = 2 * qk_dim + v_dim
    ba_col0 = z_col0 + v_dim
    sc_col0_src = ba_col0 + n_ba
    sc_col0 = ba_col0
    pool_col0 = sc_col0 + 3 * sc_dim
    ba_perm = _ba_permutation(n_vheads)
    onehot = np.zeros((n_ba, LANES), np.float32)
    onehot[ba_perm, np.arange(n_ba)] = 1.0
    ba_onehot = jnp.asarray(onehot, dtype=BF16)

    h = jnp.concatenate([x.reshape(bs_rows, D), ctx.reshape(B * CL, D)], axis=0)
    c_all = jnp.zeros((SUBLANES, D), F32).at[0:B].set(c).at[B].set(c_ctx)
    mods_all = _modulation(c_all, w_mod, b_mod.reshape(L, 1, N_MOD * D)).reshape(L, SUBLANES, N_MOD, D)

    common = dict(seq_rows=S, n_batch=B)
    seg = dict(bs_rows=bs_rows, seq=S, ctx_len=CL)
    whole = lambda w: [(0, w.shape[2])]
    d_ff2 = w_ffn1_gu.shape[2]
    w1gu = w_ffn1_gu[0].astype(BF16).reshape(D, d_ff2 // tn, tn).transpose(1, 0, 2)
    for l in range(L):
        n_live = bs_rows if l == L - 1 else R
        mods = mods_all[l]

        hn = _adaln(h, norm_ffn1[l].reshape(1, D), mods, n_rows=R, shift_idx=0, scale_idx=1, tr=tr,
                    name="ffn1_adaln", **common)
        act, (w1d,) = _matmul(hn, w1gu, swiglu=True, out_dtype=BF16, tm=tm, name="ffn1_up",
                              casts=[(w_ffn1_down, l, whole(w_ffn1_down), tn_ff)])
        h, (w_main, w_ba_src) = _matmul_residual(
            act, w1d, h, mods, n_rows=R, gate_idx=2, scale=0.5, tm=tm, name="ffn1_down", **common,
            casts=[(w_in, l, [(0, ba_col0), (sc_col0_src, w_in.shape[2])], tn),
                   (w_in, l, [(ba_col0, sc_col0_src)], None)])
        w_ba = jnp.dot(w_ba_src, ba_onehot, preferred_element_type=F32).astype(BF16).reshape(1, D, LANES)

        hn = _adaln(h, norm_mix[l].reshape(1, D), mods, n_rows=R, shift_idx=3, scale_idx=4, tr=tr,
                    name="mix_adaln", **common)
        px, (w2gu, wo) = _matmul(hn, w_main, swiglu=False, out_dtype=F32, tm=tm, name="in_proj",
                                 casts=[(w_ffn2_gu, l, whole(w_ffn2_gu), tn), (w_out, l, whole(w_out), tn)])
        ba, _ = _matmul(hn, w_ba, swiglu=False, out_dtype=F32, tm=tm, name="in_proj_ba")

        conv = dict(tr=tr, **seg)
        (qn,) = _qkv_conv(px, conv_qkv[l][:, 0:qk_dim], kind='q', col0=0, width=qk_dim, tc=tc_qk, **conv)
        kn, knt = _qkv_conv(px, conv_qkv[l][:, qk_dim:2 * qk_dim], kind='k', col0=qk_dim, width=qk_dim,
                            tc=tc_qk, **conv)
        (vn,) = _qkv_conv(px, conv_qkv[l][:, 2 * qk_dim:], kind='v', col0=2 * qk_dim, width=v_dim,
                          tc=tc_v, **conv)
        cols, rows = _gdn_gates(ba, _gate_param_row(a_log[l], n_vheads), _gate_param_row(dt_bias[l], n_vheads),
                                n_pairs=n_pairs, tm=tg)
        ymix = _gdn(qn, kn, knt, vn, cols, rows, px, z_col0, gdn_norm[l].reshape(1, HEAD_DIM),
                    n_batch=B, seq=S, ctx_len=CL, out_cols=v_dim + sc_dim + pool_dim)

        ymix = _shortconv(px, conv_short[l], ymix, x_col0=sc_col0, width=sc_dim, out_col0=v_dim,
                          tr=tr, tc=_pick_tile(sc_dim, (1024, 512, 256, 128)), **seg)
        pw = pool_w[l].astype(BF16)
        psc = pool_scale[l].reshape(1, pool_dim)
        ymix = _pool(px, pw, psc, ymix, u_col0=pool_col0, out_col0=v_dim + sc_dim, n_tok=S, unit=GRID_W,
                     blk0=0, n_batch=B, name="pool_latent")
        ymix = _pool(px, pw, psc, ymix, u_col0=pool_col0, out_col0=v_dim + sc_dim, n_tok=CL, unit=1,
                     blk0=bs_rows // CL, n_batch=B, name="pool_context")
        h, _ = _matmul_residual(ymix, wo, h, mods, n_rows=n_live, gate_idx=5, scale=1.0, tm=tm,
                                name="mix_out", **common)

        hn = _adaln(h, norm_ffn2[l].reshape(1, D), mods, n_rows=n_live, shift_idx=6, scale_idx=7, tr=tr,
                    name="ffn2_adaln", **common)
        act, (w2d,) = _matmul(hn, w2gu, swiglu=True, out_dtype=BF16, tm=tm, name="ffn2_up",
                              casts=[(w_ffn2_down, l, whole(w_ffn2_down), tn_ff)])
        next_casts = [(w_ffn1_gu, l + 1, whole(w_ffn1_gu), tn)] if l + 1 < L else []
        h, nxt = _matmul_residual(act, w2d, h, mods, n_rows=n_live, gate_idx=8, scale=0.5, tm=tm,
                                  name="ffn2_down", **common, casts=next_casts)
        if nxt:
            w1gu = nxt[0]

    out = _final_rmsnorm(h, norm_final.reshape(1, D), bs_rows, tr)
    return out.reshape(B, S, D)
```

```python
import functools
import math

import numpy as np
import jax
import jax.numpy as jnp
from jax import lax
from jax.experimental import pallas as pl
from jax.experimental.pallas import tpu as pltpu

EPS = 1e-6
N_MOD = 9
GRID_W = 64
HEAD_DIM = 128
GDN_CONV = 5
SC_CONV = 3
POOL_WINDOWS = (2, 4, 8, 16)
CHUNK = 128
INV_BLOCK = 16
LANES = 128
SUBLANES = 8
BF16_SUBLANES = 16
VMEM_LIMIT = 56 * 1024 * 1024

BF16 = jnp.bfloat16
F32 = jnp.float32


def _cparams(*sem):
    return pltpu.CompilerParams(dimension_semantics=sem, vmem_limit_bytes=VMEM_LIMIT)


def _dot(a, b):
    return jnp.dot(a, b, preferred_element_type=F32)


def _dot_hi(a, b):
    return jnp.dot(a, b, preferred_element_type=F32, precision=lax.Precision.HIGHEST)


def _silu(v):
    return v * jax.nn.sigmoid(v)


def _mod_kernel(c_ref, w_ref, b_ref, o_ref):
    a = _silu(c_ref[...]).astype(BF16)
    o_ref[...] = _dot(a, w_ref[...].astype(BF16)) + b_ref[...]


def _modulation(c_all, w_mod, b_mod, tn=512):
    L, D, N = w_mod.shape
    return pl.pallas_call(
        _mod_kernel,
        grid=(L, N // tn),
        in_specs=[pl.BlockSpec((SUBLANES, D), lambda l, j: (0, 0)),
                  pl.BlockSpec((None, D, tn), lambda l, j: (l, 0, j)),
                  pl.BlockSpec((None, 1, tn), lambda l, j: (l, 0, j))],
        out_specs=pl.BlockSpec((None, SUBLANES, tn), lambda l, j: (l, 0, j)),
        out_shape=jax.ShapeDtypeStruct((L, SUBLANES, N), F32),
        compiler_params=_cparams("parallel", "parallel"),
        name="modulation",
    )(c_all, w_mod, b_mod)


def _adaln_kernel(h_ref, nw_ref, mod_ref, o_ref, *, shift_idx, scale_idx):
    h = h_ref[...]
    ms = jnp.mean(h * h, axis=-1, keepdims=True)
    y = h * lax.rsqrt(ms + EPS) * nw_ref[...]
    shift = mod_ref[0, shift_idx:shift_idx + 1, :]
    scale = mod_ref[0, scale_idx:scale_idx + 1, :]
    o_ref[...] = (y * (1.0 + scale) + shift).astype(o_ref.dtype)


def _adaln(h, norm_w, mods, *, n_rows, seq_rows, n_batch, shift_idx, scale_idx, tr, name):
    D = h.shape[1]
    mod_map = lambda i: (jnp.minimum((i * tr) // seq_rows, n_batch), 0, 0)
    return pl.pallas_call(
        functools.partial(_adaln_kernel, shift_idx=shift_idx, scale_idx=scale_idx),
        grid=(n_rows // tr,),
        in_specs=[pl.BlockSpec((tr, D), lambda i: (i, 0)),
                  pl.BlockSpec((1, D), lambda i: (0, 0)),
                  pl.BlockSpec((1, N_MOD, D), mod_map)],
        out_specs=pl.BlockSpec((tr, D), lambda i: (i, 0)),
        out_shape=jax.ShapeDtypeStruct((n_rows, D), BF16),
        compiler_params=_cparams("parallel"),
        name=name,
    )(h, norm_w, mods)


def _cast_row_block(rows, steps):
    for rps in range(BF16_SUBLANES, rows + 1, BF16_SUBLANES):
        if rows % rps == 0 and rows // rps <= steps:
            return rps
    raise ValueError(f"cannot spread {rows} rows over {steps} steps")


def _cast_plan(casts, n_i, nj):
    in_specs, out_specs, out_shapes, args, plans = [], [], [], [], []
    for src, layer, ranges, tile in casts:
        _, rows, cols = src.shape
        rps = _cast_row_block(rows, n_i * nj)
        last = rows // rps - 1
        out_cols = sum(b - a for a, b in ranges)
        src_map = lambda i, j, last=last, layer=layer: (layer, jnp.minimum(i * nj + j, last), 0)
        in_specs.append(pl.BlockSpec((None, rps, cols), src_map))
        if tile is None:
            out_specs.append(pl.BlockSpec((rps, out_cols), lambda i, j, last=last: (jnp.minimum(i * nj + j, last), 0)))
            out_shapes.append(jax.ShapeDtypeStruct((rows, out_cols), BF16))
            pieces = tuple((None, a, b) for a, b in ranges)
        else:
            assert all((b - a) % tile == 0 for a, b in ranges)
            nt = out_cols // tile
            out_specs.append(pl.BlockSpec((nt, rps, tile),
                                          lambda i, j, last=last: (0, jnp.minimum(i * nj + j, last), 0)))
            out_shapes.append(jax.ShapeDtypeStruct((nt, rows, tile), BF16))
            pieces = tuple((None, s, s + tile) for a, b in ranges for s in range(a, b, tile))
            pieces = tuple((t, a, b) for t, (_, a, b) in enumerate(pieces))
        args.append(src)
        plans.append(pieces)
    return in_specs, out_specs, out_shapes, args, plans


def _run_casts(src_refs, dst_refs, plans):
    for src_ref, dst_ref, pieces in zip(src_refs, dst_refs, plans):
        off = 0
        for t, a, b in pieces:
            if t is None:
                dst_ref[:, off:off + b - a] = src_ref[:, a:b].astype(dst_ref.dtype)
                off += b - a
            else:
                dst_ref[t] = src_ref[:, a:b].astype(dst_ref.dtype)


def _mm_kernel(a_ref, *rest, swiglu, cast_ranges):
    nc = len(cast_ranges)
    n_w = 2 if swiglu else 1
    w_refs, src_refs = rest[:n_w], rest[n_w:n_w + nc]
    o_ref, dst_refs = rest[n_w + nc], rest[n_w + nc + 1:]
    a = a_ref[...]
    if swiglu:
        g = _dot(a, w_refs[0][...])
        u = _dot(a, w_refs[1][...])
        o_ref[...] = (_silu(g) * u).astype(o_ref.dtype)
    else:
        o_ref[...] = _dot(a, w_refs[0][...]).astype(o_ref.dtype)
    _run_casts(src_refs, dst_refs, cast_ranges)


def _weight_specs(w, tn, swiglu):
    if w.ndim == 2:
        k, nt = w.shape[0], w.shape[1] // tn
        nj = nt // 2 if swiglu else nt
        specs = [pl.BlockSpec((k, tn), lambda i, j: (0, j))]
        if swiglu:
            specs.append(pl.BlockSpec((k, tn), lambda i, j: (0, j + nj)))
    else:
        nt, k, tn = w.shape
        nj = nt // 2 if swiglu else nt
        specs = [pl.BlockSpec((None, k, tn), lambda i, j: (j, 0, 0))]
        if swiglu:
            specs.append(pl.BlockSpec((None, k, tn), lambda i, j: (j + nj, 0, 0)))
    return specs, nj, tn


def _matmul(a, w, *, swiglu, out_dtype, tm, name, tn=None, casts=()):
    R, K = a.shape
    w_specs, nj, tn = _weight_specs(w, tn, swiglu)
    n_out = nj * tn
    in_specs = [pl.BlockSpec((tm, K), lambda i, j: (i, 0))] + w_specs
    args = [a] + [w] * len(w_specs)
    c_in, c_out, c_shapes, c_args, c_ranges = _cast_plan(casts, R // tm, nj)
    res = pl.pallas_call(
        functools.partial(_mm_kernel, swiglu=swiglu, cast_ranges=c_ranges),
        grid=(R // tm, nj),
        in_specs=in_specs + c_in,
        out_specs=[pl.BlockSpec((tm, tn), lambda i, j: (i, j))] + c_out,
        out_shape=[jax.ShapeDtypeStruct((R, n_out), out_dtype)] + c_shapes,
        compiler_params=_cparams("arbitrary", "arbitrary"),
        name=name,
    )(*args, *c_args)
    return res[0], res[1:]


def _mm_res_kernel(a_ref, w_ref, res_ref, mod_ref, *rest, gate_idx, scale, cast_ranges):
    nc = len(cast_ranges)
    src_refs, o_ref, dst_refs = rest[:nc], rest[nc], rest[nc + 1:]
    acc = _dot(a_ref[...], w_ref[...])
    gate = mod_ref[0, gate_idx:gate_idx + 1, :]
    o_ref[...] = res_ref[...] + (scale * gate) * acc
    _run_casts(src_refs, dst_refs, cast_ranges)


def _matmul_residual(a, w, res, mods, *, n_rows, seq_rows, n_batch, gate_idx, scale, tm, name, casts=()):
    K = a.shape[1]
    nt, _, tn = w.shape
    D = nt * tn
    mod_map = lambda i, j: (jnp.minimum((i * tm) // seq_rows, n_batch), 0, j)
    c_in, c_out, c_shapes, c_args, c_ranges = _cast_plan(casts, n_rows // tm, D // tn)
    out = pl.pallas_call(
        functools.partial(_mm_res_kernel, gate_idx=gate_idx, scale=scale, cast_ranges=c_ranges),
        grid=(n_rows // tm, D // tn),
        in_specs=[pl.BlockSpec((tm, K), lambda i, j: (i, 0)),
                  pl.BlockSpec((None, K, tn), lambda i, j: (j, 0, 0)),
                  pl.BlockSpec((tm, tn), lambda i, j: (i, j)),
                  pl.BlockSpec((1, N_MOD, tn), mod_map)] + c_in,
        out_specs=[pl.BlockSpec((tm, tn), lambda i, j: (i, j))] + c_out,
        out_shape=[jax.ShapeDtypeStruct((n_rows, D), F32)] + c_shapes,
        compiler_params=_cparams("arbitrary", "arbitrary"),
        name=name,
    )(a, w, res, mods, *c_args)
    return out[0], out[1:]


def _rmsnorm_kernel(h_ref, w_ref, o_ref):
    h = h_ref[...]
    ms = jnp.mean(h * h, axis=-1, keepdims=True)
    o_ref[...] = h * lax.rsqrt(ms + EPS) * w_ref[...]


def _final_rmsnorm(h, w, rows, tm):
    D = h.shape[1]
    return pl.pallas_call(
        _rmsnorm_kernel,
        grid=(rows // tm,),
        in_specs=[pl.BlockSpec((tm, D), lambda i: (i, 0)),
                  pl.BlockSpec((1, D), lambda i: (0, 0))],
        out_specs=pl.BlockSpec((tm, D), lambda i: (i, 0)),
        out_shape=jax.ShapeDtypeStruct((rows, D), F32),
        compiler_params=_cparams("parallel"),
        name="final_rmsnorm",
    )(h, w)


def _segment_position(tile_rows, bs_rows, seq, ctx_len):
    row0 = pl.program_id(0) * tile_rows
    in_x = row0 < bs_rows
    off = jnp.where(in_x, lax.rem(row0, seq), lax.rem(row0 - bs_rows, ctx_len))
    seglen = jnp.where(in_x, seq, ctx_len)
    return in_x, off, seglen


def _qkv_conv_kernel(cur_ref, prev_ref, next_ref, w_ref, *rest, kind, tr, bs_rows, seq, ctx_len):
    if kind == 'k':
        o_ref, ot_ref, ext_ref = rest
    else:
        o_ref, ext_ref = rest
    _, off, seglen = _segment_position(tr, bs_rows, seq, ctx_len)
    is_start = off == 0
    is_end = off + tr == seglen
    pad = SUBLANES
    half = GDN_CONV // 2
    ext_ref[0:pad, :] = jnp.where(is_start, 0.0, prev_ref[...])
    ext_ref[pad:pad + tr, :] = cur_ref[...]
    ext_ref[pad + tr:pad + tr + pad, :] = jnp.where(is_end, 0.0, next_ref[...])
    ext = ext_ref[...]
    n_ext = tr + 2 * pad
    acc = None
    for j in range(GDN_CONV):
        shifted = ext if j == half else pltpu.roll(ext, (half - j) % n_ext, axis=0)
        term = shifted[pad:pad + tr, :] * w_ref[j:j + 1, :]
        acc = term if acc is None else acc + term
    y = _silu(acc)
    tc = y.shape[1]
    if kind == 'v':
        o_ref[...] = y.astype(o_ref.dtype)
        return
    mult = HEAD_DIM ** -0.5 if kind == 'q' else 1.0
    for hd in range(tc // HEAD_DIM):
        yh = y[:, hd * HEAD_DIM:(hd + 1) * HEAD_DIM]
        inv = lax.rsqrt(jnp.sum(yh * yh, axis=-1, keepdims=True) + EPS)
        yn = yh * inv
        if kind == 'q':
            yn = yn * mult
        o_ref[:, hd * HEAD_DIM:(hd + 1) * HEAD_DIM] = yn.astype(o_ref.dtype)
        if kind == 'k':
            ot_ref[hd * HEAD_DIM:(hd + 1) * HEAD_DIM, :] = yn.T.astype(ot_ref.dtype)


def _qkv_conv(px, conv_w, *, kind, col0, width, tr, tc, bs_rows, seq, ctx_len):
    R = px.shape[0]
    cb0 = col0 // tc
    rb = tr // SUBLANES
    last = R // SUBLANES - 1
    in_specs = [pl.BlockSpec((tr, tc), lambda i, j: (i, cb0 + j)),
                pl.BlockSpec((SUBLANES, tc), lambda i, j: (jnp.maximum(i * rb - 1, 0), cb0 + j)),
                pl.BlockSpec((SUBLANES, tc), lambda i, j: (jnp.minimum((i + 1) * rb, last), cb0 + j)),
                pl.BlockSpec((GDN_CONV, tc), lambda i, j: (0, j))]
    out_specs = [pl.BlockSpec((tr, tc), lambda i, j: (i, j))]
    out_shape = [jax.ShapeDtypeStruct((R, width), BF16)]
    if kind == 'k':
        out_specs.append(pl.BlockSpec((tc, tr), lambda i, j: (j, i)))
        out_shape.append(jax.ShapeDtypeStruct((width, R), BF16))
    res = pl.pallas_call(
        functools.partial(_qkv_conv_kernel, kind=kind, tr=tr, bs_rows=bs_rows, seq=seq, ctx_len=ctx_len),
        grid=(R // tr, width // tc),
        in_specs=in_specs,
        out_specs=out_specs,
        out_shape=out_shape,
        scratch_shapes=[pltpu.VMEM((tr + 2 * SUBLANES, tc), F32)],
        compiler_params=_cparams("parallel", "parallel"),
        name="gdn_conv_" + kind,
    )(px, px, px, conv_w)
    return res


def _gate_kernel(ba_ref, alog_ref, dtb_ref, cols_ref, rows_ref, *, tm, n_pairs):
    lane = lax.broadcasted_iota(jnp.int32, (CHUNK, LANES), 1)
    is_beta = (lane & 1) == 0
    is_bwd = (lane & 4) != 0
    ri = lax.broadcasted_iota(jnp.int32, (CHUNK, CHUNK), 0)
    ci = lax.broadcasted_iota(jnp.int32, (CHUNK, CHUNK), 1)
    tri_lo = jnp.where(ci <= ri, 1.0, 0.0)
    tri_up = jnp.where(ci >= ri, 1.0, 0.0)
    neg_a = -jnp.exp(alog_ref[...])
    for c in range(tm // CHUNK):
        sl = slice(c * CHUNK, (c + 1) * CHUNK)
        x = ba_ref[sl, :]
        beta = jax.nn.sigmoid(x)
        y = x + dtb_ref[...]
        softplus = jnp.maximum(y, 0.0) + jnp.log1p(jnp.exp(-jnp.abs(y)))
        g = jnp.where(is_beta, 0.0, neg_a * softplus)
        gc = jnp.where(is_bwd, _dot_hi(tri_up, g), _dot_hi(tri_lo, g))
        full = jnp.where(is_beta, beta, gc)
        for p in range(n_pairs):
            cols_ref[p, sl, :] = full[:, p * 8:(p + 1) * 8]
        rows_ref[:, sl] = full.T[0:n_pairs * 8, :]


def _gdn_gates(ba, alog_row, dtb_row, *, n_pairs, tm):
    R = ba.shape[0]
    return pl.pallas_call(
        functools.partial(_gate_kernel, tm=tm, n_pairs=n_pairs),
        grid=(R // tm,),
        in_specs=[pl.BlockSpec((tm, LANES), lambda i: (i, 0)),
                  pl.BlockSpec((1, LANES), lambda i: (0, 0)),
                  pl.BlockSpec((1, LANES), lambda i: (0, 0))],
        out_specs=[pl.BlockSpec((n_pairs, tm, 8), lambda i: (0, i, 0)),
                   pl.BlockSpec((n_pairs * 8, tm), lambda i: (0, i))],
        out_shape=[jax.ShapeDtypeStruct((n_pairs, R, 8), F32),
                   jax.ShapeDtypeStruct((n_pairs * 8, R), F32)],
        compiler_params=_cparams("parallel"),
        name="gdn_gates",
    )(ba, alog_row, dtb_row)


def _gdn_kernel(qx_ref, qc_ref, kx_ref, kc_ref, ktx_ref, ktc_ref, vx_ref, vc_ref,
                colx_ref, colc_ref, rowx_ref, rowc_ref, zx_ref, zc_ref, nw_ref, ox_ref, oc_ref,
                t_ref, a_ref, of_ref, ob_ref, sf_ref, sb_ref, *, seq, ctx_len, rb, n_pair, prep_chunks):
    C = CHUNK
    HD = HEAD_DIM
    n_xc = seq // C
    n_cc = ctx_len // C
    x_refs = (qx_ref, kx_ref, ktx_ref, vx_ref, colx_ref, rowx_ref)
    c_refs = (qc_ref, kc_ref, ktc_ref, vc_ref, colc_ref, rowc_ref)

    ri = lax.broadcasted_iota(jnp.int32, (C, C), 0)
    ci = lax.broadcasted_iota(jnp.int32, (C, C), 1)
    eye = (ri == ci).astype(F32)

    def lane_idx(d, s, kind):
        return d * 4 + s * 2 + kind

    block_diff = ri ^ ci

    def unit_triangular_inverses(nmats):
        base = int(math.log2(INV_BLOCK))
        n0s = [jnp.where((block_diff >> base) == 0, n, 0.0) for n in nmats]
        invs = [eye + n0 for n0 in n0s]
        pws = n0s
        for _ in range(base - 1):
            pwbs = [pw.astype(BF16) for pw in pws]
            pws = [_dot(b, b) for b in pwbs]
            invs = [inv + _dot(inv.astype(BF16), pw.astype(BF16)) for inv, pw in zip(invs, pws)]
        for level in range(base, int(math.log2(C))):
            offs = [jnp.where((block_diff >> level) == 1, n, 0.0).astype(BF16) for n in nmats]
            invbs = [inv.astype(BF16) for inv in invs]
            halves = [_dot(ib, off).astype(BF16) for ib, off in zip(invbs, offs)]
            invs = [inv + _dot(h, ib) for inv, h, ib in zip(invs, halves, invbs)]
        return invs

    def prepare(chunks):
        nmats, dests = [], []
        for refs, row0, slot in chunks:
            q_ref, k_ref, kt_ref, _, col_ref, row_ref = refs
            rows = pl.ds(row0, C)
            for u in range(n_pair):
                hs = slice(u * HD, (u + 1) * HD)
                qk = jnp.concatenate([q_ref[rows, hs], k_ref[rows, hs]], axis=0)
                gram = _dot(qk, kt_ref[hs, rows])
                qkt = gram[0:C, :]
                kkt = gram[C:2 * C, :]
                cols = col_ref[u, rows, :]
                for d in range(2):
                    incl = (ci <= ri) if d == 0 else (ci >= ri)
                    strict = (ci < ri) if d == 0 else (ci > ri)
                    for s in range(2):
                        kb, kg = lane_idx(d, s, 0), lane_idx(d, s, 1)
                        beta_col = cols[:, kb:kb + 1]
                        gc_col = cols[:, kg:kg + 1]
                        gc_row = row_ref[u * 8 + kg:u * 8 + kg + 1, rows]
                        decay = jnp.where(incl, jnp.exp(jnp.where(incl, gc_col - gc_row, 0.0)), 0.0)
                        nmats.append(jnp.where(strict, -(kkt * beta_col) * decay, 0.0))
                        a_ref[slot, u * 4 + d * 2 + s] = (qkt * decay).astype(BF16)
                        dests.append((slot, u * 4 + d * 2 + s))
        for (slot, idx), inv in zip(dests, unit_triangular_inverses(nmats)):
            t_ref[slot, idx] = inv.astype(BF16)

    def chain(tasks):
        units = [(d, u, refs, pl.ds(row0, C), slot, orow0)
                 for d, refs, row0, slot, orow0 in tasks for u in range(n_pair)]
        states, projs = [], []
        for d, u, refs, rows, _, _ in units:
            hs = slice(u * HD, (u + 1) * HD)
            state = (sf_ref if d == 0 else sb_ref)[u]
            qk = jnp.concatenate([refs[0][rows, hs], refs[1][rows, hs]], axis=0)
            states.append(state)
            projs.append(_dot(qk, state.astype(BF16)))
        scalars, rhss = [], []
        for (d, u, refs, rows, _, _), proj in zip(units, projs):
            cols = refs[4][u, rows, :]
            for s in range(2):
                kb, kg = lane_idx(d, s, 0), lane_idx(d, s, 1)
                beta_col = cols[:, kb:kb + 1]
                gc_col = cols[:, kg:kg + 1]
                g_last = gc_col[C - 1:C, :] if d == 0 else gc_col[0:1, :]
                e_col = jnp.exp(gc_col)
                vs = slice((2 * u + s) * HD, (2 * u + s + 1) * HD)
                v = refs[3][rows, vs].astype(F32)
                k_s = proj[C:2 * C, s * HD:(s + 1) * HD]
                rhss.append((beta_col * (v - e_col * k_s)).astype(BF16))
                scalars.append((gc_col, e_col, g_last))
        v_news = []
        for i, (d, u, _, _, slot, _) in enumerate(units):
            for s in range(2):
                v_news.append(_dot(t_ref[slot, u * 4 + d * 2 + s], rhss[2 * i + s]))
        for i, (d, u, refs, rows, slot, orow0) in enumerate(units):
            o_acc = of_ref if d == 0 else ob_ref
            evs, gls = [], []
            for s in range(2):
                gc_col, e_col, g_last = scalars[2 * i + s]
                v_new = v_news[2 * i + s]
                q_s = projs[i][0:C, s * HD:(s + 1) * HD]
                out = e_col * q_s + _dot(a_ref[slot, u * 4 + d * 2 + s], v_new.astype(BF16))
                o_acc[pl.ds(orow0, C), (2 * u + s) * HD:(2 * u + s + 1) * HD] = out
                evs.append(jnp.exp(g_last - gc_col) * v_new)
                gls.append(jnp.broadcast_to(jnp.exp(g_last), (HD, HD)))
            ev = jnp.concatenate(evs, axis=1).astype(BF16)
            new_state = states[i] * jnp.concatenate(gls, axis=1) + _dot(refs[2][u * HD:(u + 1) * HD, rows], ev)
            (sf_ref if d == 0 else sb_ref)[u] = new_state

    sf_ref[...] = jnp.zeros_like(sf_ref)
    sb_ref[...] = jnp.zeros_like(sb_ref)
    prepare([(c_refs, n * C, n) for n in range(n_cc)])

    def prep_body(it, carry):
        n0 = it * prep_chunks
        prepare([(x_refs, pl.multiple_of((n0 + j) * C, C), n_cc + n0 + j) for j in range(prep_chunks)])
        return carry
    lax.fori_loop(0, n_xc // prep_chunks, prep_body, 0)

    for n in range(n_cc):
        m = n_cc - 1 - n
        chain([(0, c_refs, n * C, n, seq + n * C), (1, c_refs, m * C, m, seq + m * C)])

    def chain_body(n, carry):
        m = n_xc - 1 - n
        rf = pl.multiple_of(n * C, C)
        rbk = pl.multiple_of(m * C, C)
        chain([(0, x_refs, rf, n_cc + n, rf), (1, x_refs, rbk, n_cc + m, rbk)])
        return carry
    lax.fori_loop(0, n_xc, chain_body, 0)

    def finish(z_ref, out_ref, base, n_blocks):
        def body(bi, carry):
            r0 = pl.multiple_of(bi * rb, rb)
            rows = pl.ds(r0, rb)
            acc_rows = pl.ds(pl.multiple_of(base + r0, rb), rb)
            o = of_ref[acc_rows, :] + ob_ref[acc_rows, :]
            z = z_ref[rows, :]
            for hd in range(2 * n_pair):
                hs = slice(hd * HD, (hd + 1) * HD)
                oh = o[:, hs]
                ms = jnp.mean(oh * oh, axis=-1, keepdims=True)
                y = oh * lax.rsqrt(ms + EPS) * nw_ref[...]
                out_ref[rows, hs] = (y * _silu(z[:, hs])).astype(out_ref.dtype)
            return carry
        lax.fori_loop(0, n_blocks, body, 0)

    finish(zx_ref, ox_ref, 0, seq // rb)
    finish(zc_ref, oc_ref, seq, ctx_len // rb)


def _gdn(qn, kn, knt, vn, cols, rows, px, z_col0, norm_w, *, n_batch, seq, ctx_len, out_cols):
    R = qn.shape[0]
    n_pairs = qn.shape[1] // HEAD_DIM
    C = CHUNK
    rb = min(256, ctx_len)
    cblk0 = n_batch * (seq // ctx_len)
    n_slots = (seq + ctx_len) // C
    n_pair = 2 if n_pairs % 2 == 0 else 1
    prep_chunks = 2 if (seq // C) % 2 == 0 else 1
    qw = n_pair * HEAD_DIM
    vw = 2 * qw
    zb0 = z_col0 // vw
    assert z_col0 % vw == 0

    x_rows = lambda w, c0=0: pl.BlockSpec((seq, w), lambda b, p: (b, c0 + p))
    c_rows = lambda w, c0=0: pl.BlockSpec((ctx_len, w), lambda b, p: (cblk0 + b, c0 + p))
    in_specs = [
        x_rows(qw), c_rows(qw),
        x_rows(qw), c_rows(qw),
        pl.BlockSpec((qw, seq), lambda b, p: (p, b)),
        pl.BlockSpec((qw, ctx_len), lambda b, p: (p, cblk0 + b)),
        x_rows(vw), c_rows(vw),
        pl.BlockSpec((n_pair, seq, 8), lambda b, p: (p, b, 0)),
        pl.BlockSpec((n_pair, ctx_len, 8), lambda b, p: (p, cblk0 + b, 0)),
        pl.BlockSpec((n_pair * 8, seq), lambda b, p: (p, b)),
        pl.BlockSpec((n_pair * 8, ctx_len), lambda b, p: (p, cblk0 + b)),
        x_rows(vw, zb0), c_rows(vw, zb0),
        pl.BlockSpec((1, HEAD_DIM), lambda b, p: (0, 0)),
    ]
    return pl.pallas_call(
        functools.partial(_gdn_kernel, seq=seq, ctx_len=ctx_len, rb=rb, n_pair=n_pair, prep_chunks=prep_chunks),
        grid=(n_batch, n_pairs // n_pair),
        in_specs=in_specs,
        out_specs=[pl.BlockSpec((seq, vw), lambda b, p: (b, p)),
                   pl.BlockSpec((ctx_len, vw), lambda b, p: (b, p))],
        out_shape=[jax.ShapeDtypeStruct((R, out_cols), BF16),
                   jax.ShapeDtypeStruct((n_batch * ctx_len, vn.shape[1]), BF16)],
        scratch_shapes=[pltpu.VMEM((n_slots, n_pair * 4, C, C), BF16),
                        pltpu.VMEM((n_slots, n_pair * 4, C, C), BF16),
                        pltpu.VMEM((seq + ctx_len, vw), F32),
                        pltpu.VMEM((seq + ctx_len, vw), F32),
                        pltpu.VMEM((n_pair, HEAD_DIM, 2 * HEAD_DIM), F32),
                        pltpu.VMEM((n_pair, HEAD_DIM, 2 * HEAD_DIM), F32)],
        compiler_params=_cparams("parallel", "parallel"),
        name="gdn_delta_rule",
    )(qn, qn, kn, kn, knt, knt, vn, vn, cols, cols, rows, rows, px, px, norm_w)


def _copy_kernel(src_ref, dst_hbm_ref, o_ref):
    del dst_hbm_ref
    o_ref[...] = src_ref[...]


def _place_context_rows(y_ctx, ymix, *, row_blk0, blk_rows, blk_cols):
    rows, cols = y_ctx.shape
    return pl.pallas_call(
        _copy_kernel,
        grid=(rows // blk_rows, cols // blk_cols),
        in_specs=[pl.BlockSpec((blk_rows, blk_cols), lambda i, j: (i, j)),
                  pl.BlockSpec(memory_space=pl.ANY)],
        out_specs=pl.BlockSpec((blk_rows, blk_cols), lambda i, j: (row_blk0 + i, j)),
        out_shape=jax.ShapeDtypeStruct(ymix.shape, ymix.dtype),
        input_output_aliases={1: 0},
        compiler_params=_cparams("parallel", "parallel"),
        name="place_context_rows",
    )(y_ctx, ymix)


def _shortconv_kernel(x_ref, b_ref, c_ref, w_ref, ymix_ref, o_ref, *, tr, bs_rows, seq, ctx_len):
    del ymix_ref
    in_x, off, _ = _segment_position(tr, bs_rows, seq, ctx_len)
    period = jnp.where(in_x, GRID_W, ctx_len)
    pos = (off + lax.broadcasted_iota(jnp.int32, (tr, 1), 0)) & (period - 1)
    v = c_ref[...] * x_ref[...]
    left = jnp.where(pos == 0, 0.0, pltpu.roll(v, 1, axis=0))
    right = jnp.where(pos == period - 1, 0.0, pltpu.roll(v, tr - 1, axis=0))
    y = left * w_ref[0:1, :] + v * w_ref[1:2, :] + right * w_ref[2:3, :]
    o_ref[...] = (b_ref[...] * y).astype(o_ref.dtype)


def _shortconv(px, conv_w, ymix, *, x_col0, width, out_col0, tr, tc, bs_rows, seq, ctx_len):
    R = px.shape[0]
    nb = width // tc
    xb0 = x_col0 // tc
    ob0 = out_col0 // tc
    return pl.pallas_call(
        functools.partial(_shortconv_kernel, tr=tr, bs_rows=bs_rows, seq=seq, ctx_len=ctx_len),
        grid=(R // tr, nb),
        in_specs=[pl.BlockSpec((tr, tc), lambda i, j: (i, xb0 + j)),
                  pl.BlockSpec((tr, tc), lambda i, j: (i, xb0 + nb + j)),
                  pl.BlockSpec((tr, tc), lambda i, j: (i, xb0 + 2 * nb + j)),
                  pl.BlockSpec((SC_CONV, tc), lambda i, j: (0, j)),
                  pl.BlockSpec(memory_space=pl.ANY)],
        out_specs=pl.BlockSpec((tr, tc), lambda i, j: (i, ob0 + j)),
        out_shape=jax.ShapeDtypeStruct(ymix.shape, ymix.dtype),
        input_output_aliases={4: 0},
        compiler_params=_cparams("parallel", "parallel"),
        name="short_conv",
    )(px, px, px, conv_w, ymix)


def _pool_kernel(u_ref, w_ref, sc_ref, ymix_ref, o_ref, pa_ref, pb_ref, *, n_tok, unit, pg):
    del ymix_ref
    n_pos = n_tok // unit
    reach = 8 * unit
    pad = -(-reach // SUBLANES) * SUBLANES
    mg = -(-4 * unit // SUBLANES) * SUBLANES
    tp = n_tok + 2 * pad
    zeros_mg = jnp.zeros((mg, pg), F32)
    zeros_pad = jnp.zeros((pad, pg), F32)
    for buf in (pa_ref, pb_ref):
        buf[0:mg, :] = zeros_mg
        buf[mg + tp:mg + tp + mg, :] = zeros_mg
    row = lax.broadcasted_iota(jnp.int32, (n_tok, 1), 0) // unit
    for g, win in enumerate(POOL_WINDOWS):
        cs = slice(g * pg, (g + 1) * pg)
        u = u_ref[:, cs]
        pa_ref[mg:mg + pad, :] = zeros_pad
        pa_ref[mg + pad:mg + pad + n_tok, :] = u
        pa_ref[mg + pad + n_tok:mg + tp, :] = zeros_pad
        src, dst = pa_ref, pb_ref
        dst[mg:mg + tp, :] = src[mg - unit:mg - unit + tp, :] + src[mg:mg + tp, :]
        src, dst = dst, src
        step, w = 1, 2
        while w < win:
            dst[mg:mg + tp, :] = (src[mg - step * unit:mg - step * unit + tp, :]
                                  + src[mg + step * unit:mg + step * unit + tp, :])
            src, dst = dst, src
            step, w = step * 2, w * 2
        total = src[mg + pad:mg + pad + n_tok, :]
        lo = jnp.maximum(row - win // 2, 0)
        hi = jnp.minimum(row - win // 2 + win, n_pos)
        count = (hi - lo).astype(F32)
        dlt = total / count - u
        y = _dot(dlt.astype(BF16), w_ref[g]) * sc_ref[:, cs]
        o_ref[:, cs] = y.astype(o_ref.dtype)


def _pool(px, pool_w, pool_scale, ymix, *, u_col0, out_col0, n_tok, unit, blk0, n_batch, name):
    width = pool_scale.shape[1]
    pg = width // len(POOL_WINDOWS)
    reach = 8 * unit
    pad = -(-reach // SUBLANES) * SUBLANES
    mg = -(-4 * unit // SUBLANES) * SUBLANES
    buf_rows = n_tok + 2 * pad + 2 * mg
    ub = u_col0 // width
    ob = out_col0 // width
    return pl.pallas_call(
        functools.partial(_pool_kernel, n_tok=n_tok, unit=unit, pg=pg),
        grid=(n_batch,),
        in_specs=[pl.BlockSpec((n_tok, width), lambda b: (blk0 + b, ub)),
                  pl.BlockSpec(pool_w.shape, lambda b: (0, 0, 0)),
                  pl.BlockSpec((1, width), lambda b: (0, 0)),
                  pl.BlockSpec(memory_space=pl.ANY)],
        out_specs=pl.BlockSpec((n_tok, width), lambda b: (blk0 + b, ob)),
        out_shape=jax.ShapeDtypeStruct(ymix.shape, ymix.dtype),
        input_output_aliases={3: 0},
        scratch_shapes=[pltpu.VMEM((buf_rows, pg), F32), pltpu.VMEM((buf_rows, pg), F32)],
        compiler_params=_cparams("parallel"),
        name=name,
    )(px, pool_w, pool_scale, ymix)


def _ba_permutation(n_heads):
    perm = np.zeros(4 * n_heads, np.int32)
    for p in range(n_heads // 2):
        for d in range(2):
            for s in range(2):
                for kind in range(2):
                    perm[p * 8 + d * 4 + s * 2 + kind] = d * 2 * n_heads + kind * n_heads + 2 * p + s
    return perm


def _gate_param_row(param, n_heads):
    row = jnp.zeros((LANES,), F32)
    idx, src_d, src_h = [], [], []
    for p in range(n_heads // 2):
        for d in range(2):
            for s in range(2):
                idx.append(p * 8 + d * 4 + s * 2 + 1)
                src_d.append(d)
                src_h.append(2 * p + s)
    row = row.at[np.array(idx)].set(param[np.array(src_d), np.array(src_h)])
    return row.reshape(1, LANES)


def _pick_tile(n, prefs):
    for t in prefs:
        if n % t == 0:
            return t
    raise ValueError(f"no tile for {n}")


def kernel(x, c, ctx, c_ctx, w_mod, b_mod, norm_ffn1, w_ffn1_gu, w_ffn1_down, norm_mix, w_in, conv_qkv, a_log,
           dt_bias, gdn_norm, conv_short, pool_w, pool_scale, w_out, norm_ffn2, w_ffn2_gu, w_ffn2_down, norm_final):
    B, S, D = x.shape
    CL = ctx.shape[1]
    L = w_mod.shape[0]
    n_vheads = a_log.shape[2]
    qk_dim = (n_vheads // 2) * HEAD_DIM
    v_dim = n_vheads * HEAD_DIM
    sc_dim = conv_short.shape[2]
    pool_dim = pool_scale.shape[1]
    n_ba = 4 * n_vheads
    n_pairs = n_vheads // 2
    bs_rows = B * S
    R = bs_rows + B * CL
    assert S % CHUNK == 0 and CL % CHUNK == 0 and S % CL == 0 and n_ba <= LANES

    tm = _pick_tile(math.gcd(S, R), (1024, 512, 256, 128))
    tg = _pick_tile(math.gcd(S, R), (512, 256, 128))
    tr = min(256, CL)
    tn = 512 if D % 512 == 0 else 256
    tn_ff = 256
    tc_qk = _pick_tile(qk_dim, (1024, 512, 256, 128))
    tc_v = math.gcd(v_dim, 2 * tc_qk)

    z_col0 = 2 * qk_dim + v_dim
    ba_col0 = z_col0 + v_dim
    sc_col0_src = ba_col0 + n_ba
    sc_col0 = ba_col0
    pool_col0 = sc_col0 + 3 * sc_dim
    ba_perm = _ba_permutation(n_vheads)
    onehot = np.zeros((n_ba, LANES), np.float32)
    onehot[ba_perm, np.arange(n_ba)] = 1.0
    ba_onehot = jnp.asarray(onehot, dtype=BF16)

    h = jnp.concatenate([x.reshape(bs_rows, D), ctx.reshape(B * CL, D)], axis=0)
    c_all = jnp.zeros((SUBLANES, D), F32).at[0:B].set(c).at[B].set(c_ctx)
    mods_all = _modulation(c_all, w_mod, b_mod.reshape(L, 1, N_MOD * D)).reshape(L, SUBLANES, N_MOD, D)

    common = dict(seq_rows=S, n_batch=B)
    seg = dict(bs_rows=bs_rows, seq=S, ctx_len=CL)
    whole = lambda w: [(0, w.shape[2])]
    w1gu = w_ffn1_gu[0].astype(BF16)
    for l in range(L):
        n_live = bs_rows if l == L - 1 else R
        mods = mods_all[l]

        hn = _adaln(h, norm_ffn1[l].reshape(1, D), mods, n_rows=R, shift_idx=0, scale_idx=1, tr=tg,
                    name="ffn1_adaln", **common)
        act, (w1d,) = _matmul(hn, w1gu, swiglu=True, out_dtype=BF16, tm=tm, tn=tn, name="ffn1_up",
                              casts=[(w_ffn1_down, l, whole(w_ffn1_down), tn_ff)])
        h, (w_main, w_ba_src) = _matmul_residual(
            act, w1d, h, mods, n_rows=R, gate_idx=2, scale=0.5, tm=tm, name="ffn1_down", **common,
            casts=[(w_in, l, [(0, ba_col0), (sc_col0_src, w_in.shape[2])], tn),
                   (w_in, l, [(ba_col0, sc_col0_src)], None)])
        w_ba = jnp.dot(w_ba_src, ba_onehot, preferred_element_type=F32).astype(BF16).reshape(1, D, LANES)

        hn = _adaln(h, norm_mix[l].reshape(1, D), mods, n_rows=R, shift_idx=3, scale_idx=4, tr=tg,
                    name="mix_adaln", **common)
        px, (w2gu, wo) = _matmul(hn, w_main, swiglu=False, out_dtype=F32, tm=tm, name="in_proj",
                                 casts=[(w_ffn2_gu, l, whole(w_ffn2_gu), tn), (w_out, l, whole(w_out), tn)])
        ba, _ = _matmul(hn, w_ba, swiglu=False, out_dtype=F32, tm=tm, name="in_proj_ba")

        conv = dict(tr=tr, **seg)
        (qn,) = _qkv_conv(px, conv_qkv[l][:, 0:qk_dim], kind='q', col0=0, width=qk_dim, tc=tc_qk, **conv)
        kn, knt = _qkv_conv(px, conv_qkv[l][:, qk_dim:2 * qk_dim], kind='k', col0=qk_dim, width=qk_dim,
                            tc=tc_qk, **conv)
        (vn,) = _qkv_conv(px, conv_qkv[l][:, 2 * qk_dim:], kind='v', col0=2 * qk_dim, width=v_dim,
                          tc=tc_v, **conv)
        cols, rows = _gdn_gates(ba, _gate_param_row(a_log[l], n_vheads), _gate_param_row(dt_bias[l], n_vheads),
                                n_pairs=n_pairs, tm=tg)
        ymix, y_ctx = _gdn(qn, kn, knt, vn, cols, rows, px, z_col0, gdn_norm[l].reshape(1, HEAD_DIM),
                           n_batch=B, seq=S, ctx_len=CL, out_cols=v_dim + sc_dim + pool_dim)
        if l + 1 < L:
            ymix = _place_context_rows(y_ctx, ymix, row_blk0=bs_rows // CL, blk_rows=CL,
                                       blk_cols=math.gcd(v_dim, 4 * LANES))

        ymix = _shortconv(px, conv_short[l], ymix, x_col0=sc_col0, width=sc_dim, out_col0=v_dim,
                          tr=tr, tc=_pick_tile(sc_dim, (1024, 512, 256, 128)), **seg)
        pw = pool_w[l].astype(BF16)
        psc = pool_scale[l].reshape(1, pool_dim)
        ymix = _pool(px, pw, psc, ymix, u_col0=pool_col0, out_col0=v_dim + sc_dim, n_tok=S, unit=GRID_W,
                     blk0=0, n_batch=B, name="pool_latent")
        ymix = _pool(px, pw, psc, ymix, u_col0=pool_col0, out_col0=v_dim + sc_dim, n_tok=CL, unit=1,
                     blk0=bs_rows // CL, n_batch=B, name="pool_context")
        h, _ = _matmul_residual(ymix, wo, h, mods, n_rows=n_live, gate_idx=5, scale=1.0, tm=tm,
                                name="mix_out", **common)

        hn = _adaln(h, norm_ffn2[l].reshape(1, D), mods, n_rows=n_live, shift_idx=6, scale_idx=7, tr=tg,
                    name="ffn2_adaln", **common)
        act, (w2d,) = _matmul(hn, w2gu, swiglu=True, out_dtype=BF16, tm=tm, name="ffn2_up",
                              casts=[(w_ffn2_down, l, whole(w_ffn2_down), tn_ff)])
        next_casts = [(w_ffn1_gu, l + 1, whole(w_ffn1_gu), tn)] if l + 1 < L else []
        h, nxt = _matmul_residual(act, w2d, h, mods, n_rows=n_live, gate_idx=8, scale=0.5, tm=tm,
                                  name="ffn2_down", **common, casts=next_casts)
        if nxt:
            w1gu = nxt[0]

    out = _final_rmsnorm(h, norm_final.reshape(1, D), bs_rows, tg)
    return out.reshape(B, S, D)
```

```python
import functools
import math

import numpy as np
import jax
import jax.numpy as jnp
from jax import lax
from jax.experimental import pallas as pl
from jax.experimental.pallas import tpu as pltpu

EPS = 1e-6
N_MOD = 9
GRID_W = 64
HEAD_DIM = 128
GDN_CONV = 5
SC_CONV = 3
POOL_WINDOWS = (2, 4, 8, 16)
CHUNK = 128
INV_BLOCK = 16
LANES = 128
SUBLANES = 8
BF16_SUBLANES = 16
VMEM_LIMIT = 56 * 1024 * 1024

BF16 = jnp.bfloat16
F32 = jnp.float32


def _cparams(*sem):
    return pltpu.CompilerParams(dimension_semantics=sem, vmem_limit_bytes=VMEM_LIMIT)


def _dot(a, b):
    return jnp.dot(a, b, preferred_element_type=F32)


def _dot_hi(a, b):
    return jnp.dot(a, b, preferred_element_type=F32, precision=lax.Precision.HIGHEST)


def _silu(v):
    return v * jax.nn.sigmoid(v)


def _mod_kernel(c_ref, w_ref, b_ref, o_ref):
    a = _silu(c_ref[...]).astype(BF16)
    o_ref[...] = _dot(a, w_ref[...].astype(BF16)) + b_ref[...]


def _modulation(c_all, w_mod, b_mod, tn=512):
    L, D, N = w_mod.shape
    return pl.pallas_call(
        _mod_kernel,
        grid=(L, N // tn),
        in_specs=[pl.BlockSpec((SUBLANES, D), lambda l, j: (0, 0)),
                  pl.BlockSpec((None, D, tn), lambda l, j: (l, 0, j)),
                  pl.BlockSpec((None, 1, tn), lambda l, j: (l, 0, j))],
        out_specs=pl.BlockSpec((None, SUBLANES, tn), lambda l, j: (l, 0, j)),
        out_shape=jax.ShapeDtypeStruct((L, SUBLANES, N), F32),
        compiler_params=_cparams("parallel", "parallel"),
        name="modulation",
    )(c_all, w_mod, b_mod)


def _two_part_specs(parts, block, n_first, col_of):
    first = pl.BlockSpec(block, lambda i, *j: (jnp.minimum(i, n_first - 1), *col_of(i < n_first, *j)))
    second = pl.BlockSpec(block, lambda i, *j: (jnp.maximum(i - n_first, 0), *col_of(i >= n_first, *j)))
    return [first, second]


def _adaln_kernel(*refs, shift_idx, scale_idx, n_first):
    if n_first is None:
        h_ref, nw_ref, mod_ref, o_ref = refs
        h = h_ref[...]
    else:
        x_ref, c_ref, nw_ref, mod_ref, o_ref = refs
        h = jnp.where(pl.program_id(0) < n_first, x_ref[...], c_ref[...])
    ms = jnp.mean(h * h, axis=-1, keepdims=True)
    y = h * lax.rsqrt(ms + EPS) * nw_ref[...]
    shift = mod_ref[0, shift_idx:shift_idx + 1, :]
    scale = mod_ref[0, scale_idx:scale_idx + 1, :]
    o_ref[...] = (y * (1.0 + scale) + shift).astype(o_ref.dtype)


def _adaln(h, norm_w, mods, *, n_rows, seq_rows, n_batch, shift_idx, scale_idx, tr, name):
    parts = h if isinstance(h, tuple) else (h,)
    D = parts[0].shape[1]
    mod_map = lambda i: (jnp.minimum((i * tr) // seq_rows, n_batch), 0, 0)
    if len(parts) == 2:
        n_first = parts[0].shape[0] // tr
        h_specs = _two_part_specs(parts, (tr, D), n_first, lambda active: (0,))
    else:
        n_first = None
        h_specs = [pl.BlockSpec((tr, D), lambda i: (i, 0))]
    return pl.pallas_call(
        functools.partial(_adaln_kernel, shift_idx=shift_idx, scale_idx=scale_idx, n_first=n_first),
        grid=(n_rows // tr,),
        in_specs=h_specs + [pl.BlockSpec((1, D), lambda i: (0, 0)),
                            pl.BlockSpec((1, N_MOD, D), mod_map)],
        out_specs=pl.BlockSpec((tr, D), lambda i: (i, 0)),
        out_shape=jax.ShapeDtypeStruct((n_rows, D), BF16),
        compiler_params=_cparams("parallel"),
        name=name,
    )(*parts, norm_w, mods)


def _cast_row_block(rows, steps):
    for rps in range(BF16_SUBLANES, rows + 1, BF16_SUBLANES):
        if rows % rps == 0 and rows // rps <= steps:
            return rps
    raise ValueError(f"cannot spread {rows} rows over {steps} steps")


def _cast_plan(casts, n_i, nj):
    in_specs, out_specs, out_shapes, args, plans = [], [], [], [], []
    for src, layer, ranges, tile in casts:
        _, rows, cols = src.shape
        rps = _cast_row_block(rows, n_i * nj)
        last = rows // rps - 1
        out_cols = sum(b - a for a, b in ranges)
        src_map = lambda i, j, last=last, layer=layer: (layer, jnp.minimum(i * nj + j, last), 0)
        in_specs.append(pl.BlockSpec((None, rps, cols), src_map))
        if tile is None:
            out_specs.append(pl.BlockSpec((rps, out_cols), lambda i, j, last=last: (jnp.minimum(i * nj + j, last), 0)))
            out_shapes.append(jax.ShapeDtypeStruct((rows, out_cols), BF16))
            pieces = tuple((None, a, b) for a, b in ranges)
        else:
            assert all((b - a) % tile == 0 for a, b in ranges)
            nt = out_cols // tile
            out_specs.append(pl.BlockSpec((nt, rps, tile),
                                          lambda i, j, last=last: (0, jnp.minimum(i * nj + j, last), 0)))
            out_shapes.append(jax.ShapeDtypeStruct((nt, rows, tile), BF16))
            pieces = tuple((None, s, s + tile) for a, b in ranges for s in range(a, b, tile))
            pieces = tuple((t, a, b) for t, (_, a, b) in enumerate(pieces))
        args.append(src)
        plans.append(pieces)
    return in_specs, out_specs, out_shapes, args, plans


def _run_casts(src_refs, dst_refs, plans):
    for src_ref, dst_ref, pieces in zip(src_refs, dst_refs, plans):
        off = 0
        for t, a, b in pieces:
            if t is None:
                dst_ref[:, off:off + b - a] = src_ref[:, a:b].astype(dst_ref.dtype)
                off += b - a
            else:
                dst_ref[t] = src_ref[:, a:b].astype(dst_ref.dtype)


def _mm_kernel(a_ref, *rest, swiglu, cast_ranges):
    nc = len(cast_ranges)
    n_w = 2 if swiglu else 1
    w_refs, src_refs = rest[:n_w], rest[n_w:n_w + nc]
    o_ref, dst_refs = rest[n_w + nc], rest[n_w + nc + 1:]
    a = a_ref[...]
    if swiglu:
        half = o_ref.shape[1] // 2
        for cs in (slice(0, half), slice(half, 2 * half)):
            g = _dot(a, w_refs[0][:, cs])
            u = _dot(a, w_refs[1][:, cs])
            o_ref[:, cs] = (_silu(g) * u).astype(o_ref.dtype)
    else:
        o_ref[...] = _dot(a, w_refs[0][...]).astype(o_ref.dtype)
    _run_casts(src_refs, dst_refs, cast_ranges)


def _weight_specs(w, tn, swiglu):
    if w.ndim == 2:
        k, nt = w.shape[0], w.shape[1] // tn
        nj = nt // 2 if swiglu else nt
        specs = [pl.BlockSpec((k, tn), lambda i, j: (0, j))]
        if swiglu:
            specs.append(pl.BlockSpec((k, tn), lambda i, j: (0, j + nj)))
    else:
        nt, k, tn = w.shape
        nj = nt // 2 if swiglu else nt
        specs = [pl.BlockSpec((None, k, tn), lambda i, j: (j, 0, 0))]
        if swiglu:
            specs.append(pl.BlockSpec((None, k, tn), lambda i, j: (j + nj, 0, 0)))
    return specs, nj, tn


def _matmul(a, w, *, swiglu, out_dtype, tm, name, tn=None, casts=()):
    R, K = a.shape
    w_specs, nj, tn = _weight_specs(w, tn, swiglu)
    n_out = nj * tn
    in_specs = [pl.BlockSpec((tm, K), lambda i, j: (i, 0))] + w_specs
    args = [a] + [w] * len(w_specs)
    c_in, c_out, c_shapes, c_args, c_ranges = _cast_plan(casts, R // tm, nj)
    res = pl.pallas_call(
        functools.partial(_mm_kernel, swiglu=swiglu, cast_ranges=c_ranges),
        grid=(R // tm, nj),
        in_specs=in_specs + c_in,
        out_specs=[pl.BlockSpec((tm, tn), lambda i, j: (i, j))] + c_out,
        out_shape=[jax.ShapeDtypeStruct((R, n_out), out_dtype)] + c_shapes,
        compiler_params=_cparams("arbitrary", "arbitrary"),
        name=name,
    )(*args, *c_args)
    return res[0], res[1:]


def _mm_res_kernel(a_ref, w_ref, *rest, gate_idx, scale, cast_ranges, n_first):
    nc = len(cast_ranges)
    if n_first is None:
        res = rest[0][...]
        rest = rest[1:]
    else:
        res = jnp.where(pl.program_id(0) < n_first, rest[0][...], rest[1][...])
        rest = rest[2:]
    mod_ref, src_refs, o_ref, dst_refs = rest[0], rest[1:1 + nc], rest[1 + nc], rest[2 + nc:]
    acc = _dot(a_ref[...], w_ref[...])
    gate = mod_ref[0, gate_idx:gate_idx + 1, :]
    o_ref[...] = res + (scale * gate) * acc
    _run_casts(src_refs, dst_refs, cast_ranges)


def _matmul_residual(a, w, res, mods, *, n_rows, seq_rows, n_batch, gate_idx, scale, tm, name, casts=()):
    K = a.shape[1]
    nt, _, tn = w.shape
    D = nt * tn
    mod_map = lambda i, j: (jnp.minimum((i * tm) // seq_rows, n_batch), 0, j)
    parts = res if isinstance(res, tuple) else (res,)
    if len(parts) == 2:
        n_first = parts[0].shape[0] // tm
        res_specs = _two_part_specs(parts, (tm, tn), n_first, lambda active, j: (jnp.where(active, j, 0),))
    else:
        n_first = None
        res_specs = [pl.BlockSpec((tm, tn), lambda i, j: (i, j))]
    c_in, c_out, c_shapes, c_args, c_ranges = _cast_plan(casts, n_rows // tm, D // tn)
    out = pl.pallas_call(
        functools.partial(_mm_res_kernel, gate_idx=gate_idx, scale=scale, cast_ranges=c_ranges, n_first=n_first),
        grid=(n_rows // tm, D // tn),
        in_specs=[pl.BlockSpec((tm, K), lambda i, j: (i, 0)),
                  pl.BlockSpec((None, K, tn), lambda i, j: (j, 0, 0))] + res_specs
                 + [pl.BlockSpec((1, N_MOD, tn), mod_map)] + c_in,
        out_specs=[pl.BlockSpec((tm, tn), lambda i, j: (i, j))] + c_out,
        out_shape=[jax.ShapeDtypeStruct((n_rows, D), F32)] + c_shapes,
        compiler_params=_cparams("arbitrary", "arbitrary"),
        name=name,
    )(a, w, *parts, mods, *c_args)
    return out[0], out[1:]


def _rmsnorm_kernel(h_ref, w_ref, o_ref):
    h = h_ref[...]
    ms = jnp.mean(h * h, axis=-1, keepdims=True)
    o_ref[...] = h * lax.rsqrt(ms + EPS) * w_ref[...]


def _final_rmsnorm(h, w, rows, tm):
    D = h.shape[1]
    return pl.pallas_call(
        _rmsnorm_kernel,
        grid=(rows // tm,),
        in_specs=[pl.BlockSpec((tm, D), lambda i: (i, 0)),
                  pl.BlockSpec((1, D), lambda i: (0, 0))],
        out_specs=pl.BlockSpec((tm, D), lambda i: (i, 0)),
        out_shape=jax.ShapeDtypeStruct((rows, D), F32),
        compiler_params=_cparams("parallel"),
        name="final_rmsnorm",
    )(h, w)


def _segment_position(tile_rows, bs_rows, seq, ctx_len):
    row0 = pl.program_id(0) * tile_rows
    in_x = row0 < bs_rows
    off = jnp.where(in_x, lax.rem(row0, seq), lax.rem(row0 - bs_rows, ctx_len))
    seglen = jnp.where(in_x, seq, ctx_len)
    return in_x, off, seglen


def _qkv_conv_kernel(cur_ref, prev_ref, next_ref, w_ref, *rest, kind, tr, bs_rows, seq, ctx_len):
    if kind == 'k':
        o_ref, ot_ref, ext_ref = rest
    else:
        o_ref, ext_ref = rest
    _, off, seglen = _segment_position(tr, bs_rows, seq, ctx_len)
    is_start = off == 0
    is_end = off + tr == seglen
    pad = SUBLANES
    half = GDN_CONV // 2
    ext_ref[0:pad, :] = jnp.where(is_start, 0.0, prev_ref[...])
    ext_ref[pad:pad + tr, :] = cur_ref[...]
    ext_ref[pad + tr:pad + tr + pad, :] = jnp.where(is_end, 0.0, next_ref[...])
    ext = ext_ref[...]
    n_ext = tr + 2 * pad
    acc = None
    for j in range(GDN_CONV):
        shifted = ext if j == half else pltpu.roll(ext, (half - j) % n_ext, axis=0)
        term = shifted[pad:pad + tr, :] * w_ref[j:j + 1, :]
        acc = term if acc is None else acc + term
    y = _silu(acc)
    tc = y.shape[1]
    if kind == 'v':
        o_ref[...] = y.astype(o_ref.dtype)
        return
    mult = HEAD_DIM ** -0.5 if kind == 'q' else 1.0
    for hd in range(tc // HEAD_DIM):
        yh = y[:, hd * HEAD_DIM:(hd + 1) * HEAD_DIM]
        inv = lax.rsqrt(jnp.sum(yh * yh, axis=-1, keepdims=True) + EPS)
        yn = yh * inv
        if kind == 'q':
            yn = yn * mult
        o_ref[:, hd * HEAD_DIM:(hd + 1) * HEAD_DIM] = yn.astype(o_ref.dtype)
        if kind == 'k':
            ot_ref[hd * HEAD_DIM:(hd + 1) * HEAD_DIM, :] = yn.T.astype(ot_ref.dtype)


def _qkv_conv(px, conv_w, *, kind, col0, width, tr, tc, bs_rows, seq, ctx_len):
    R = px.shape[0]
    cb0 = col0 // tc
    rb = tr // SUBLANES
    last = R // SUBLANES - 1
    in_specs = [pl.BlockSpec((tr, tc), lambda i, j: (i, cb0 + j)),
                pl.BlockSpec((SUBLANES, tc), lambda i, j: (jnp.maximum(i * rb - 1, 0), cb0 + j)),
                pl.BlockSpec((SUBLANES, tc), lambda i, j: (jnp.minimum((i + 1) * rb, last), cb0 + j)),
                pl.BlockSpec((GDN_CONV, tc), lambda i, j: (0, j))]
    out_specs = [pl.BlockSpec((tr, tc), lambda i, j: (i, j))]
    out_shape = [jax.ShapeDtypeStruct((R, width), BF16)]
    if kind == 'k':
        out_specs.append(pl.BlockSpec((tc, tr), lambda i, j: (j, i)))
        out_shape.append(jax.ShapeDtypeStruct((width, R), BF16))
    res = pl.pallas_call(
        functools.partial(_qkv_conv_kernel, kind=kind, tr=tr, bs_rows=bs_rows, seq=seq, ctx_len=ctx_len),
        grid=(R // tr, width // tc),
        in_specs=in_specs,
        out_specs=out_specs,
        out_shape=out_shape,
        scratch_shapes=[pltpu.VMEM((tr + 2 * SUBLANES, tc), F32)],
        compiler_params=_cparams("parallel", "parallel"),
        name="gdn_conv_" + kind,
    )(px, px, px, conv_w)
    return res


def _gate_kernel(ba_ref, alog_ref, dtb_ref, cols_ref, rows_ref, *, tm, n_pairs):
    lane = lax.broadcasted_iota(jnp.int32, (CHUNK, LANES), 1)
    is_beta = (lane & 1) == 0
    is_bwd = (lane & 4) != 0
    ri = lax.broadcasted_iota(jnp.int32, (CHUNK, CHUNK), 0)
    ci = lax.broadcasted_iota(jnp.int32, (CHUNK, CHUNK), 1)
    tri_lo = jnp.where(ci <= ri, 1.0, 0.0)
    tri_up = jnp.where(ci >= ri, 1.0, 0.0)
    neg_a = -jnp.exp(alog_ref[...])
    for c in range(tm // CHUNK):
        sl = slice(c * CHUNK, (c + 1) * CHUNK)
        x = ba_ref[sl, :]
        beta = jax.nn.sigmoid(x)
        y = x + dtb_ref[...]
        softplus = jnp.maximum(y, 0.0) + jnp.log1p(jnp.exp(-jnp.abs(y)))
        g = jnp.where(is_beta, 0.0, neg_a * softplus)
        gc = jnp.where(is_bwd, _dot_hi(tri_up, g), _dot_hi(tri_lo, g))
        full = jnp.where(is_beta, beta, gc)
        for p in range(n_pairs):
            cols_ref[p, sl, :] = full[:, p * 8:(p + 1) * 8]
        rows_ref[:, sl] = full.T[0:n_pairs * 8, :]


def _gdn_gates(ba, alog_row, dtb_row, *, n_pairs, tm):
    R = ba.shape[0]
    return pl.pallas_call(
        functools.partial(_gate_kernel, tm=tm, n_pairs=n_pairs),
        grid=(R // tm,),
        in_specs=[pl.BlockSpec((tm, LANES), lambda i: (i, 0)),
                  pl.BlockSpec((1, LANES), lambda i: (0, 0)),
                  pl.BlockSpec((1, LANES), lambda i: (0, 0))],
        out_specs=[pl.BlockSpec((n_pairs, tm, 8), lambda i: (0, i, 0)),
                   pl.BlockSpec((n_pairs * 8, tm), lambda i: (0, i))],
        out_shape=[jax.ShapeDtypeStruct((n_pairs, R, 8), F32),
                   jax.ShapeDtypeStruct((n_pairs * 8, R), F32)],
        compiler_params=_cparams("parallel"),
        name="gdn_gates",
    )(ba, alog_row, dtb_row)


def _gdn_kernel(qx_ref, qc_ref, kx_ref, kc_ref, ktx_ref, ktc_ref, vx_ref, vc_ref,
                colx_ref, colc_ref, rowx_ref, rowc_ref, zx_ref, zc_ref, nw_ref, ox_ref, oc_ref,
                t_ref, a_ref, of_ref, ob_ref, sf_ref, sb_ref, *, seq, ctx_len, rb, n_pair, prep_chunks):
    C = CHUNK
    HD = HEAD_DIM
    n_xc = seq // C
    n_cc = ctx_len // C
    x_refs = (qx_ref, kx_ref, ktx_ref, vx_ref, colx_ref, rowx_ref)
    c_refs = (qc_ref, kc_ref, ktc_ref, vc_ref, colc_ref, rowc_ref)

    ri = lax.broadcasted_iota(jnp.int32, (C, C), 0)
    ci = lax.broadcasted_iota(jnp.int32, (C, C), 1)
    eye = (ri == ci).astype(F32)

    def lane_idx(d, s, kind):
        return d * 4 + s * 2 + kind

    block_diff = ri ^ ci

    def unit_triangular_inverses(nmats):
        base = int(math.log2(INV_BLOCK))
        n0s = [jnp.where((block_diff >> base) == 0, n, 0.0) for n in nmats]
        invs = [eye + n0 for n0 in n0s]
        pws = n0s
        for _ in range(base - 1):
            pwbs = [pw.astype(BF16) for pw in pws]
            pws = [_dot(b, b) for b in pwbs]
            invs = [inv + _dot(inv.astype(BF16), pw.astype(BF16)) for inv, pw in zip(invs, pws)]
        for level in range(base, int(math.log2(C))):
            offs = [jnp.where((block_diff >> level) == 1, n, 0.0).astype(BF16) for n in nmats]
            invbs = [inv.astype(BF16) for inv in invs]
            halves = [_dot(ib, off).astype(BF16) for ib, off in zip(invbs, offs)]
            invs = [inv + _dot(h, ib) for inv, h, ib in zip(invs, halves, invbs)]
        return invs

    def prepare(chunks):
        nmats, dests = [], []
        for refs, row0, slot in chunks:
            q_ref, k_ref, kt_ref, _, col_ref, row_ref = refs
            rows = pl.ds(row0, C)
            for u in range(n_pair):
                hs = slice(u * HD, (u + 1) * HD)
                qk = jnp.concatenate([q_ref[rows, hs], k_ref[rows, hs]], axis=0)
                gram = _dot(qk, kt_ref[hs, rows])
                qkt = gram[0:C, :]
                kkt = gram[C:2 * C, :]
                cols = col_ref[u, rows, :]
                for d in range(2):
                    incl = (ci <= ri) if d == 0 else (ci >= ri)
                    strict = (ci < ri) if d == 0 else (ci > ri)
                    for s in range(2):
                        kb, kg = lane_idx(d, s, 0), lane_idx(d, s, 1)
                        beta_col = cols[:, kb:kb + 1]
                        gc_col = cols[:, kg:kg + 1]
                        gc_row = row_ref[u * 8 + kg:u * 8 + kg + 1, rows]
                        decay = jnp.where(incl, jnp.exp(jnp.where(incl, gc_col - gc_row, 0.0)), 0.0)
                        nmats.append(jnp.where(strict, -(kkt * beta_col) * decay, 0.0))
                        a_ref[slot, u * 4 + d * 2 + s] = (qkt * decay).astype(BF16)
                        dests.append((slot, u * 4 + d * 2 + s))
        for (slot, idx), inv in zip(dests, unit_triangular_inverses(nmats)):
            t_ref[slot, idx] = inv.astype(BF16)

    def chain(tasks):
        units = [(d, u, refs, pl.ds(row0, C), slot, orow0)
                 for d, refs, row0, slot, orow0 in tasks for u in range(n_pair)]
        states, projs = [], []
        for d, u, refs, rows, _, _ in units:
            hs = slice(u * HD, (u + 1) * HD)
            state = (sf_ref if d == 0 else sb_ref)[u]
            qk = jnp.concatenate([refs[0][rows, hs], refs[1][rows, hs]], axis=0)
            states.append(state)
            projs.append(_dot(qk, state.astype(BF16)))
        scalars, rhss = [], []
        for (d, u, refs, rows, _, _), proj in zip(units, projs):
            cols = refs[4][u, rows, :]
            for s in range(2):
                kb, kg = lane_idx(d, s, 0), lane_idx(d, s, 1)
                beta_col = cols[:, kb:kb + 1]
                gc_col = cols[:, kg:kg + 1]
                g_last = gc_col[C - 1:C, :] if d == 0 else gc_col[0:1, :]
                e_col = jnp.exp(gc_col)
                vs = slice((2 * u + s) * HD, (2 * u + s + 1) * HD)
                v = refs[3][rows, vs].astype(F32)
                k_s = proj[C:2 * C, s * HD:(s + 1) * HD]
                rhss.append((beta_col * (v - e_col * k_s)).astype(BF16))
                scalars.append((gc_col, e_col, g_last))
        v_news = []
        for i, (d, u, _, _, slot, _) in enumerate(units):
            for s in range(2):
                v_news.append(_dot(t_ref[slot, u * 4 + d * 2 + s], rhss[2 * i + s]))
        for i, (d, u, refs, rows, slot, orow0) in enumerate(units):
            o_acc = of_ref if d == 0 else ob_ref
            evs, gls = [], []
            for s in range(2):
                gc_col, e_col, g_last = scalars[2 * i + s]
                v_new = v_news[2 * i + s]
                q_s = projs[i][0:C, s * HD:(s + 1) * HD]
                out = e_col * q_s + _dot(a_ref[slot, u * 4 + d * 2 + s], v_new.astype(BF16))
                o_acc[pl.ds(orow0, C), (2 * u + s) * HD:(2 * u + s + 1) * HD] = out
                evs.append(jnp.exp(g_last - gc_col) * v_new)
                gls.append(jnp.broadcast_to(jnp.exp(g_last), (HD, HD)))
            ev = jnp.concatenate(evs, axis=1).astype(BF16)
            new_state = states[i] * jnp.concatenate(gls, axis=1) + _dot(refs[2][u * HD:(u + 1) * HD, rows], ev)
            (sf_ref if d == 0 else sb_ref)[u] = new_state

    sf_ref[...] = jnp.zeros_like(sf_ref)
    sb_ref[...] = jnp.zeros_like(sb_ref)
    prepare([(c_refs, n * C, n) for n in range(n_cc)])

    def prep_body(it, carry):
        n0 = it * prep_chunks
        prepare([(x_refs, pl.multiple_of((n0 + j) * C, C), n_cc + n0 + j) for j in range(prep_chunks)])
        return carry
    lax.fori_loop(0, n_xc // prep_chunks, prep_body, 0)

    for n in range(n_cc):
        m = n_cc - 1 - n
        chain([(0, c_refs, n * C, n, seq + n * C), (1, c_refs, m * C, m, seq + m * C)])

    def chain_body(n, carry):
        m = n_xc - 1 - n
        rf = pl.multiple_of(n * C, C)
        rbk = pl.multiple_of(m * C, C)
        chain([(0, x_refs, rf, n_cc + n, rf), (1, x_refs, rbk, n_cc + m, rbk)])
        return carry
    lax.fori_loop(0, n_xc, chain_body, 0)

    def finish(z_ref, out_ref, base, n_blocks):
        def body(bi, carry):
            r0 = pl.multiple_of(bi * rb, rb)
            rows = pl.ds(r0, rb)
            acc_rows = pl.ds(pl.multiple_of(base + r0, rb), rb)
            o = of_ref[acc_rows, :] + ob_ref[acc_rows, :]
            z = z_ref[rows, :]
            for hd in range(2 * n_pair):
                hs = slice(hd * HD, (hd + 1) * HD)
                oh = o[:, hs]
                ms = jnp.mean(oh * oh, axis=-1, keepdims=True)
                y = oh * lax.rsqrt(ms + EPS) * nw_ref[...]
                out_ref[rows, hs] = (y * _silu(z[:, hs])).astype(out_ref.dtype)
            return carry
        lax.fori_loop(0, n_blocks, body, 0)

    finish(zx_ref, ox_ref, 0, seq // rb)
    finish(zc_ref, oc_ref, seq, ctx_len // rb)


def _gdn(qn, kn, knt, vn, cols, rows, px, z_col0, norm_w, *, n_batch, seq, ctx_len, out_cols):
    R = qn.shape[0]
    n_pairs = qn.shape[1] // HEAD_DIM
    C = CHUNK
    rb = min(256, ctx_len)
    cblk0 = n_batch * (seq // ctx_len)
    n_slots = (seq + ctx_len) // C
    n_pair = 2 if n_pairs % 2 == 0 else 1
    prep_chunks = 2 if (seq // C) % 2 == 0 else 1
    qw = n_pair * HEAD_DIM
    vw = 2 * qw
    zb0 = z_col0 // vw
    assert z_col0 % vw == 0

    x_rows = lambda w, c0=0: pl.BlockSpec((seq, w), lambda b, p: (b, c0 + p))
    c_rows = lambda w, c0=0: pl.BlockSpec((ctx_len, w), lambda b, p: (cblk0 + b, c0 + p))
    in_specs = [
        x_rows(qw), c_rows(qw),
        x_rows(qw), c_rows(qw),
        pl.BlockSpec((qw, seq), lambda b, p: (p, b)),
        pl.BlockSpec((qw, ctx_len), lambda b, p: (p, cblk0 + b)),
        x_rows(vw), c_rows(vw),
        pl.BlockSpec((n_pair, seq, 8), lambda b, p: (p, b, 0)),
        pl.BlockSpec((n_pair, ctx_len, 8), lambda b, p: (p, cblk0 + b, 0)),
        pl.BlockSpec((n_pair * 8, seq), lambda b, p: (p, b)),
        pl.BlockSpec((n_pair * 8, ctx_len), lambda b, p: (p, cblk0 + b)),
        x_rows(vw, zb0), c_rows(vw, zb0),
        pl.BlockSpec((1, HEAD_DIM), lambda b, p: (0, 0)),
    ]
    return pl.pallas_call(
        functools.partial(_gdn_kernel, seq=seq, ctx_len=ctx_len, rb=rb, n_pair=n_pair, prep_chunks=prep_chunks),
        grid=(n_batch, n_pairs // n_pair),
        in_specs=in_specs,
        out_specs=[pl.BlockSpec((seq, vw), lambda b, p: (b, p)),
                   pl.BlockSpec((ctx_len, vw), lambda b, p: (b, p))],
        out_shape=[jax.ShapeDtypeStruct((R, out_cols), BF16),
                   jax.ShapeDtypeStruct((n_batch * ctx_len, vn.shape[1]), BF16)],
        scratch_shapes=[pltpu.VMEM((n_slots, n_pair * 4, C, C), BF16),
                        pltpu.VMEM((n_slots, n_pair * 4, C, C), BF16),
                        pltpu.VMEM((seq + ctx_len, vw), F32),
                        pltpu.VMEM((seq + ctx_len, vw), F32),
                        pltpu.VMEM((n_pair, HEAD_DIM, 2 * HEAD_DIM), F32),
                        pltpu.VMEM((n_pair, HEAD_DIM, 2 * HEAD_DIM), F32)],
        compiler_params=_cparams("parallel", "parallel"),
        name="gdn_delta_rule",
    )(qn, qn, kn, kn, knt, knt, vn, vn, cols, cols, rows, rows, px, px, norm_w)


def _copy_kernel(src_ref, dst_hbm_ref, o_ref):
    del dst_hbm_ref
    o_ref[...] = src_ref[...]


def _place_context_rows(y_ctx, ymix, *, row_blk0, blk_rows, blk_cols):
    rows, cols = y_ctx.shape
    return pl.pallas_call(
        _copy_kernel,
        grid=(rows // blk_rows, cols // blk_cols),
        in_specs=[pl.BlockSpec((blk_rows, blk_cols), lambda i, j: (i, j)),
                  pl.BlockSpec(memory_space=pl.ANY)],
        out_specs=pl.BlockSpec((blk_rows, blk_cols), lambda i, j: (row_blk0 + i, j)),
        out_shape=jax.ShapeDtypeStruct(ymix.shape, ymix.dtype),
        input_output_aliases={1: 0},
        compiler_params=_cparams("parallel", "parallel"),
        name="place_context_rows",
    )(y_ctx, ymix)


def _shortconv_kernel(x_ref, b_ref, c_ref, w_ref, ymix_ref, o_ref, *, tr, bs_rows, seq, ctx_len):
    del ymix_ref
    in_x, off, _ = _segment_position(tr, bs_rows, seq, ctx_len)
    period = jnp.where(in_x, GRID_W, ctx_len)
    pos = (off + lax.broadcasted_iota(jnp.int32, (tr, 1), 0)) & (period - 1)
    v = c_ref[...] * x_ref[...]
    left = jnp.where(pos == 0, 0.0, pltpu.roll(v, 1, axis=0))
    right = jnp.where(pos == period - 1, 0.0, pltpu.roll(v, tr - 1, axis=0))
    y = left * w_ref[0:1, :] + v * w_ref[1:2, :] + right * w_ref[2:3, :]
    o_ref[...] = (b_ref[...] * y).astype(o_ref.dtype)


def _shortconv(px, conv_w, ymix, *, x_col0, width, out_col0, tr, tc, bs_rows, seq, ctx_len):
    R = px.shape[0]
    nb = width // tc
    xb0 = x_col0 // tc
    ob0 = out_col0 // tc
    return pl.pallas_call(
        functools.partial(_shortconv_kernel, tr=tr, bs_rows=bs_rows, seq=seq, ctx_len=ctx_len),
        grid=(R // tr, nb),
        in_specs=[pl.BlockSpec((tr, tc), lambda i, j: (i, xb0 + j)),
                  pl.BlockSpec((tr, tc), lambda i, j: (i, xb0 + nb + j)),
                  pl.BlockSpec((tr, tc), lambda i, j: (i, xb0 + 2 * nb + j)),
                  pl.BlockSpec((SC_CONV, tc), lambda i, j: (0, j)),
                  pl.BlockSpec(memory_space=pl.ANY)],
        out_specs=pl.BlockSpec((tr, tc), lambda i, j: (i, ob0 + j)),
        out_shape=jax.ShapeDtypeStruct(ymix.shape, ymix.dtype),
        input_output_aliases={4: 0},
        compiler_params=_cparams("parallel", "parallel"),
        name="short_conv",
    )(px, px, px, conv_w, ymix)


def _pool_kernel(u_ref, w_ref, sc_ref, ymix_ref, o_ref, pa_ref, pb_ref, *, n_tok, unit, pg):
    del ymix_ref
    n_pos = n_tok // unit
    reach = 8 * unit
    pad = -(-reach // SUBLANES) * SUBLANES
    mg = -(-4 * unit // SUBLANES) * SUBLANES
    tp = n_tok + 2 * pad
    zeros_mg = jnp.zeros((mg, pg), F32)
    zeros_pad = jnp.zeros((pad, pg), F32)
    for buf in (pa_ref, pb_ref):
        buf[0:mg, :] = zeros_mg
        buf[mg + tp:mg + tp + mg, :] = zeros_mg
    row = lax.broadcasted_iota(jnp.int32, (n_tok, 1), 0) // unit
    for g, win in enumerate(POOL_WINDOWS):
        cs = slice(g * pg, (g + 1) * pg)
        u = u_ref[:, cs]
        pa_ref[mg:mg + pad, :] = zeros_pad
        pa_ref[mg + pad:mg + pad + n_tok, :] = u
        pa_ref[mg + pad + n_tok:mg + tp, :] = zeros_pad
        src, dst = pa_ref, pb_ref
        dst[mg:mg + tp, :] = src[mg - unit:mg - unit + tp, :] + src[mg:mg + tp, :]
        src, dst = dst, src
        step, w = 1, 2
        while w < win:
            dst[mg:mg + tp, :] = (src[mg - step * unit:mg - step * unit + tp, :]
                                  + src[mg + step * unit:mg + step * unit + tp, :])
            src, dst = dst, src
            step, w = step * 2, w * 2
        total = src[mg + pad:mg + pad + n_tok, :]
        lo = jnp.maximum(row - win // 2, 0)
        hi = jnp.minimum(row - win // 2 + win, n_pos)
        count = (hi - lo).astype(F32)
        dlt = total / count - u
        y = _dot(dlt.astype(BF16), w_ref[g]) * sc_ref[:, cs]
        o_ref[:, cs] = y.astype(o_ref.dtype)


def _pool(px, pool_w, pool_scale, ymix, *, u_col0, out_col0, n_tok, unit, blk0, n_batch, name):
    width = pool_scale.shape[1]
    pg = width // len(POOL_WINDOWS)
    reach = 8 * unit
    pad = -(-reach // SUBLANES) * SUBLANES
    mg = -(-4 * unit // SUBLANES) * SUBLANES
    buf_rows = n_tok + 2 * pad + 2 * mg
    ub = u_col0 // width
    ob = out_col0 // width
    return pl.pallas_call(
        functools.partial(_pool_kernel, n_tok=n_tok, unit=unit, pg=pg),
        grid=(n_batch,),
        in_specs=[pl.BlockSpec((n_tok, width), lambda b: (blk0 + b, ub)),
                  pl.BlockSpec(pool_w.shape, lambda b: (0, 0, 0)),
                  pl.BlockSpec((1, width), lambda b: (0, 0)),
                  pl.BlockSpec(memory_space=pl.ANY)],
        out_specs=pl.BlockSpec((n_tok, width), lambda b: (blk0 + b, ob)),
        out_shape=jax.ShapeDtypeStruct(ymix.shape, ymix.dtype),
        input_output_aliases={3: 0},
        scratch_shapes=[pltpu.VMEM((buf_rows, pg), F32), pltpu.VMEM((buf_rows, pg), F32)],
        compiler_params=_cparams("parallel"),
        name=name,
    )(px, pool_w, pool_scale, ymix)


def _ba_permutation(n_heads):
    perm = np.zeros(4 * n_heads, np.int32)
    for p in range(n_heads // 2):
        for d in range(2):
            for s in range(2):
                for kind in range(2):
                    perm[p * 8 + d * 4 + s * 2 + kind] = d * 2 * n_heads + kind * n_heads + 2 * p + s
    return perm


def _gate_param_row(param, n_heads):
    row = jnp.zeros((LANES,), F32)
    idx, src_d, src_h = [], [], []
    for p in range(n_heads // 2):
        for d in range(2):
            for s in range(2):
                idx.append(p * 8 + d * 4 + s * 2 + 1)
                src_d.append(d)
                src_h.append(2 * p + s)
    row = row.at[np.array(idx)].set(param[np.array(src_d), np.array(src_h)])
    return row.reshape(1, LANES)


def _pick_tile(n, prefs):
    for t in prefs:
        if n % t == 0:
            return t
    raise ValueError(f"no tile for {n}")


def kernel(x, c, ctx, c_ctx, w_mod, b_mod, norm_ffn1, w_ffn1_gu, w_ffn1_down, norm_mix, w_in, conv_qkv, a_log,
           dt_bias, gdn_norm, conv_short, pool_w, pool_scale, w_out, norm_ffn2, w_ffn2_gu, w_ffn2_down, norm_final):
    B, S, D = x.shape
    CL = ctx.shape[1]
    L = w_mod.shape[0]
    n_vheads = a_log.shape[2]
    qk_dim = (n_vheads // 2) * HEAD_DIM
    v_dim = n_vheads * HEAD_DIM
    sc_dim = conv_short.shape[2]
    pool_dim = pool_scale.shape[1]
    n_ba = 4 * n_vheads
    n_pairs = n_vheads // 2
    bs_rows = B * S
    R = bs_rows + B * CL
    assert S % CHUNK == 0 and CL % CHUNK == 0 and S % CL == 0 and n_ba <= LANES

    tm = _pick_tile(math.gcd(S, R), (1024, 512, 256, 128))
    tg = _pick_tile(math.gcd(S, R), (512, 256, 128))
    tr = min(256, CL)
    assert (B * CL) % tm == 0
    tn = 512 if D % 512 == 0 else 256
    tn_ff = 256
    tc_qk = _pick_tile(qk_dim, (1024, 512, 256, 128))
    tc_v = math.gcd(v_dim, 2 * tc_qk)

    z_col0 = 2 * qk_dim + v_dim
    ba_col0 = z_col0 + v_dim
    sc_col0_src = ba_col0 + n_ba
    sc_col0 = ba_col0
    pool_col0 = sc_col0 + 3 * sc_dim
    ba_perm = _ba_permutation(n_vheads)
    onehot = np.zeros((n_ba, LANES), np.float32)
    onehot[ba_perm, np.arange(n_ba)] = 1.0
    ba_onehot = jnp.asarray(onehot, dtype=BF16)

    h = (x.reshape(bs_rows, D), ctx.reshape(B * CL, D))
    c_all = jnp.zeros((SUBLANES, D), F32).at[0:B].set(c).at[B].set(c_ctx)
    mods_all = _modulation(c_all, w_mod, b_mod.reshape(L, 1, N_MOD * D)).reshape(L, SUBLANES, N_MOD, D)

    common = dict(seq_rows=S, n_batch=B)
    seg = dict(bs_rows=bs_rows, seq=S, ctx_len=CL)
    whole = lambda w: [(0, w.shape[2])]
    w1gu = w_ffn1_gu[0].astype(BF16)
    for l in range(L):
        n_live = bs_rows if l == L - 1 else R
        mods = mods_all[l]

        hn = _adaln(h, norm_ffn1[l].reshape(1, D), mods, n_rows=R, shift_idx=0, scale_idx=1, tr=tg,
                    name="ffn1_adaln", **common)
        act, (w1d,) = _matmul(hn, w1gu, swiglu=True, out_dtype=BF16, tm=tm, tn=tn, name="ffn1_up",
                              casts=[(w_ffn1_down, l, whole(w_ffn1_down), tn_ff)])
        h, (w_main, w_ba_src) = _matmul_residual(
            act, w1d, h, mods, n_rows=R, gate_idx=2, scale=0.5, tm=tm, name="ffn1_down", **common,
            casts=[(w_in, l, [(0, ba_col0), (sc_col0_src, w_in.shape[2])], tn),
                   (w_in, l, [(ba_col0, sc_col0_src)], None)])
        w_ba = jnp.dot(w_ba_src, ba_onehot, preferred_element_type=F32).astype(BF16).reshape(1, D, LANES)

        hn = _adaln(h, norm_mix[l].reshape(1, D), mods, n_rows=R, shift_idx=3, scale_idx=4, tr=tg,
                    name="mix_adaln", **common)
        px, (w2gu, wo) = _matmul(hn, w_main, swiglu=False, out_dtype=F32, tm=tm, name="in_proj",
                                 casts=[(w_ffn2_gu, l, whole(w_ffn2_gu), tn), (w_out, l, whole(w_out), tn)])
        ba, _ = _matmul(hn, w_ba, swiglu=False, out_dtype=F32, tm=tm, name="in_proj_ba")

        conv = dict(tr=tr, **seg)
        (qn,) = _qkv_conv(px, conv_qkv[l][:, 0:qk_dim], kind='q', col0=0, width=qk_dim, tc=tc_qk, **conv)
        kn, knt = _qkv_conv(px, conv_qkv[l][:, qk_dim:2 * qk_dim], kind='k', col0=qk_dim, width=qk_dim,
                            tc=tc_qk, **conv)
        (vn,) = _qkv_conv(px, conv_qkv[l][:, 2 * qk_dim:], kind='v', col0=2 * qk_dim, width=v_dim,
                          tc=tc_v, **conv)
        cols, rows = _gdn_gates(ba, _gate_param_row(a_log[l], n_vheads), _gate_param_row(dt_bias[l], n_vheads),
                                n_pairs=n_pairs, tm=tg)
        ymix, y_ctx = _gdn(qn, kn, knt, vn, cols, rows, px, z_col0, gdn_norm[l].reshape(1, HEAD_DIM),
                           n_batch=B, seq=S, ctx_len=CL, out_cols=v_dim + sc_dim + pool_dim)
        if l + 1 < L:
            ymix = _place_context_rows(y_ctx, ymix, row_blk0=bs_rows // CL, blk_rows=CL,
                                       blk_cols=math.gcd(v_dim, 4 * LANES))

        ymix = _shortconv(px, conv_short[l], ymix, x_col0=sc_col0, width=sc_dim, out_col0=v_dim,
                          tr=tr, tc=_pick_tile(sc_dim, (1024, 512, 256, 128)), **seg)
        pw = pool_w[l].astype(BF16)
        psc = pool_scale[l].reshape(1, pool_dim)
        ymix = _pool(px, pw, psc, ymix, u_col0=pool_col0, out_col0=v_dim + sc_dim, n_tok=S, unit=GRID_W,
                     blk0=0, n_batch=B, name="pool_latent")
        ymix = _pool(px, pw, psc, ymix, u_col0=pool_col0, out_col0=v_dim + sc_dim, n_tok=CL, unit=1,
                     blk0=bs_rows // CL, n_batch=B, name="pool_context")
        h, _ = _matmul_residual(ymix, wo, h, mods, n_rows=n_live, gate_idx=5, scale=1.0, tm=tm,
                                name="mix_out", **common)

        hn = _adaln(h, norm_ffn2[l].reshape(1, D), mods, n_rows=n_live, shift_idx=6, scale_idx=7, tr=tg,
                    name="ffn2_adaln", **common)
        act, (w2d,) = _matmul(hn, w2gu, swiglu=True, out_dtype=BF16, tm=tm, name="ffn2_up",
                              casts=[(w_ffn2_down, l, whole(w_ffn2_down), tn_ff)])
        next_casts = [(w_ffn1_gu, l + 1, whole(w_ffn1_gu), tn)] if l + 1 < L else []
        h, nxt = _matmul_residual(act, w2d, h, mods, n_rows=n_live, gate_idx=8, scale=0.5, tm=tm,
                                  name="ffn2_down", **common, casts=next_casts)
        if nxt:
            w1gu = nxt[0]

    out = _final_rmsnorm(h, norm_final.reshape(1, D), bs_rows, tg)
    return out.reshape(B, S, D)
```

```python
import functools
import math

import numpy as np
import jax
import jax.numpy as jnp
from jax import lax
from jax.experimental import pallas as pl
from jax.experimental.pallas import tpu as pltpu

EPS = 1e-6
N_MOD = 9
GRID_W = 64
HEAD_DIM = 128
GDN_CONV = 5
SC_CONV = 3
POOL_WINDOWS = (2, 4, 8, 16)
CHUNK = 128
INV_BLOCK = 16
LANES = 128
SUBLANES = 8
BF16_SUBLANES = 16
VMEM_LIMIT = 56 * 1024 * 1024

BF16 = jnp.bfloat16
F32 = jnp.float32


def _cparams(*sem):
    return pltpu.CompilerParams(dimension_semantics=sem, vmem_limit_bytes=VMEM_LIMIT)


def _dot(a, b):
    return jnp.dot(a, b, preferred_element_type=F32)


def _dot_hi(a, b):
    return jnp.dot(a, b, preferred_element_type=F32, precision=lax.Precision.HIGHEST)


def _silu(v):
    return v * jax.nn.sigmoid(v)


def _mod_kernel(c_ref, w_ref, b_ref, o_ref):
    a = _silu(c_ref[...]).astype(BF16)
    o_ref[...] = _dot(a, w_ref[...].astype(BF16)) + b_ref[...]


def _modulation(c_all, w_mod, b_mod, tn=512):
    L, D, N = w_mod.shape
    return pl.pallas_call(
        _mod_kernel,
        grid=(L, N // tn),
        in_specs=[pl.BlockSpec((SUBLANES, D), lambda l, j: (0, 0)),
                  pl.BlockSpec((None, D, tn), lambda l, j: (l, 0, j)),
                  pl.BlockSpec((None, 1, tn), lambda l, j: (l, 0, j))],
        out_specs=pl.BlockSpec((None, SUBLANES, tn), lambda l, j: (l, 0, j)),
        out_shape=jax.ShapeDtypeStruct((L, SUBLANES, N), F32),
        compiler_params=_cparams("parallel", "parallel"),
        name="modulation",
    )(c_all, w_mod, b_mod)


def _two_part_specs(parts, block, n_first, col_of):
    first = pl.BlockSpec(block, lambda i, *j: (jnp.minimum(i, n_first - 1), *col_of(i < n_first, *j)))
    second = pl.BlockSpec(block, lambda i, *j: (jnp.maximum(i - n_first, 0), *col_of(i >= n_first, *j)))
    return [first, second]


def _adaln_kernel(*refs, shift_idx, scale_idx, n_first):
    if n_first is None:
        h_ref, nw_ref, mod_ref, o_ref = refs
        h = h_ref[...]
    else:
        x_ref, c_ref, nw_ref, mod_ref, o_ref = refs
        h = jnp.where(pl.program_id(0) < n_first, x_ref[...], c_ref[...])
    ms = jnp.mean(h * h, axis=-1, keepdims=True)
    y = h * lax.rsqrt(ms + EPS) * nw_ref[...]
    shift = mod_ref[0, shift_idx:shift_idx + 1, :]
    scale = mod_ref[0, scale_idx:scale_idx + 1, :]
    o_ref[...] = (y * (1.0 + scale) + shift).astype(o_ref.dtype)


def _adaln(h, norm_w, mods, *, n_rows, seq_rows, n_batch, shift_idx, scale_idx, tr, name):
    parts = h if isinstance(h, tuple) else (h,)
    D = parts[0].shape[1]
    mod_map = lambda i: (jnp.minimum((i * tr) // seq_rows, n_batch), 0, 0)
    if len(parts) == 2:
        n_first = parts[0].shape[0] // tr
        h_specs = _two_part_specs(parts, (tr, D), n_first, lambda active: (0,))
    else:
        n_first = None
        h_specs = [pl.BlockSpec((tr, D), lambda i: (i, 0))]
    return pl.pallas_call(
        functools.partial(_adaln_kernel, shift_idx=shift_idx, scale_idx=scale_idx, n_first=n_first),
        grid=(n_rows // tr,),
        in_specs=h_specs + [pl.BlockSpec((1, D), lambda i: (0, 0)),
                            pl.BlockSpec((1, N_MOD, D), mod_map)],
        out_specs=pl.BlockSpec((tr, D), lambda i: (i, 0)),
        out_shape=jax.ShapeDtypeStruct((n_rows, D), BF16),
        compiler_params=_cparams("parallel"),
        name=name,
    )(*parts, norm_w, mods)


def _cast_row_block(rows, steps):
    for rps in range(BF16_SUBLANES, rows + 1, BF16_SUBLANES):
        if rows % rps == 0 and rows // rps <= steps:
            return rps
    raise ValueError(f"cannot spread {rows} rows over {steps} steps")


def _cast_plan(casts, n_i, nj):
    in_specs, out_specs, out_shapes, args, plans = [], [], [], [], []
    for src, layer, ranges, tile in casts:
        _, rows, cols = src.shape
        rps = _cast_row_block(rows, n_i * nj)
        last = rows // rps - 1
        out_cols = sum(b - a for a, b in ranges)
        src_map = lambda i, j, last=last, layer=layer: (layer, jnp.minimum(i * nj + j, last), 0)
        in_specs.append(pl.BlockSpec((None, rps, cols), src_map))
        if tile is None:
            out_specs.append(pl.BlockSpec((rps, out_cols), lambda i, j, last=last: (jnp.minimum(i * nj + j, last), 0)))
            out_shapes.append(jax.ShapeDtypeStruct((rows, out_cols), BF16))
            pieces = tuple((None, a, b) for a, b in ranges)
        else:
            assert all((b - a) % tile == 0 for a, b in ranges)
            nt = out_cols // tile
            out_specs.append(pl.BlockSpec((nt, rps, tile),
                                          lambda i, j, last=last: (0, jnp.minimum(i * nj + j, last), 0)))
            out_shapes.append(jax.ShapeDtypeStruct((nt, rows, tile), BF16))
            pieces = tuple((None, s, s + tile) for a, b in ranges for s in range(a, b, tile))
            pieces = tuple((t, a, b) for t, (_, a, b) in enumerate(pieces))
        args.append(src)
        plans.append(pieces)
    return in_specs, out_specs, out_shapes, args, plans


def _run_casts(src_refs, dst_refs, plans):
    for src_ref, dst_ref, pieces in zip(src_refs, dst_refs, plans):
        off = 0
        for t, a, b in pieces:
            if t is None:
                dst_ref[:, off:off + b - a] = src_ref[:, a:b].astype(dst_ref.dtype)
                off += b - a
            else:
                dst_ref[t] = src_ref[:, a:b].astype(dst_ref.dtype)


def _mm_kernel(a_ref, *rest, swiglu, cast_ranges):
    nc = len(cast_ranges)
    n_w = 2 if swiglu else 1
    w_refs, src_refs = rest[:n_w], rest[n_w:n_w + nc]
    o_ref, dst_refs = rest[n_w + nc], rest[n_w + nc + 1:]
    a = a_ref[...]
    if swiglu:
        half = o_ref.shape[1] // 2
        for cs in (slice(0, half), slice(half, 2 * half)):
            g = _dot(a, w_refs[0][:, cs])
            u = _dot(a, w_refs[1][:, cs])
            o_ref[:, cs] = (_silu(g) * u).astype(o_ref.dtype)
    else:
        o_ref[...] = _dot(a, w_refs[0][...]).astype(o_ref.dtype)
    _run_casts(src_refs, dst_refs, cast_ranges)


def _weight_specs(w, tn, swiglu):
    if w.ndim == 2:
        k, nt = w.shape[0], w.shape[1] // tn
        nj = nt // 2 if swiglu else nt
        specs = [pl.BlockSpec((k, tn), lambda i, j: (0, j))]
        if swiglu:
            specs.append(pl.BlockSpec((k, tn), lambda i, j: (0, j + nj)))
    else:
        nt, k, tn = w.shape
        nj = nt // 2 if swiglu else nt
        specs = [pl.BlockSpec((None, k, tn), lambda i, j: (j, 0, 0))]
        if swiglu:
            specs.append(pl.BlockSpec((None, k, tn), lambda i, j: (j + nj, 0, 0)))
    return specs, nj, tn


def _matmul(a, w, *, swiglu, out_dtype, tm, name, tn=None, casts=()):
    R, K = a.shape
    w_specs, nj, tn = _weight_specs(w, tn, swiglu)
    n_out = nj * tn
    in_specs = [pl.BlockSpec((tm, K), lambda i, j: (i, 0))] + w_specs
    args = [a] + [w] * len(w_specs)
    c_in, c_out, c_shapes, c_args, c_ranges = _cast_plan(casts, R // tm, nj)
    res = pl.pallas_call(
        functools.partial(_mm_kernel, swiglu=swiglu, cast_ranges=c_ranges),
        grid=(R // tm, nj),
        in_specs=in_specs + c_in,
        out_specs=[pl.BlockSpec((tm, tn), lambda i, j: (i, j))] + c_out,
        out_shape=[jax.ShapeDtypeStruct((R, n_out), out_dtype)] + c_shapes,
        compiler_params=_cparams("arbitrary", "arbitrary"),
        name=name,
    )(*args, *c_args)
    return res[0], res[1:]


def _mm_res_kernel(a_ref, w_ref, *rest, gate_idx, scale, cast_ranges, n_first):
    nc = len(cast_ranges)
    if n_first is None:
        res = rest[0][...]
        rest = rest[1:]
    else:
        res = jnp.where(pl.program_id(0) < n_first, rest[0][...], rest[1][...])
        rest = rest[2:]
    mod_ref, src_refs, o_ref, dst_refs = rest[0], rest[1:1 + nc], rest[1 + nc], rest[2 + nc:]
    acc = _dot(a_ref[...], w_ref[...])
    gate = mod_ref[0, gate_idx:gate_idx + 1, :]
    o_ref[...] = res + (scale * gate) * acc
    _run_casts(src_refs, dst_refs, cast_ranges)


def _matmul_residual(a, w, res, mods, *, n_rows, seq_rows, n_batch, gate_idx, scale, tm, name, casts=()):
    K = a.shape[1]
    nt, _, tn = w.shape
    D = nt * tn
    mod_map = lambda i, j: (jnp.minimum((i * tm) // seq_rows, n_batch), 0, j)
    parts = res if isinstance(res, tuple) else (res,)
    if len(parts) == 2:
        n_first = parts[0].shape[0] // tm
        res_specs = _two_part_specs(parts, (tm, tn), n_first, lambda active, j: (jnp.where(active, j, 0),))
    else:
        n_first = None
        res_specs = [pl.BlockSpec((tm, tn), lambda i, j: (i, j))]
    c_in, c_out, c_shapes, c_args, c_ranges = _cast_plan(casts, n_rows // tm, D // tn)
    out = pl.pallas_call(
        functools.partial(_mm_res_kernel, gate_idx=gate_idx, scale=scale, cast_ranges=c_ranges, n_first=n_first),
        grid=(n_rows // tm, D // tn),
        in_specs=[pl.BlockSpec((tm, K), lambda i, j: (i, 0)),
                  pl.BlockSpec((None, K, tn), lambda i, j: (j, 0, 0))] + res_specs
                 + [pl.BlockSpec((1, N_MOD, tn), mod_map)] + c_in,
        out_specs=[pl.BlockSpec((tm, tn), lambda i, j: (i, j))] + c_out,
        out_shape=[jax.ShapeDtypeStruct((n_rows, D), F32)] + c_shapes,
        compiler_params=_cparams("arbitrary", "arbitrary"),
        name=name,
    )(a, w, *parts, mods, *c_args)
    return out[0], out[1:]


def _rmsnorm_kernel(h_ref, w_ref, o_ref):
    h = h_ref[...]
    ms = jnp.mean(h * h, axis=-1, keepdims=True)
    o_ref[...] = h * lax.rsqrt(ms + EPS) * w_ref[...]


def _final_rmsnorm(h, w, rows, tm):
    D = h.shape[1]
    return pl.pallas_call(
        _rmsnorm_kernel,
        grid=(rows // tm,),
        in_specs=[pl.BlockSpec((tm, D), lambda i: (i, 0)),
                  pl.BlockSpec((1, D), lambda i: (0, 0))],
        out_specs=pl.BlockSpec((tm, D), lambda i: (i, 0)),
        out_shape=jax.ShapeDtypeStruct((rows, D), F32),
        compiler_params=_cparams("parallel"),
        name="final_rmsnorm",
    )(h, w)


def _segment_position(tile_rows, bs_rows, seq, ctx_len):
    row0 = pl.program_id(0) * tile_rows
    in_x = row0 < bs_rows
    off = jnp.where(in_x, lax.rem(row0, seq), lax.rem(row0 - bs_rows, ctx_len))
    seglen = jnp.where(in_x, seq, ctx_len)
    return in_x, off, seglen


def _qkv_conv_kernel(cur_ref, prev_ref, next_ref, w_ref, *rest, kind, tr, bs_rows, seq, ctx_len):
    if kind == 'k':
        o_ref, ot_ref, ext_ref = rest
    else:
        o_ref, ext_ref = rest
    _, off, seglen = _segment_position(tr, bs_rows, seq, ctx_len)
    is_start = off == 0
    is_end = off + tr == seglen
    pad = SUBLANES
    half = GDN_CONV // 2
    ext_ref[0:pad, :] = jnp.where(is_start, 0.0, prev_ref[...])
    ext_ref[pad:pad + tr, :] = cur_ref[...]
    ext_ref[pad + tr:pad + tr + pad, :] = jnp.where(is_end, 0.0, next_ref[...])
    ext = ext_ref[...]
    n_ext = tr + 2 * pad
    acc = None
    for j in range(GDN_CONV):
        shifted = ext if j == half else pltpu.roll(ext, (half - j) % n_ext, axis=0)
        term = shifted[pad:pad + tr, :] * w_ref[j:j + 1, :]
        acc = term if acc is None else acc + term
    y = _silu(acc)
    tc = y.shape[1]
    if kind == 'v':
        o_ref[...] = y.astype(o_ref.dtype)
        return
    mult = HEAD_DIM ** -0.5 if kind == 'q' else 1.0
    for hd in range(tc // HEAD_DIM):
        yh = y[:, hd * HEAD_DIM:(hd + 1) * HEAD_DIM]
        inv = lax.rsqrt(jnp.sum(yh * yh, axis=-1, keepdims=True) + EPS)
        yn = yh * inv
        if kind == 'q':
            yn = yn * mult
        o_ref[:, hd * HEAD_DIM:(hd + 1) * HEAD_DIM] = yn.astype(o_ref.dtype)
        if kind == 'k':
            ot_ref[hd * HEAD_DIM:(hd + 1) * HEAD_DIM, :] = yn.T.astype(ot_ref.dtype)


def _qkv_conv(px, conv_w, *, kind, col0, width, tr, tc, bs_rows, seq, ctx_len):
    R = px.shape[0]
    cb0 = col0 // tc
    rb = tr // SUBLANES
    last = R // SUBLANES - 1
    in_specs = [pl.BlockSpec((tr, tc), lambda i, j: (i, cb0 + j)),
                pl.BlockSpec((SUBLANES, tc), lambda i, j: (jnp.maximum(i * rb - 1, 0), cb0 + j)),
                pl.BlockSpec((SUBLANES, tc), lambda i, j: (jnp.minimum((i + 1) * rb, last), cb0 + j)),
                pl.BlockSpec((GDN_CONV, tc), lambda i, j: (0, j))]
    out_specs = [pl.BlockSpec((tr, tc), lambda i, j: (i, j))]
    out_shape = [jax.ShapeDtypeStruct((R, width), BF16)]
    if kind == 'k':
        out_specs.append(pl.BlockSpec((tc, tr), lambda i, j: (j, i)))
        out_shape.append(jax.ShapeDtypeStruct((width, R), BF16))
    res = pl.pallas_call(
        functools.partial(_qkv_conv_kernel, kind=kind, tr=tr, bs_rows=bs_rows, seq=seq, ctx_len=ctx_len),
        grid=(R // tr, width // tc),
        in_specs=in_specs,
        out_specs=out_specs,
        out_shape=out_shape,
        scratch_shapes=[pltpu.VMEM((tr + 2 * SUBLANES, tc), F32)],
        compiler_params=_cparams("parallel", "parallel"),
        name="gdn_conv_" + kind,
    )(px, px, px, conv_w)
    return res


def _gate_kernel(ba_ref, alog_ref, dtb_ref, cols_ref, rows_ref, *, tm, n_pairs):
    lane = lax.broadcasted_iota(jnp.int32, (CHUNK, LANES), 1)
    is_beta = (lane & 1) == 0
    is_bwd = (lane & 4) != 0
    ri = lax.broadcasted_iota(jnp.int32, (CHUNK, CHUNK), 0)
    ci = lax.broadcasted_iota(jnp.int32, (CHUNK, CHUNK), 1)
    tri_lo = jnp.where(ci <= ri, 1.0, 0.0)
    tri_up = jnp.where(ci >= ri, 1.0, 0.0)
    neg_a = -jnp.exp(alog_ref[...])
    for c in range(tm // CHUNK):
        sl = slice(c * CHUNK, (c + 1) * CHUNK)
        x = ba_ref[sl, :]
        beta = jax.nn.sigmoid(x)
        y = x + dtb_ref[...]
        softplus = jnp.maximum(y, 0.0) + jnp.log1p(jnp.exp(-jnp.abs(y)))
        g = jnp.where(is_beta, 0.0, neg_a * softplus)
        gc = jnp.where(is_bwd, _dot_hi(tri_up, g), _dot_hi(tri_lo, g))
        full = jnp.where(is_beta, beta, gc)
        for p in range(n_pairs):
            cols_ref[p, sl, :] = full[:, p * 8:(p + 1) * 8]
        rows_ref[:, sl] = full.T[0:n_pairs * 8, :]


def _gdn_gates(ba, alog_row, dtb_row, *, n_pairs, tm):
    R = ba.shape[0]
    return pl.pallas_call(
        functools.partial(_gate_kernel, tm=tm, n_pairs=n_pairs),
        grid=(R // tm,),
        in_specs=[pl.BlockSpec((tm, LANES), lambda i: (i, 0)),
                  pl.BlockSpec((1, LANES), lambda i: (0, 0)),
                  pl.BlockSpec((1, LANES), lambda i: (0, 0))],
        out_specs=[pl.BlockSpec((n_pairs, tm, 8), lambda i: (0, i, 0)),
                   pl.BlockSpec((n_pairs * 8, tm), lambda i: (0, i))],
        out_shape=[jax.ShapeDtypeStruct((n_pairs, R, 8), F32),
                   jax.ShapeDtypeStruct((n_pairs * 8, R), F32)],
        compiler_params=_cparams("parallel"),
        name="gdn_gates",
    )(ba, alog_row, dtb_row)


def _gdn_kernel(qx_ref, qc_ref, kx_ref, kc_ref, ktx_ref, ktc_ref, vx_ref, vc_ref,
                colx_ref, colc_ref, rowx_ref, rowc_ref, zx_ref, zc_ref, nw_ref, ox_ref, oc_ref,
                t_ref, a_ref, of_ref, ob_ref, sf_ref, sb_ref, *, seq, ctx_len, rb, n_pair, prep_chunks):
    C = CHUNK
    HD = HEAD_DIM
    n_xc = seq // C
    n_cc = ctx_len // C
    x_refs = (qx_ref, kx_ref, ktx_ref, vx_ref, colx_ref, rowx_ref)
    c_refs = (qc_ref, kc_ref, ktc_ref, vc_ref, colc_ref, rowc_ref)

    ri = lax.broadcasted_iota(jnp.int32, (C, C), 0)
    ci = lax.broadcasted_iota(jnp.int32, (C, C), 1)
    eye = (ri == ci).astype(F32)

    def lane_idx(d, s, kind):
        return d * 4 + s * 2 + kind

    block_diff = ri ^ ci

    def unit_triangular_inverses(nmats):
        base = int(math.log2(INV_BLOCK))
        n0s = [jnp.where((block_diff >> base) == 0, n, 0.0) for n in nmats]
        invs = [eye + n0 for n0 in n0s]
        pws = n0s
        for _ in range(base - 1):
            pwbs = [pw.astype(BF16) for pw in pws]
            pws = [_dot(b, b) for b in pwbs]
            invs = [inv + _dot(inv.astype(BF16), pw.astype(BF16)) for inv, pw in zip(invs, pws)]
        for level in range(base, int(math.log2(C))):
            offs = [jnp.where((block_diff >> level) == 1, n, 0.0).astype(BF16) for n in nmats]
            invbs = [inv.astype(BF16) for inv in invs]
            halves = [_dot(ib, off).astype(BF16) for ib, off in zip(invbs, offs)]
            invs = [inv + _dot(h, ib) for inv, h, ib in zip(invs, halves, invbs)]
        return invs

    def prepare(chunks):
        nmats, dests = [], []
        for refs, row0, slot in chunks:
            q_ref, k_ref, kt_ref, _, col_ref, row_ref = refs
            rows = pl.ds(row0, C)
            for u in range(n_pair):
                hs = slice(u * HD, (u + 1) * HD)
                qk = jnp.concatenate([q_ref[rows, hs], k_ref[rows, hs]], axis=0)
                gram = _dot(qk, kt_ref[hs, rows])
                qkt = gram[0:C, :]
                kkt = gram[C:2 * C, :]
                cols = col_ref[u, rows, :]
                for d in range(2):
                    incl = (ci <= ri) if d == 0 else (ci >= ri)
                    strict = (ci < ri) if d == 0 else (ci > ri)
                    for s in range(2):
                        kb, kg = lane_idx(d, s, 0), lane_idx(d, s, 1)
                        beta_col = cols[:, kb:kb + 1]
                        gc_col = cols[:, kg:kg + 1]
                        gc_row = row_ref[u * 8 + kg:u * 8 + kg + 1, rows]
                        decay = jnp.where(incl, jnp.exp(jnp.where(incl, gc_col - gc_row, 0.0)), 0.0)
                        nmats.append(jnp.where(strict, -(kkt * beta_col) * decay, 0.0))
                        a_ref[slot, u * 4 + d * 2 + s] = (qkt * decay).astype(BF16)
                        dests.append((slot, u * 4 + d * 2 + s))
        for (slot, idx), inv in zip(dests, unit_triangular_inverses(nmats)):
            t_ref[slot, idx] = inv.astype(BF16)

    def chain(tasks):
        units = [(d, u, refs, pl.ds(row0, C), slot, orow0)
                 for d, refs, row0, slot, orow0 in tasks for u in range(n_pair)]
        states, projs = [], []
        for d, u, refs, rows, _, _ in units:
            hs = slice(u * HD, (u + 1) * HD)
            state = (sf_ref if d == 0 else sb_ref)[u]
            qk = jnp.concatenate([refs[0][rows, hs], refs[1][rows, hs]], axis=0)
            states.append(state)
            projs.append(_dot(qk, state.astype(BF16)))
        scalars, rhss = [], []
        for (d, u, refs, rows, _, _), proj in zip(units, projs):
            cols = refs[4][u, rows, :]
            for s in range(2):
                kb, kg = lane_idx(d, s, 0), lane_idx(d, s, 1)
                beta_col = cols[:, kb:kb + 1]
                gc_col = cols[:, kg:kg + 1]
                g_last = gc_col[C - 1:C, :] if d == 0 else gc_col[0:1, :]
                e_col = jnp.exp(gc_col)
                vs = slice((2 * u + s) * HD, (2 * u + s + 1) * HD)
                v = refs[3][rows, vs].astype(F32)
                k_s = proj[C:2 * C, s * HD:(s + 1) * HD]
                rhss.append((beta_col * (v - e_col * k_s)).astype(BF16))
                scalars.append((gc_col, e_col, g_last))
        v_news = []
        for i, (d, u, _, _, slot, _) in enumerate(units):
            for s in range(2):
                v_news.append(_dot(t_ref[slot, u * 4 + d * 2 + s], rhss[2 * i + s]))
        for i, (d, u, refs, rows, slot, orow0) in enumerate(units):
            o_acc = of_ref if d == 0 else ob_ref
            evs, gls = [], []
            for s in range(2):
                gc_col, e_col, g_last = scalars[2 * i + s]
                v_new = v_news[2 * i + s]
                q_s = projs[i][0:C, s * HD:(s + 1) * HD]
                out = e_col * q_s + _dot(a_ref[slot, u * 4 + d * 2 + s], v_new.astype(BF16))
                o_acc[pl.ds(orow0, C), (2 * u + s) * HD:(2 * u + s + 1) * HD] = out
                evs.append(jnp.exp(g_last - gc_col) * v_new)
                gls.append(jnp.broadcast_to(jnp.exp(g_last), (HD, HD)))
            ev = jnp.concatenate(evs, axis=1).astype(BF16)
            new_state = states[i] * jnp.concatenate(gls, axis=1) + _dot(refs[2][u * HD:(u + 1) * HD, rows], ev)
            (sf_ref if d == 0 else sb_ref)[u] = new_state

    sf_ref[...] = jnp.zeros_like(sf_ref)
    sb_ref[...] = jnp.zeros_like(sb_ref)
    prepare([(c_refs, n * C, n) for n in range(n_cc)])

    def prep_body(it, carry):
        n0 = it * prep_chunks
        prepare([(x_refs, pl.multiple_of((n0 + j) * C, C), n_cc + n0 + j) for j in range(prep_chunks)])
        return carry
    lax.fori_loop(0, n_xc // prep_chunks, prep_body, 0)

    for n in range(n_cc):
        m = n_cc - 1 - n
        chain([(0, c_refs, n * C, n, seq + n * C), (1, c_refs, m * C, m, seq + m * C)])

    def chain_body(n, carry):
        m = n_xc - 1 - n
        rf = pl.multiple_of(n * C, C)
        rbk = pl.multiple_of(m * C, C)
        chain([(0, x_refs, rf, n_cc + n, rf), (1, x_refs, rbk, n_cc + m, rbk)])
        return carry
    lax.fori_loop(0, n_xc, chain_body, 0)

    def finish(z_ref, out_ref, base, n_blocks):
        def body(bi, carry):
            r0 = pl.multiple_of(bi * rb, rb)
            rows = pl.ds(r0, rb)
            acc_rows = pl.ds(pl.multiple_of(base + r0, rb), rb)
            o = of_ref[acc_rows, :] + ob_ref[acc_rows, :]
            z = z_ref[rows, :]
            for hd in range(2 * n_pair):
                hs = slice(hd * HD, (hd + 1) * HD)
                oh = o[:, hs]
                ms = jnp.mean(oh * oh, axis=-1, keepdims=True)
                y = oh * lax.rsqrt(ms + EPS) * nw_ref[...]
                out_ref[rows, hs] = (y * _silu(z[:, hs])).astype(out_ref.dtype)
            return carry
        lax.fori_loop(0, n_blocks, body, 0)

    finish(zx_ref, ox_ref, 0, seq // rb)
    finish(zc_ref, oc_ref, seq, ctx_len // rb)


def _gdn(qn, kn, knt, vn, cols, rows, px, z_col0, norm_w, *, n_batch, seq, ctx_len, out_cols):
    R = qn.shape[0]
    n_pairs = qn.shape[1] // HEAD_DIM
    C = CHUNK
    rb = min(256, ctx_len)
    cblk0 = n_batch * (seq // ctx_len)
    n_slots = (seq + ctx_len) // C
    n_pair = 2 if n_pairs % 2 == 0 else 1
    prep_chunks = 2 if (seq // C) % 2 == 0 else 1
    qw = n_pair * HEAD_DIM
    vw = 2 * qw
    zb0 = z_col0 // vw
    assert z_col0 % vw == 0

    x_rows = lambda w, c0=0: pl.BlockSpec((seq, w), lambda b, p: (b, c0 + p))
    c_rows = lambda w, c0=0: pl.BlockSpec((ctx_len, w), lambda b, p: (cblk0 + b, c0 + p))
    in_specs = [
        x_rows(qw), c_rows(qw),
        x_rows(qw), c_rows(qw),
        pl.BlockSpec((qw, seq), lambda b, p: (p, b)),
        pl.BlockSpec((qw, ctx_len), lambda b, p: (p, cblk0 + b)),
        x_rows(vw), c_rows(vw),
        pl.BlockSpec((n_pair, seq, 8), lambda b, p: (p, b, 0)),
        pl.BlockSpec((n_pair, ctx_len, 8), lambda b, p: (p, cblk0 + b, 0)),
        pl.BlockSpec((n_pair * 8, seq), lambda b, p: (p, b)),
        pl.BlockSpec((n_pair * 8, ctx_len), lambda b, p: (p, cblk0 + b)),
        x_rows(vw, zb0), c_rows(vw, zb0),
        pl.BlockSpec((1, HEAD_DIM), lambda b, p: (0, 0)),
    ]
    return pl.pallas_call(
        functools.partial(_gdn_kernel, seq=seq, ctx_len=ctx_len, rb=rb, n_pair=n_pair, prep_chunks=prep_chunks),
        grid=(n_batch, n_pairs // n_pair),
        in_specs=in_specs,
        out_specs=[pl.BlockSpec((seq, vw), lambda b, p: (b, p)),
                   pl.BlockSpec((ctx_len, vw), lambda b, p: (b, p))],
        out_shape=[jax.ShapeDtypeStruct((R, out_cols), BF16),
                   jax.ShapeDtypeStruct((n_batch * ctx_len, vn.shape[1]), BF16)],
        scratch_shapes=[pltpu.VMEM((n_slots, n_pair * 4, C, C), BF16),
                        pltpu.VMEM((n_slots, n_pair * 4, C, C), BF16),
                        pltpu.VMEM((seq + ctx_len, vw), F32),
                        pltpu.VMEM((seq + ctx_len, vw), F32),
                        pltpu.VMEM((n_pair, HEAD_DIM, 2 * HEAD_DIM), F32),
                        pltpu.VMEM((n_pair, HEAD_DIM, 2 * HEAD_DIM), F32)],
        compiler_params=_cparams("parallel", "parallel"),
        name="gdn_delta_rule",
    )(qn, qn, kn, kn, knt, knt, vn, vn, cols, cols, rows, rows, px, px, norm_w)


def _copy_kernel(src_ref, dst_hbm_ref, o_ref):
    del dst_hbm_ref
    o_ref[...] = src_ref[...]


def _place_context_rows(y_ctx, ymix, *, row_blk0, blk_rows, blk_cols):
    rows, cols = y_ctx.shape
    return pl.pallas_call(
        _copy_kernel,
        grid=(rows // blk_rows, cols // blk_cols),
        in_specs=[pl.BlockSpec((blk_rows, blk_cols), lambda i, j: (i, j)),
                  pl.BlockSpec(memory_space=pl.ANY)],
        out_specs=pl.BlockSpec((blk_rows, blk_cols), lambda i, j: (row_blk0 + i, j)),
        out_shape=jax.ShapeDtypeStruct(ymix.shape, ymix.dtype),
        input_output_aliases={1: 0},
        compiler_params=_cparams("parallel", "parallel"),
        name="place_context_rows",
    )(y_ctx, ymix)


def _shortconv_kernel(x_ref, b_ref, c_ref, w_ref, ymix_ref, o_ref, *, tr, bs_rows, seq, ctx_len):
    del ymix_ref
    in_x, off, _ = _segment_position(tr, bs_rows, seq, ctx_len)
    period = jnp.where(in_x, GRID_W, ctx_len)
    pos = (off + lax.broadcasted_iota(jnp.int32, (tr, 1), 0)) & (period - 1)
    v = c_ref[...] * x_ref[...]
    left = jnp.where(pos == 0, 0.0, pltpu.roll(v, 1, axis=0))
    right = jnp.where(pos == period - 1, 0.0, pltpu.roll(v, tr - 1, axis=0))
    y = left * w_ref[0:1, :] + v * w_ref[1:2, :] + right * w_ref[2:3, :]
    o_ref[...] = (b_ref[...] * y).astype(o_ref.dtype)


def _shortconv(px, conv_w, ymix, *, x_col0, width, out_col0, tr, tc, bs_rows, seq, ctx_len):
    R = px.shape[0]
    nb = width // tc
    xb0 = x_col0 // tc
    ob0 = out_col0 // tc
    return pl.pallas_call(
        functools.partial(_shortconv_kernel, tr=tr, bs_rows=bs_rows, seq=seq, ctx_len=ctx_len),
        grid=(R // tr, nb),
        in_specs=[pl.BlockSpec((tr, tc), lambda i, j: (i, xb0 + j)),
                  pl.BlockSpec((tr, tc), lambda i, j: (i, xb0 + nb + j)),
                  pl.BlockSpec((tr, tc), lambda i, j: (i, xb0 + 2 * nb + j)),
                  pl.BlockSpec((SC_CONV, tc), lambda i, j: (0, j)),
                  pl.BlockSpec(memory_space=pl.ANY)],
        out_specs=pl.BlockSpec((tr, tc), lambda i, j: (i, ob0 + j)),
        out_shape=jax.ShapeDtypeStruct(ymix.shape, ymix.dtype),
        input_output_aliases={4: 0},
        compiler_params=_cparams("parallel", "parallel"),
        name="short_conv",
    )(px, px, px, conv_w, ymix)


def _pool_kernel(u_ref, w_ref, sc_ref, ymix_ref, o_ref, pa_ref, pb_ref, *, n_tok, unit, pg):
    del ymix_ref
    n_pos = n_tok // unit
    reach = 8 * unit
    pad = -(-reach // SUBLANES) * SUBLANES
    mg = -(-4 * unit // SUBLANES) * SUBLANES
    tp = n_tok + 2 * pad
    zeros_mg = jnp.zeros((mg, pg), F32)
    zeros_pad = jnp.zeros((pad, pg), F32)
    for buf in (pa_ref, pb_ref):
        buf[0:mg, :] = zeros_mg
        buf[mg + tp:mg + tp + mg, :] = zeros_mg
    row = lax.broadcasted_iota(jnp.int32, (n_tok, 1), 0) // unit
    for g, win in enumerate(POOL_WINDOWS):
        cs = slice(g * pg, (g + 1) * pg)
        u = u_ref[:, cs]
        pa_ref[mg:mg + pad, :] = zeros_pad
        pa_ref[mg + pad:mg + pad + n_tok, :] = u
        pa_ref[mg + pad + n_tok:mg + tp, :] = zeros_pad
        src, dst = pa_ref, pb_ref
        dst[mg:mg + tp, :] = src[mg - unit:mg - unit + tp, :] + src[mg:mg + tp, :]
        src, dst = dst, src
        step, w = 1, 2
        while w < win:
            dst[mg:mg + tp, :] = (src[mg - step * unit:mg - step * unit + tp, :]
                                  + src[mg + step * unit:mg + step * unit + tp, :])
            src, dst = dst, src
            step, w = step * 2, w * 2
        total = src[mg + pad:mg + pad + n_tok, :]
        lo = jnp.maximum(row - win // 2, 0)
        hi = jnp.minimum(row - win // 2 + win, n_pos)
        count = (hi - lo).astype(F32)
        dlt = total / count - u
        y = _dot(dlt.astype(BF16), w_ref[g]) * sc_ref[:, cs]
        o_ref[:, cs] = y.astype(o_ref.dtype)


def _pool(px, pool_w, pool_scale, ymix, *, u_col0, out_col0, n_tok, unit, blk0, n_batch, name):
    width = pool_scale.shape[1]
    pg = width // len(POOL_WINDOWS)
    reach = 8 * unit
    pad = -(-reach // SUBLANES) * SUBLANES
    mg = -(-4 * unit // SUBLANES) * SUBLANES
    buf_rows = n_tok + 2 * pad + 2 * mg
    ub = u_col0 // width
    ob = out_col0 // width
    return pl.pallas_call(
        functools.partial(_pool_kernel, n_tok=n_tok, unit=unit, pg=pg),
        grid=(n_batch,),
        in_specs=[pl.BlockSpec((n_tok, width), lambda b: (blk0 + b, ub)),
                  pl.BlockSpec(pool_w.shape, lambda b: (0, 0, 0)),
                  pl.BlockSpec((1, width), lambda b: (0, 0)),
                  pl.BlockSpec(memory_space=pl.ANY)],
        out_specs=pl.BlockSpec((n_tok, width), lambda b: (blk0 + b, ob)),
        out_shape=jax.ShapeDtypeStruct(ymix.shape, ymix.dtype),
        input_output_aliases={3: 0},
        scratch_shapes=[pltpu.VMEM((buf_rows, pg), F32), pltpu.VMEM((buf_rows, pg), F32)],
        compiler_params=_cparams("parallel"),
        name=name,
    )(px, pool_w, pool_scale, ymix)


def _ba_permutation(n_heads):
    perm = np.zeros(4 * n_heads, np.int32)
    for p in range(n_heads // 2):
        for d in range(2):
            for s in range(2):
                for kind in range(2):
                    perm[p * 8 + d * 4 + s * 2 + kind] = d * 2 * n_heads + kind * n_heads + 2 * p + s
    return perm


def _gate_param_row(param, n_heads):
    row = jnp.zeros((LANES,), F32)
    idx, src_d, src_h = [], [], []
    for p in range(n_heads // 2):
        for d in range(2):
            for s in range(2):
                idx.append(p * 8 + d * 4 + s * 2 + 1)
                src_d.append(d)
                src_h.append(2 * p + s)
    row = row.at[np.array(idx)].set(param[np.array(src_d), np.array(src_h)])
    return row.reshape(1, LANES)


def _pick_tile(n, prefs):
    for t in prefs:
        if n % t == 0:
            return t
    raise ValueError(f"no tile for {n}")


def kernel(x, c, ctx, c_ctx, w_mod, b_mod, norm_ffn1, w_ffn1_gu, w_ffn1_down, norm_mix, w_in, conv_qkv, a_log,
           dt_bias, gdn_norm, conv_short, pool_w, pool_scale, w_out, norm_ffn2, w_ffn2_gu, w_ffn2_down, norm_final):
    B, S, D = x.shape
    CL = ctx.shape[1]
    L = w_mod.shape[0]
    n_vheads = a_log.shape[2]
    qk_dim = (n_vheads // 2) * HEAD_DIM
    v_dim = n_vheads * HEAD_DIM
    sc_dim = conv_short.shape[2]
    pool_dim = pool_scale.shape[1]
    n_ba = 4 * n_vheads
    n_pairs = n_vheads // 2
    bs_rows = B * S
    R = bs_rows + B * CL
    assert S % CHUNK == 0 and CL % CHUNK == 0 and S % CL == 0 and n_ba <= LANES

    tm = _pick_tile(math.gcd(S, R), (1024, 512, 256, 128))
    tg = _pick_tile(math.gcd(S, R), (512, 256, 128))
    tr = min(256, CL)
    assert (B * CL) % tm == 0
    tn = 512 if D % 512 == 0 else 256
    tm_ff = tm // 2 if tm >= 512 else tm
    tn_ff = 512
    tn_out = 1024 if D % 1024 == 0 else tn
    tc_qk = _pick_tile(qk_dim, (1024, 512, 256, 128))
    tc_v = math.gcd(v_dim, 2 * tc_qk)

    z_col0 = 2 * qk_dim + v_dim
    ba_col0 = z_col0 + v_dim
    sc_col0_src = ba_col0 + n_ba
    sc_col0 = ba_col0
    pool_col0 = sc_col0 + 3 * sc_dim
    ba_perm = _ba_permutation(n_vheads)
    onehot = np.zeros((n_ba, LANES), np.float32)
    onehot[ba_perm, np.arange(n_ba)] = 1.0
    ba_onehot = jnp.asarray(onehot, dtype=BF16)

    h = (x.reshape(bs_rows, D), ctx.reshape(B * CL, D))
    c_all = jnp.zeros((SUBLANES, D), F32).at[0:B].set(c).at[B].set(c_ctx)
    mods_all = _modulation(c_all, w_mod, b_mod.reshape(L, 1, N_MOD * D)).reshape(L, SUBLANES, N_MOD, D)

    common = dict(seq_rows=S, n_batch=B)
    seg = dict(bs_rows=bs_rows, seq=S, ctx_len=CL)
    whole = lambda w: [(0, w.shape[2])]
    w1gu = w_ffn1_gu[0].astype(BF16)
    for l in range(L):
        n_live = bs_rows if l == L - 1 else R
        mods = mods_all[l]

        hn = _adaln(h, norm_ffn1[l].reshape(1, D), mods, n_rows=R, shift_idx=0, scale_idx=1, tr=tg,
                    name="ffn1_adaln", **common)
        act, (w1d,) = _matmul(hn, w1gu, swiglu=True, out_dtype=BF16, tm=tm, tn=tn, name="ffn1_up",
                              casts=[(w_ffn1_down, l, whole(w_ffn1_down), tn_ff)])
        h, (w_main, w_ba_src) = _matmul_residual(
            act, w1d, h, mods, n_rows=R, gate_idx=2, scale=0.5, tm=tm_ff, name="ffn1_down", **common,
            casts=[(w_in, l, [(0, ba_col0), (sc_col0_src, w_in.shape[2])], tn),
                   (w_in, l, [(ba_col0, sc_col0_src)], None)])
        w_ba = jnp.dot(w_ba_src, ba_onehot, preferred_element_type=F32).astype(BF16).reshape(1, D, LANES)

        hn = _adaln(h, norm_mix[l].reshape(1, D), mods, n_rows=R, shift_idx=3, scale_idx=4, tr=tg,
                    name="mix_adaln", **common)
        px, (w2gu, wo) = _matmul(hn, w_main, swiglu=False, out_dtype=F32, tm=tm, name="in_proj",
                                 casts=[(w_ffn2_gu, l, whole(w_ffn2_gu), tn), (w_out, l, whole(w_out), tn_out)])
        ba, _ = _matmul(hn, w_ba, swiglu=False, out_dtype=F32, tm=tm, name="in_proj_ba")

        conv = dict(tr=tr, **seg)
        (qn,) = _qkv_conv(px, conv_qkv[l][:, 0:qk_dim], kind='q', col0=0, width=qk_dim, tc=tc_qk, **conv)
        kn, knt = _qkv_conv(px, conv_qkv[l][:, qk_dim:2 * qk_dim], kind='k', col0=qk_dim, width=qk_dim,
                            tc=tc_qk, **conv)
        (vn,) = _qkv_conv(px, conv_qkv[l][:, 2 * qk_dim:], kind='v', col0=2 * qk_dim, width=v_dim,
                          tc=tc_v, **conv)
        cols, rows = _gdn_gates(ba, _gate_param_row(a_log[l], n_vheads), _gate_param_row(dt_bias[l], n_vheads),
                                n_pairs=n_pairs, tm=tg)
        ymix, y_ctx = _gdn(qn, kn, knt, vn, cols, rows, px, z_col0, gdn_norm[l].reshape(1, HEAD_DIM),
                           n_batch=B, seq=S, ctx_len=CL, out_cols=v_dim + sc_dim + pool_dim)
        if l + 1 < L:
            ymix = _place_context_rows(y_ctx, ymix, row_blk0=bs_rows // CL, blk_rows=CL,
                                       blk_cols=math.gcd(v_dim, 4 * LANES))

        ymix = _shortconv(px, conv_short[l], ymix, x_col0=sc_col0, width=sc_dim, out_col0=v_dim,
                          tr=tr, tc=_pick_tile(sc_dim, (1024, 512, 256, 128)), **seg)
        pw = pool_w[l].astype(BF16)
        psc = pool_scale[l].reshape(1, pool_dim)
        ymix = _pool(px, pw, psc, ymix, u_col0=pool_col0, out_col0=v_dim + sc_dim, n_tok=S, unit=GRID_W,
                     blk0=0, n_batch=B, name="pool_latent")
        ymix = _pool(px, pw, psc, ymix, u_col0=pool_col0, out_col0=v_dim + sc_dim, n_tok=CL, unit=1,
                     blk0=bs_rows // CL, n_batch=B, name="pool_context")
        h, _ = _matmul_residual(ymix, wo, h, mods, n_rows=n_live, gate_idx=5, scale=1.0, tm=tm_ff,
                                name="mix_out", **common)

        hn = _adaln(h, norm_ffn2[l].reshape(1, D), mods, n_rows=n_live, shift_idx=6, scale_idx=7, tr=tg,
                    name="ffn2_adaln", **common)
        act, (w2d,) = _matmul(hn, w2gu, swiglu=True, out_dtype=BF16, tm=tm, name="ffn2_up",
                              casts=[(w_ffn2_down, l, whole(w_ffn2_down), tn_ff)])
        next_casts = [(w_ffn1_gu, l + 1, whole(w_ffn1_gu), tn)] if l + 1 < L else []
        h, nxt = _matmul_residual(act, w2d, h, mods, n_rows=n_live, gate_idx=8, scale=0.5, tm=tm_ff,
                                  name="ffn2_down", **common, casts=next_casts)
        if nxt:
            w1gu = nxt[0]

    out = _final_rmsnorm(h, norm_final.reshape(1, D), bs_rows, tg)
    return out.reshape(B, S, D)
```

```python
import functools
import math

import numpy as np
import jax
import jax.numpy as jnp
from jax import lax
from jax.experimental import pallas as pl
from jax.experimental.pallas import tpu as pltpu

EPS = 1e-6
N_MOD = 9
GRID_W = 64
HEAD_DIM = 128
GDN_CONV = 5
SC_CONV = 3
POOL_WINDOWS = (2, 4, 8, 16)
CHUNK = 128
INV_BLOCK = 16
LANES = 128
SUBLANES = 8
BF16_SUBLANES = 16
VMEM_LIMIT = 56 * 1024 * 1024

BF16 = jnp.bfloat16
F32 = jnp.float32


def _cparams(*sem):
    return pltpu.CompilerParams(dimension_semantics=sem, vmem_limit_bytes=VMEM_LIMIT)


def _dot(a, b):
    return jnp.dot(a, b, preferred_element_type=F32)


def _dot_hi(a, b):
    return jnp.dot(a, b, preferred_element_type=F32, precision=lax.Precision.HIGHEST)


def _silu(v):
    return v * jax.nn.sigmoid(v)


def _mod_kernel(c_ref, w_ref, b_ref, o_ref):
    a = _silu(c_ref[...]).astype(BF16)
    o_ref[...] = _dot(a, w_ref[...].astype(BF16)) + b_ref[...]


def _modulation(c_all, w_mod, b_mod, tn=512):
    L, D, N = w_mod.shape
    return pl.pallas_call(
        _mod_kernel,
        grid=(L, N // tn),
        in_specs=[pl.BlockSpec((SUBLANES, D), lambda l, j: (0, 0)),
                  pl.BlockSpec((None, D, tn), lambda l, j: (l, 0, j)),
                  pl.BlockSpec((None, 1, tn), lambda l, j: (l, 0, j))],
        out_specs=pl.BlockSpec((None, SUBLANES, tn), lambda l, j: (l, 0, j)),
        out_shape=jax.ShapeDtypeStruct((L, SUBLANES, N), F32),
        compiler_params=_cparams("parallel", "parallel"),
        name="modulation",
    )(c_all, w_mod, b_mod)


def _two_part_specs(parts, block, n_first, col_of):
    first = pl.BlockSpec(block, lambda i, *j: (jnp.minimum(i, n_first - 1), *col_of(i < n_first, *j)))
    second = pl.BlockSpec(block, lambda i, *j: (jnp.maximum(i - n_first, 0), *col_of(i >= n_first, *j)))
    return [first, second]


def _adaln_kernel(*refs, shift_idx, scale_idx, n_first):
    if n_first is None:
        h_ref, nw_ref, mod_ref, o_ref = refs
        h = h_ref[...]
    else:
        x_ref, c_ref, nw_ref, mod_ref, o_ref = refs
        h = jnp.where(pl.program_id(0) < n_first, x_ref[...], c_ref[...])
    ms = jnp.mean(h * h, axis=-1, keepdims=True)
    y = h * lax.rsqrt(ms + EPS) * nw_ref[...]
    shift = mod_ref[0, shift_idx:shift_idx + 1, :]
    scale = mod_ref[0, scale_idx:scale_idx + 1, :]
    o_ref[...] = (y * (1.0 + scale) + shift).astype(o_ref.dtype)


def _adaln(h, norm_w, mods, *, n_rows, seq_rows, n_batch, shift_idx, scale_idx, tr, name):
    parts = h if isinstance(h, tuple) else (h,)
    D = parts[0].shape[1]
    mod_map = lambda i: (jnp.minimum((i * tr) // seq_rows, n_batch), 0, 0)
    if len(parts) == 2:
        n_first = parts[0].shape[0] // tr
        h_specs = _two_part_specs(parts, (tr, D), n_first, lambda active: (0,))
    else:
        n_first = None
        h_specs = [pl.BlockSpec((tr, D), lambda i: (i, 0))]
    return pl.pallas_call(
        functools.partial(_adaln_kernel, shift_idx=shift_idx, scale_idx=scale_idx, n_first=n_first),
        grid=(n_rows // tr,),
        in_specs=h_specs + [pl.BlockSpec((1, D), lambda i: (0, 0)),
                            pl.BlockSpec((1, N_MOD, D), mod_map)],
        out_specs=pl.BlockSpec((tr, D), lambda i: (i, 0)),
        out_shape=jax.ShapeDtypeStruct((n_rows, D), BF16),
        compiler_params=_cparams("parallel"),
        name=name,
    )(*parts, norm_w, mods)


def _cast_row_block(rows, steps):
    for rps in range(BF16_SUBLANES, rows + 1, BF16_SUBLANES):
        if rows % rps == 0 and rows // rps <= steps:
            return rps
    raise ValueError(f"cannot spread {rows} rows over {steps} steps")


def _cast_plan(casts, n_i, nj):
    in_specs, out_specs, out_shapes, args, plans = [], [], [], [], []
    for src, layer, ranges, tile in casts:
        _, rows, cols = src.shape
        rps = _cast_row_block(rows, n_i * nj)
        last = rows // rps - 1
        out_cols = sum(b - a for a, b in ranges)
        src_map = lambda i, j, last=last, layer=layer: (layer, jnp.minimum(i * nj + j, last), 0)
        in_specs.append(pl.BlockSpec((None, rps, cols), src_map))
        if tile is None:
            out_specs.append(pl.BlockSpec((rps, out_cols), lambda i, j, last=last: (jnp.minimum(i * nj + j, last), 0)))
            out_shapes.append(jax.ShapeDtypeStruct((rows, out_cols), BF16))
            pieces = tuple((None, a, b) for a, b in ranges)
        else:
            assert all((b - a) % tile == 0 for a, b in ranges)
            nt = out_cols // tile
            out_specs.append(pl.BlockSpec((nt, rps, tile),
                                          lambda i, j, last=last: (0, jnp.minimum(i * nj + j, last), 0)))
            out_shapes.append(jax.ShapeDtypeStruct((nt, rows, tile), BF16))
            pieces = tuple((None, s, s + tile) for a, b in ranges for s in range(a, b, tile))
            pieces = tuple((t, a, b) for t, (_, a, b) in enumerate(pieces))
        args.append(src)
        plans.append(pieces)
    return in_specs, out_specs, out_shapes, args, plans


def _run_casts(src_refs, dst_refs, plans):
    for src_ref, dst_ref, pieces in zip(src_refs, dst_refs, plans):
        off = 0
        for t, a, b in pieces:
            if t is None:
                dst_ref[:, off:off + b - a] = src_ref[:, a:b].astype(dst_ref.dtype)
                off += b - a
            else:
                dst_ref[t] = src_ref[:, a:b].astype(dst_ref.dtype)


def _row_tile(a_hbm, buf_ref, sem_ref):
    i, j = pl.program_id(0), pl.program_id(1)
    tm = buf_ref.shape[1]
    slot = lax.rem(i, 2)

    def fetch(tile, s):
        rows = pl.ds(pl.multiple_of(tile * tm, tm), tm)
        return pltpu.make_async_copy(a_hbm.at[rows, :], buf_ref.at[s], sem_ref.at[s])

    @pl.when(jnp.logical_and(i == 0, j == 0))
    def _():
        fetch(0, 0).start()

    @pl.when(j == 0)
    def _():
        fetch(i, slot).wait()

        @pl.when(i + 1 < pl.num_programs(0))
        def _():
            fetch(i + 1, 1 - slot).start()

    return buf_ref[slot]


def _row_tile_scratch(tm, k):
    return [pltpu.VMEM((2, tm, k), BF16), pltpu.SemaphoreType.DMA((2,))]


def _mm_kernel(a_hbm, *rest, swiglu, cast_ranges):
    nc = len(cast_ranges)
    n_w = 2 if swiglu else 1
    w_refs, src_refs = rest[:n_w], rest[n_w:n_w + nc]
    o_ref, dst_refs = rest[n_w + nc], rest[n_w + nc + 1:-2]
    a = _row_tile(a_hbm, *rest[-2:])
    if swiglu:
        half = o_ref.shape[1] // 2
        for cs in (slice(0, half), slice(half, 2 * half)):
            g = _dot(a, w_refs[0][:, cs])
            u = _dot(a, w_refs[1][:, cs])
            o_ref[:, cs] = (_silu(g) * u).astype(o_ref.dtype)
    else:
        o_ref[...] = _dot(a, w_refs[0][...]).astype(o_ref.dtype)
    _run_casts(src_refs, dst_refs, cast_ranges)


def _weight_specs(w, tn, swiglu):
    if w.ndim == 2:
        k, nt = w.shape[0], w.shape[1] // tn
        nj = nt // 2 if swiglu else nt
        specs = [pl.BlockSpec((k, tn), lambda i, j: (0, j))]
        if swiglu:
            specs.append(pl.BlockSpec((k, tn), lambda i, j: (0, j + nj)))
    else:
        nt, k, tn = w.shape
        nj = nt // 2 if swiglu else nt
        specs = [pl.BlockSpec((None, k, tn), lambda i, j: (j, 0, 0))]
        if swiglu:
            specs.append(pl.BlockSpec((None, k, tn), lambda i, j: (j + nj, 0, 0)))
    return specs, nj, tn


def _matmul(a, w, *, swiglu, out_dtype, tm, name, tn=None, casts=()):
    R, K = a.shape
    w_specs, nj, tn = _weight_specs(w, tn, swiglu)
    n_out = nj * tn
    in_specs = [pl.BlockSpec(memory_space=pl.ANY)] + w_specs
    args = [a] + [w] * len(w_specs)
    c_in, c_out, c_shapes, c_args, c_ranges = _cast_plan(casts, R // tm, nj)
    res = pl.pallas_call(
        functools.partial(_mm_kernel, swiglu=swiglu, cast_ranges=c_ranges),
        grid=(R // tm, nj),
        in_specs=in_specs + c_in,
        out_specs=[pl.BlockSpec((tm, tn), lambda i, j: (i, j))] + c_out,
        out_shape=[jax.ShapeDtypeStruct((R, n_out), out_dtype)] + c_shapes,
        scratch_shapes=_row_tile_scratch(tm, K),
        compiler_params=_cparams("arbitrary", "arbitrary"),
        name=name,
    )(*args, *c_args)
    return res[0], res[1:]


def _mm_res_kernel(a_hbm, w_ref, *rest, gate_idx, scale, cast_ranges, n_first):
    nc = len(cast_ranges)
    a = _row_tile(a_hbm, *rest[-2:])
    rest = rest[:-2]
    if n_first is None:
        res = rest[0][...]
        rest = rest[1:]
    else:
        res = jnp.where(pl.program_id(0) < n_first, rest[0][...], rest[1][...])
        rest = rest[2:]
    mod_ref, src_refs, o_ref, dst_refs = rest[0], rest[1:1 + nc], rest[1 + nc], rest[2 + nc:]
    acc = _dot(a, w_ref[...])
    gate = mod_ref[0, gate_idx:gate_idx + 1, :]
    o_ref[...] = res + (scale * gate) * acc
    _run_casts(src_refs, dst_refs, cast_ranges)


def _matmul_residual(a, w, res, mods, *, n_rows, seq_rows, n_batch, gate_idx, scale, tm, name, casts=()):
    K = a.shape[1]
    nt, _, tn = w.shape
    D = nt * tn
    mod_map = lambda i, j: (jnp.minimum((i * tm) // seq_rows, n_batch), 0, j)
    parts = res if isinstance(res, tuple) else (res,)
    if len(parts) == 2:
        n_first = parts[0].shape[0] // tm
        res_specs = _two_part_specs(parts, (tm, tn), n_first, lambda active, j: (jnp.where(active, j, 0),))
    else:
        n_first = None
        res_specs = [pl.BlockSpec((tm, tn), lambda i, j: (i, j))]
    c_in, c_out, c_shapes, c_args, c_ranges = _cast_plan(casts, n_rows // tm, D // tn)
    out = pl.pallas_call(
        functools.partial(_mm_res_kernel, gate_idx=gate_idx, scale=scale, cast_ranges=c_ranges, n_first=n_first),
        grid=(n_rows // tm, D // tn),
        in_specs=[pl.BlockSpec(memory_space=pl.ANY),
                  pl.BlockSpec((None, K, tn), lambda i, j: (j, 0, 0))] + res_specs
                 + [pl.BlockSpec((1, N_MOD, tn), mod_map)] + c_in,
        out_specs=[pl.BlockSpec((tm, tn), lambda i, j: (i, j))] + c_out,
        out_shape=[jax.ShapeDtypeStruct((n_rows, D), F32)] + c_shapes,
        scratch_shapes=_row_tile_scratch(tm, K),
        compiler_params=_cparams("arbitrary", "arbitrary"),
        name=name,
    )(a, w, *parts, mods, *c_args)
    return out[0], out[1:]


def _rmsnorm_kernel(h_ref, w_ref, o_ref):
    h = h_ref[...]
    ms = jnp.mean(h * h, axis=-1, keepdims=True)
    o_ref[...] = h * lax.rsqrt(ms + EPS) * w_ref[...]


def _final_rmsnorm(h, w, rows, tm):
    D = h.shape[1]
    return pl.pallas_call(
        _rmsnorm_kernel,
        grid=(rows // tm,),
        in_specs=[pl.BlockSpec((tm, D), lambda i: (i, 0)),
                  pl.BlockSpec((1, D), lambda i: (0, 0))],
        out_specs=pl.BlockSpec((tm, D), lambda i: (i, 0)),
        out_shape=jax.ShapeDtypeStruct((rows, D), F32),
        compiler_params=_cparams("parallel"),
        name="final_rmsnorm",
    )(h, w)


def _segment_position(tile_rows, bs_rows, seq, ctx_len):
    row0 = pl.program_id(0) * tile_rows
    in_x = row0 < bs_rows
    off = jnp.where(in_x, lax.rem(row0, seq), lax.rem(row0 - bs_rows, ctx_len))
    seglen = jnp.where(in_x, seq, ctx_len)
    return in_x, off, seglen


def _qkv_conv_kernel(cur_ref, prev_ref, next_ref, w_ref, *rest, kind, tr, bs_rows, seq, ctx_len):
    if kind == 'k':
        o_ref, ot_ref, ext_ref = rest
    else:
        o_ref, ext_ref = rest
    _, off, seglen = _segment_position(tr, bs_rows, seq, ctx_len)
    is_start = off == 0
    is_end = off + tr == seglen
    pad = SUBLANES
    half = GDN_CONV // 2
    ext_ref[0:pad, :] = jnp.where(is_start, 0.0, prev_ref[...])
    ext_ref[pad:pad + tr, :] = cur_ref[...]
    ext_ref[pad + tr:pad + tr + pad, :] = jnp.where(is_end, 0.0, next_ref[...])
    ext = ext_ref[...]
    n_ext = tr + 2 * pad
    acc = None
    for j in range(GDN_CONV):
        shifted = ext if j == half else pltpu.roll(ext, (half - j) % n_ext, axis=0)
        term = shifted[pad:pad + tr, :] * w_ref[j:j + 1, :]
        acc = term if acc is None else acc + term
    y = _silu(acc)
    tc = y.shape[1]
    if kind == 'v':
        o_ref[...] = y.astype(o_ref.dtype)
        return
    mult = HEAD_DIM ** -0.5 if kind == 'q' else 1.0
    for hd in range(tc // HEAD_DIM):
        yh = y[:, hd * HEAD_DIM:(hd + 1) * HEAD_DIM]
        inv = lax.rsqrt(jnp.sum(yh * yh, axis=-1, keepdims=True) + EPS)
        yn = yh * inv
        if kind == 'q':
            yn = yn * mult
        o_ref[:, hd * HEAD_DIM:(hd + 1) * HEAD_DIM] = yn.astype(o_ref.dtype)
        if kind == 'k':
            ot_ref[hd * HEAD_DIM:(hd + 1) * HEAD_DIM, :] = yn.T.astype(ot_ref.dtype)


def _qkv_conv(px, conv_w, *, kind, col0, width, tr, tc, bs_rows, seq, ctx_len):
    R = px.shape[0]
    cb0 = col0 // tc
    rb = tr // SUBLANES
    last = R // SUBLANES - 1
    in_specs = [pl.BlockSpec((tr, tc), lambda i, j: (i, cb0 + j)),
                pl.BlockSpec((SUBLANES, tc), lambda i, j: (jnp.maximum(i * rb - 1, 0), cb0 + j)),
                pl.BlockSpec((SUBLANES, tc), lambda i, j: (jnp.minimum((i + 1) * rb, last), cb0 + j)),
                pl.BlockSpec((GDN_CONV, tc), lambda i, j: (0, j))]
    out_specs = [pl.BlockSpec((tr, tc), lambda i, j: (i, j))]
    out_shape = [jax.ShapeDtypeStruct((R, width), BF16)]
    if kind == 'k':
        out_specs.append(pl.BlockSpec((tc, tr), lambda i, j: (j, i)))
        out_shape.append(jax.ShapeDtypeStruct((width, R), BF16))
    res = pl.pallas_call(
        functools.partial(_qkv_conv_kernel, kind=kind, tr=tr, bs_rows=bs_rows, seq=seq, ctx_len=ctx_len),
        grid=(R // tr, width // tc),
        in_specs=in_specs,
        out_specs=out_specs,
        out_shape=out_shape,
        scratch_shapes=[pltpu.VMEM((tr + 2 * SUBLANES, tc), F32)],
        compiler_params=_cparams("parallel", "parallel"),
        name="gdn_conv_" + kind,
    )(px, px, px, conv_w)
    return res


def _gate_kernel(ba_ref, alog_ref, dtb_ref, cols_ref, rows_ref, *, tm, n_pairs):
    lane = lax.broadcasted_iota(jnp.int32, (CHUNK, LANES), 1)
    is_beta = (lane & 1) == 0
    is_bwd = (lane & 4) != 0
    ri = lax.broadcasted_iota(jnp.int32, (CHUNK, CHUNK), 0)
    ci = lax.broadcasted_iota(jnp.int32, (CHUNK, CHUNK), 1)
    tri_lo = jnp.where(ci <= ri, 1.0, 0.0)
    tri_up = jnp.where(ci >= ri, 1.0, 0.0)
    neg_a = -jnp.exp(alog_ref[...])
    for c in range(tm // CHUNK):
        sl = slice(c * CHUNK, (c + 1) * CHUNK)
        x = ba_ref[sl, :]
        beta = jax.nn.sigmoid(x)
        y = x + dtb_ref[...]
        softplus = jnp.maximum(y, 0.0) + jnp.log1p(jnp.exp(-jnp.abs(y)))
        g = jnp.where(is_beta, 0.0, neg_a * softplus)
        gc = jnp.where(is_bwd, _dot_hi(tri_up, g), _dot_hi(tri_lo, g))
        full = jnp.where(is_beta, beta, gc)
        for p in range(n_pairs):
            cols_ref[p, sl, :] = full[:, p * 8:(p + 1) * 8]
        rows_ref[:, sl] = full.T[0:n_pairs * 8, :]


def _gdn_gates(ba, alog_row, dtb_row, *, n_pairs, tm):
    R = ba.shape[0]
    return pl.pallas_call(
        functools.partial(_gate_kernel, tm=tm, n_pairs=n_pairs),
        grid=(R // tm,),
        in_specs=[pl.BlockSpec((tm, LANES), lambda i: (i, 0)),
                  pl.BlockSpec((1, LANES), lambda i: (0, 0)),
                  pl.BlockSpec((1, LANES), lambda i: (0, 0))],
        out_specs=[pl.BlockSpec((n_pairs, tm, 8), lambda i: (0, i, 0)),
                   pl.BlockSpec((n_pairs * 8, tm), lambda i: (0, i))],
        out_shape=[jax.ShapeDtypeStruct((n_pairs, R, 8), F32),
                   jax.ShapeDtypeStruct((n_pairs * 8, R), F32)],
        compiler_params=_cparams("parallel"),
        name="gdn_gates",
    )(ba, alog_row, dtb_row)


def _gdn_kernel(qx_ref, qc_ref, kx_ref, kc_ref, ktx_ref, ktc_ref, vx_ref, vc_ref,
                colx_ref, colc_ref, rowx_ref, rowc_ref, zx_ref, zc_ref, nw_ref, ox_ref, oc_ref,
                t_ref, a_ref, of_ref, ob_ref, sf_ref, sb_ref, *, seq, ctx_len, rb, n_pair, prep_chunks):
    C = CHUNK
    HD = HEAD_DIM
    n_xc = seq // C
    n_cc = ctx_len // C
    x_refs = (qx_ref, kx_ref, ktx_ref, vx_ref, colx_ref, rowx_ref)
    c_refs = (qc_ref, kc_ref, ktc_ref, vc_ref, colc_ref, rowc_ref)

    ri = lax.broadcasted_iota(jnp.int32, (C, C), 0)
    ci = lax.broadcasted_iota(jnp.int32, (C, C), 1)
    eye = (ri == ci).astype(F32)

    def lane_idx(d, s, kind):
        return d * 4 + s * 2 + kind

    block_diff = ri ^ ci

    def unit_triangular_inverses(nmats):
        base = int(math.log2(INV_BLOCK))
        n0s = [jnp.where((block_diff >> base) == 0, n, 0.0) for n in nmats]
        invs = [eye + n0 for n0 in n0s]
        pws = n0s
        for _ in range(base - 1):
            pwbs = [pw.astype(BF16) for pw in pws]
            pws = [_dot(b, b) for b in pwbs]
            invs = [inv + _dot(inv.astype(BF16), pw.astype(BF16)) for inv, pw in zip(invs, pws)]
        for level in range(base, int(math.log2(C))):
            offs = [jnp.where((block_diff >> level) == 1, n, 0.0).astype(BF16) for n in nmats]
            invbs = [inv.astype(BF16) for inv in invs]
            halves = [_dot(ib, off).astype(BF16) for ib, off in zip(invbs, offs)]
            invs = [inv + _dot(h, ib) for inv, h, ib in zip(invs, halves, invbs)]
        return invs

    def prepare(chunks):
        nmats, dests = [], []
        for refs, row0, slot in chunks:
            q_ref, k_ref, kt_ref, _, col_ref, row_ref = refs
            rows = pl.ds(row0, C)
            for u in range(n_pair):
                hs = slice(u * HD, (u + 1) * HD)
                qk = jnp.concatenate([q_ref[rows, hs], k_ref[rows, hs]], axis=0)
                gram = _dot(qk, kt_ref[hs, rows])
                qkt = gram[0:C, :]
                kkt = gram[C:2 * C, :]
                cols = col_ref[u, rows, :]
                for d in range(2):
                    incl = (ci <= ri) if d == 0 else (ci >= ri)
                    strict = (ci < ri) if d == 0 else (ci > ri)
                    for s in range(2):
                        kb, kg = lane_idx(d, s, 0), lane_idx(d, s, 1)
                        beta_col = cols[:, kb:kb + 1]
                        gc_col = cols[:, kg:kg + 1]
                        gc_row = row_ref[u * 8 + kg:u * 8 + kg + 1, rows]
                        decay = jnp.where(incl, jnp.exp(jnp.where(incl, gc_col - gc_row, 0.0)), 0.0)
                        nmats.append(jnp.where(strict, -(kkt * beta_col) * decay, 0.0))
                        a_ref[slot, u * 4 + d * 2 + s] = (qkt * decay).astype(BF16)
                        dests.append((slot, u * 4 + d * 2 + s))
        for (slot, idx), inv in zip(dests, unit_triangular_inverses(nmats)):
            t_ref[slot, idx] = inv.astype(BF16)

    def chain(tasks):
        units = [(d, u, refs, pl.ds(row0, C), slot, orow0)
                 for d, refs, row0, slot, orow0 in tasks for u in range(n_pair)]
        states, projs = [], []
        for d, u, refs, rows, _, _ in units:
            hs = slice(u * HD, (u + 1) * HD)
            state = (sf_ref if d == 0 else sb_ref)[u]
            qk = jnp.concatenate([refs[0][rows, hs], refs[1][rows, hs]], axis=0)
            states.append(state)
            projs.append(_dot(qk, state.astype(BF16)))
        scalars, rhss = [], []
        for (d, u, refs, rows, _, _), proj in zip(units, projs):
            cols = refs[4][u, rows, :]
            for s in range(2):
                kb, kg = lane_idx(d, s, 0), lane_idx(d, s, 1)
                beta_col = cols[:, kb:kb + 1]
                gc_col = cols[:, kg:kg + 1]
                g_last = gc_col[C - 1:C, :] if d == 0 else gc_col[0:1, :]
                e_col = jnp.exp(gc_col)
                vs = slice((2 * u + s) * HD, (2 * u + s + 1) * HD)
                v = refs[3][rows, vs].astype(F32)
                k_s = proj[C:2 * C, s * HD:(s + 1) * HD]
                rhss.append((beta_col * (v - e_col * k_s)).astype(BF16))
                scalars.append((gc_col, e_col, g_last))
        v_news = []
        for i, (d, u, _, _, slot, _) in enumerate(units):
            for s in range(2):
                v_news.append(_dot(t_ref[slot, u * 4 + d * 2 + s], rhss[2 * i + s]))
        for i, (d, u, refs, rows, slot, orow0) in enumerate(units):
            o_acc = of_ref if d == 0 else ob_ref
            evs, gls = [], []
            for s in range(2):
                gc_col, e_col, g_last = scalars[2 * i + s]
                v_new = v_news[2 * i + s]
                q_s = projs[i][0:C, s * HD:(s + 1) * HD]
                out = e_col * q_s + _dot(a_ref[slot, u * 4 + d * 2 + s], v_new.astype(BF16))
                o_acc[pl.ds(orow0, C), (2 * u + s) * HD:(2 * u + s + 1) * HD] = out
                evs.append(jnp.exp(g_last - gc_col) * v_new)
                gls.append(jnp.broadcast_to(jnp.exp(g_last), (HD, HD)))
            ev = jnp.concatenate(evs, axis=1).astype(BF16)
            new_state = states[i] * jnp.concatenate(gls, axis=1) + _dot(refs[2][u * HD:(u + 1) * HD, rows], ev)
            (sf_ref if d == 0 else sb_ref)[u] = new_state

    sf_ref[...] = jnp.zeros_like(sf_ref)
    sb_ref[...] = jnp.zeros_like(sb_ref)
    prepare([(c_refs, n * C, n) for n in range(n_cc)])

    def prep_body(it, carry):
        n0 = it * prep_chunks
        prepare([(x_refs, pl.multiple_of((n0 + j) * C, C), n_cc + n0 + j) for j in range(prep_chunks)])
        return carry
    lax.fori_loop(0, n_xc // prep_chunks, prep_body, 0)

    for n in range(n_cc):
        m = n_cc - 1 - n
        chain([(0, c_refs, n * C, n, seq + n * C), (1, c_refs, m * C, m, seq + m * C)])

    def chain_body(n, carry):
        m = n_xc - 1 - n
        rf = pl.multiple_of(n * C, C)
        rbk = pl.multiple_of(m * C, C)
        chain([(0, x_refs, rf, n_cc + n, rf), (1, x_refs, rbk, n_cc + m, rbk)])
        return carry
    lax.fori_loop(0, n_xc, chain_body, 0)

    def finish(z_ref, out_ref, base, n_blocks):
        def body(bi, carry):
            r0 = pl.multiple_of(bi * rb, rb)
            rows = pl.ds(r0, rb)
            acc_rows = pl.ds(pl.multiple_of(base + r0, rb), rb)
            o = of_ref[acc_rows, :] + ob_ref[acc_rows, :]
            z = z_ref[rows, :]
            for hd in range(2 * n_pair):
                hs = slice(hd * HD, (hd + 1) * HD)
                oh = o[:, hs]
                ms = jnp.mean(oh * oh, axis=-1, keepdims=True)
                y = oh * lax.rsqrt(ms + EPS) * nw_ref[...]
                out_ref[rows, hs] = (y * _silu(z[:, hs])).astype(out_ref.dtype)
            return carry
        lax.fori_loop(0, n_blocks, body, 0)

    finish(zx_ref, ox_ref, 0, seq // rb)
    finish(zc_ref, oc_ref, seq, ctx_len // rb)


def _gdn(qn, kn, knt, vn, cols, rows, px, z_col0, norm_w, *, n_batch, seq, ctx_len, out_cols):
    R = qn.shape[0]
    n_pairs = qn.shape[1] // HEAD_DIM
    C = CHUNK
    rb = min(256, ctx_len)
    cblk0 = n_batch * (seq // ctx_len)
    n_slots = (seq + ctx_len) // C
    n_pair = 2 if n_pairs % 2 == 0 else 1
    prep_chunks = 2 if (seq // C) % 2 == 0 else 1
    qw = n_pair * HEAD_DIM
    vw = 2 * qw
    zb0 = z_col0 // vw
    assert z_col0 % vw == 0

    x_rows = lambda w, c0=0: pl.BlockSpec((seq, w), lambda b, p: (b, c0 + p))
    c_rows = lambda w, c0=0: pl.BlockSpec((ctx_len, w), lambda b, p: (cblk0 + b, c0 + p))
    in_specs = [
        x_rows(qw), c_rows(qw),
        x_rows(qw), c_rows(qw),
        pl.BlockSpec((qw, seq), lambda b, p: (p, b)),
        pl.BlockSpec((qw, ctx_len), lambda b, p: (p, cblk0 + b)),
        x_rows(vw), c_rows(vw),
        pl.BlockSpec((n_pair, seq, 8), lambda b, p: (p, b, 0)),
        pl.BlockSpec((n_pair, ctx_len, 8), lambda b, p: (p, cblk0 + b, 0)),
        pl.BlockSpec((n_pair * 8, seq), lambda b, p: (p, b)),
        pl.BlockSpec((n_pair * 8, ctx_len), lambda b, p: (p, cblk0 + b)),
        x_rows(vw, zb0), c_rows(vw, zb0),
        pl.BlockSpec((1, HEAD_DIM), lambda b, p: (0, 0)),
    ]
    return pl.pallas_call(
        functools.partial(_gdn_kernel, seq=seq, ctx_len=ctx_len, rb=rb, n_pair=n_pair, prep_chunks=prep_chunks),
        grid=(n_batch, n_pairs // n_pair),
        in_specs=in_specs,
        out_specs=[pl.BlockSpec((seq, vw), lambda b, p: (b, p)),
                   pl.BlockSpec((ctx_len, vw), lambda b, p: (b, p))],
        out_shape=[jax.ShapeDtypeStruct((R, out_cols), BF16),
                   jax.ShapeDtypeStruct((n_batch * ctx_len, vn.shape[1]), BF16)],
        scratch_shapes=[pltpu.VMEM((n_slots, n_pair * 4, C, C), BF16),
                        pltpu.VMEM((n_slots, n_pair * 4, C, C), BF16),
                        pltpu.VMEM((seq + ctx_len, vw), F32),
                        pltpu.VMEM((seq + ctx_len, vw), F32),
                        pltpu.VMEM((n_pair, HEAD_DIM, 2 * HEAD_DIM), F32),
                        pltpu.VMEM((n_pair, HEAD_DIM, 2 * HEAD_DIM), F32)],
        compiler_params=_cparams("parallel", "parallel"),
        name="gdn_delta_rule",
    )(qn, qn, kn, kn, knt, knt, vn, vn, cols, cols, rows, rows, px, px, norm_w)


def _copy_kernel(src_ref, dst_hbm_ref, o_ref):
    del dst_hbm_ref
    o_ref[...] = src_ref[...]


def _place_context_rows(y_ctx, ymix, *, row_blk0, blk_rows, blk_cols):
    rows, cols = y_ctx.shape
    return pl.pallas_call(
        _copy_kernel,
        grid=(rows // blk_rows, cols // blk_cols),
        in_specs=[pl.BlockSpec((blk_rows, blk_cols), lambda i, j: (i, j)),
                  pl.BlockSpec(memory_space=pl.ANY)],
        out_specs=pl.BlockSpec((blk_rows, blk_cols), lambda i, j: (row_blk0 + i, j)),
        out_shape=jax.ShapeDtypeStruct(ymix.shape, ymix.dtype),
        input_output_aliases={1: 0},
        compiler_params=_cparams("parallel", "parallel"),
        name="place_context_rows",
    )(y_ctx, ymix)


def _shortconv_kernel(x_ref, b_ref, c_ref, w_ref, ymix_ref, o_ref, *, tr, bs_rows, seq, ctx_len):
    del ymix_ref
    in_x, off, _ = _segment_position(tr, bs_rows, seq, ctx_len)
    period = jnp.where(in_x, GRID_W, ctx_len)
    pos = (off + lax.broadcasted_iota(jnp.int32, (tr, 1), 0)) & (period - 1)
    v = c_ref[...] * x_ref[...]
    left = jnp.where(pos == 0, 0.0, pltpu.roll(v, 1, axis=0))
    right = jnp.where(pos == period - 1, 0.0, pltpu.roll(v, tr - 1, axis=0))
    y = left * w_ref[0:1, :] + v * w_ref[1:2, :] + right * w_ref[2:3, :]
    o_ref[...] = (b_ref[...] * y).astype(o_ref.dtype)


def _shortconv(px, conv_w, ymix, *, x_col0, width, out_col0, tr, tc, bs_rows, seq, ctx_len):
    R = px.shape[0]
    nb = width // tc
    xb0 = x_col0 // tc
    ob0 = out_col0 // tc
    return pl.pallas_call(
        functools.partial(_shortconv_kernel, tr=tr, bs_rows=bs_rows, seq=seq, ctx_len=ctx_len),
        grid=(R // tr, nb),
        in_specs=[pl.BlockSpec((tr, tc), lambda i, j: (i, xb0 + j)),
                  pl.BlockSpec((tr, tc), lambda i, j: (i, xb0 + nb + j)),
                  pl.BlockSpec((tr, tc), lambda i, j: (i, xb0 + 2 * nb + j)),
                  pl.BlockSpec((SC_CONV, tc), lambda i, j: (0, j)),
                  pl.BlockSpec(memory_space=pl.ANY)],
        out_specs=pl.BlockSpec((tr, tc), lambda i, j: (i, ob0 + j)),
        out_shape=jax.ShapeDtypeStruct(ymix.shape, ymix.dtype),
        input_output_aliases={4: 0},
        compiler_params=_cparams("parallel", "parallel"),
        name="short_conv",
    )(px, px, px, conv_w, ymix)


def _pool_kernel(u_ref, w_ref, sc_ref, ymix_ref, o_ref, pa_ref, pb_ref, *, n_tok, unit, pg):
    del ymix_ref
    n_pos = n_tok // unit
    reach = 8 * unit
    pad = -(-reach // SUBLANES) * SUBLANES
    mg = -(-4 * unit // SUBLANES) * SUBLANES
    tp = n_tok + 2 * pad
    zeros_mg = jnp.zeros((mg, pg), F32)
    zeros_pad = jnp.zeros((pad, pg), F32)
    for buf in (pa_ref, pb_ref):
        buf[0:mg, :] = zeros_mg
        buf[mg + tp:mg + tp + mg, :] = zeros_mg
    row = lax.broadcasted_iota(jnp.int32, (n_tok, 1), 0) // unit
    for g, win in enumerate(POOL_WINDOWS):
        cs = slice(g * pg, (g + 1) * pg)
        u = u_ref[:, cs]
        pa_ref[mg:mg + pad, :] = zeros_pad
        pa_ref[mg + pad:mg + pad + n_tok, :] = u
        pa_ref[mg + pad + n_tok:mg + tp, :] = zeros_pad
        src, dst = pa_ref, pb_ref
        dst[mg:mg + tp, :] = src[mg - unit:mg - unit + tp, :] + src[mg:mg + tp, :]
        src, dst = dst, src
        step, w = 1, 2
        while w < win:
            dst[mg:mg + tp, :] = (src[mg - step * unit:mg - step * unit + tp, :]
                                  + src[mg + step * unit:mg + step * unit + tp, :])
            src, dst = dst, src
            step, w = step * 2, w * 2
        total = src[mg + pad:mg + pad + n_tok, :]
        lo = jnp.maximum(row - win // 2, 0)
        hi = jnp.minimum(row - win // 2 + win, n_pos)
        count = (hi - lo).astype(F32)
        dlt = total / count - u
        y = _dot(dlt.astype(BF16), w_ref[g]) * sc_ref[:, cs]
        o_ref[:, cs] = y.astype(o_ref.dtype)


def _pool(px, pool_w, pool_scale, ymix, *, u_col0, out_col0, n_tok, unit, blk0, n_batch, name):
    width = pool_scale.shape[1]
    pg = width // len(POOL_WINDOWS)
    reach = 8 * unit
    pad = -(-reach // SUBLANES) * SUBLANES
    mg = -(-4 * unit // SUBLANES) * SUBLANES
    buf_rows = n_tok + 2 * pad + 2 * mg
    ub = u_col0 // width
    ob = out_col0 // width
    return pl.pallas_call(
        functools.partial(_pool_kernel, n_tok=n_tok, unit=unit, pg=pg),
        grid=(n_batch,),
        in_specs=[pl.BlockSpec((n_tok, width), lambda b: (blk0 + b, ub)),
                  pl.BlockSpec(pool_w.shape, lambda b: (0, 0, 0)),
                  pl.BlockSpec((1, width), lambda b: (0, 0)),
                  pl.BlockSpec(memory_space=pl.ANY)],
        out_specs=pl.BlockSpec((n_tok, width), lambda b: (blk0 + b, ob)),
        out_shape=jax.ShapeDtypeStruct(ymix.shape, ymix.dtype),
        input_output_aliases={3: 0},
        scratch_shapes=[pltpu.VMEM((buf_rows, pg), F32), pltpu.VMEM((buf_rows, pg), F32)],
        compiler_params=_cparams("parallel"),
        name=name,
    )(px, pool_w, pool_scale, ymix)


def _ba_permutation(n_heads):
    perm = np.zeros(4 * n_heads, np.int32)
    for p in range(n_heads // 2):
        for d in range(2):
            for s in range(2):
                for kind in range(2):
                    perm[p * 8 + d * 4 + s * 2 + kind] = d * 2 * n_heads + kind * n_heads + 2 * p + s
    return perm


def _gate_param_row(param, n_heads):
    row = jnp.zeros((LANES,), F32)
    idx, src_d, src_h = [], [], []
    for p in range(n_heads // 2):
        for d in range(2):
            for s in range(2):
                idx.append(p * 8 + d * 4 + s * 2 + 1)
                src_d.append(d)
                src_h.append(2 * p + s)
    row = row.at[np.array(idx)].set(param[np.array(src_d), np.array(src_h)])
    return row.reshape(1, LANES)


def _pick_tile(n, prefs):
    for t in prefs:
        if n % t == 0:
            return t
    raise ValueError(f"no tile for {n}")


def kernel(x, c, ctx, c_ctx, w_mod, b_mod, norm_ffn1, w_ffn1_gu, w_ffn1_down, norm_mix, w_in, conv_qkv, a_log,
           dt_bias, gdn_norm, conv_short, pool_w, pool_scale, w_out, norm_ffn2, w_ffn2_gu, w_ffn2_down, norm_final):
    B, S, D = x.shape
    CL = ctx.shape[1]
    L = w_mod.shape[0]
    n_vheads = a_log.shape[2]
    qk_dim = (n_vheads // 2) * HEAD_DIM
    v_dim = n_vheads * HEAD_DIM
    sc_dim = conv_short.shape[2]
    pool_dim = pool_scale.shape[1]
    n_ba = 4 * n_vheads
    n_pairs = n_vheads // 2
    bs_rows = B * S
    R = bs_rows + B * CL
    assert S % CHUNK == 0 and CL % CHUNK == 0 and S % CL == 0 and n_ba <= LANES

    tm = _pick_tile(math.gcd(S, R), (1024, 512, 256, 128))
    tg = _pick_tile(math.gcd(S, R), (512, 256, 128))
    tr = min(256, CL)
    assert (B * CL) % tm == 0
    tn = 512 if D % 512 == 0 else 256
    tn_ff = 256
    tc_qk = _pick_tile(qk_dim, (1024, 512, 256, 128))
    tc_v = math.gcd(v_dim, 2 * tc_qk)

    z_col0 = 2 * qk_dim + v_dim
    ba_col0 = z_col0 + v_dim
    sc_col0_src = ba_col0 + n_ba
    sc_col0 = ba_col0
    pool_col0 = sc_col0 + 3 * sc_dim
    ba_perm = _ba_permutation(n_vheads)
    onehot = np.zeros((n_ba, LANES), np.float32)
    onehot[ba_perm, np.arange(n_ba)] = 1.0
    ba_onehot = jnp.asarray(onehot, dtype=BF16)

    h = (x.reshape(bs_rows, D), ctx.reshape(B * CL, D))
    c_all = jnp.zeros((SUBLANES, D), F32).at[0:B].set(c).at[B].set(c_ctx)
    mods_all = _modulation(c_all, w_mod, b_mod.reshape(L, 1, N_MOD * D)).reshape(L, SUBLANES, N_MOD, D)

    common = dict(seq_rows=S, n_batch=B)
    seg = dict(bs_rows=bs_rows, seq=S, ctx_len=CL)
    whole = lambda w: [(0, w.shape[2])]
    w1gu = w_ffn1_gu[0].astype(BF16)
    for l in range(L):
        n_live = bs_rows if l == L - 1 else R
        mods = mods_all[l]

        hn = _adaln(h, norm_ffn1[l].reshape(1, D), mods, n_rows=R, shift_idx=0, scale_idx=1, tr=tg,
                    name="ffn1_adaln", **common)
        act, (w1d,) = _matmul(hn, w1gu, swiglu=True, out_dtype=BF16, tm=tm, tn=tn, name="ffn1_up",
                              casts=[(w_ffn1_down, l, whole(w_ffn1_down), tn_ff)])
        h, (w_main, w_ba_src) = _matmul_residual(
            act, w1d, h, mods, n_rows=R, gate_idx=2, scale=0.5, tm=tm, name="ffn1_down", **common,
            casts=[(w_in, l, [(0, ba_col0), (sc_col0_src, w_in.shape[2])], tn),
                   (w_in, l, [(ba_col0, sc_col0_src)], None)])
        w_ba = jnp.dot(w_ba_src, ba_onehot, preferred_element_type=F32).astype(BF16).reshape(1, D, LANES)

        hn = _adaln(h, norm_mix[l].reshape(1, D), mods, n_rows=R, shift_idx=3, scale_idx=4, tr=tg,
                    name="mix_adaln", **common)
        px, (w2gu, wo) = _matmul(hn, w_main, swiglu=False, out_dtype=F32, tm=tm, name="in_proj",
                                 casts=[(w_ffn2_gu, l, whole(w_ffn2_gu), tn), (w_out, l, whole(w_out), tn)])
        ba, _ = _matmul(hn, w_ba, swiglu=False, out_dtype=F32, tm=tm, name="in_proj_ba")

        conv = dict(tr=tr, **seg)
        (qn,) = _qkv_conv(px, conv_qkv[l][:, 0:qk_dim], kind='q', col0=0, width=qk_dim, tc=tc_qk, **conv)
        kn, knt = _qkv_conv(px, conv_qkv[l][:, qk_dim:2 * qk_dim], kind='k', col0=qk_dim, width=qk_dim,
                            tc=tc_qk, **conv)
        (vn,) = _qkv_conv(px, conv_qkv[l][:, 2 * qk_dim:], kind='v', col0=2 * qk_dim, width=v_dim,
                          tc=tc_v, **conv)
        cols, rows = _gdn_gates(ba, _gate_param_row(a_log[l], n_vheads), _gate_param_row(dt_bias[l], n_vheads),
                                n_pairs=n_pairs, tm=tg)
        ymix, y_ctx = _gdn(qn, kn, knt, vn, cols, rows, px, z_col0, gdn_norm[l].reshape(1, HEAD_DIM),
                           n_batch=B, seq=S, ctx_len=CL, out_cols=v_dim + sc_dim + pool_dim)
        if l + 1 < L:
            ymix = _place_context_rows(y_ctx, ymix, row_blk0=bs_rows // CL, blk_rows=CL,
                                       blk_cols=math.gcd(v_dim, 4 * LANES))

        ymix = _shortconv(px, conv_short[l], ymix, x_col0=sc_col0, width=sc_dim, out_col0=v_dim,
                          tr=tr, tc=_pick_tile(sc_dim, (1024, 512, 256, 128)), **seg)
        pw = pool_w[l].astype(BF16)
        psc = pool_scale[l].reshape(1, pool_dim)
        ymix = _pool(px, pw, psc, ymix, u_col0=pool_col0, out_col0=v_dim + sc_dim, n_tok=S, unit=GRID_W,
                     blk0=0, n_batch=B, name="pool_latent")
        ymix = _pool(px, pw, psc, ymix, u_col0=pool_col0, out_col0=v_dim + sc_dim, n_tok=CL, unit=1,
                     blk0=bs_rows // CL, n_batch=B, name="pool_context")
        h, _ = _matmul_residual(ymix, wo, h, mods, n_rows=n_live, gate_idx=5, scale=1.0, tm=tm,
                                name="mix_out", **common)

        hn = _adaln(h, norm_ffn2[l].reshape(1, D), mods, n_rows=n_live, shift_idx=6, scale_idx=7, tr=tg,
                    name="ffn2_adaln", **common)
        act, (w2d,) = _matmul(hn, w2gu, swiglu=True, out_dtype=BF16, tm=tm, name="ffn2_up",
                              casts=[(w_ffn2_down, l, whole(w_ffn2_down), tn_ff)])
        next_casts = [(w_ffn1_gu, l + 1, whole(w_ffn1_gu), tn)] if l + 1 < L else []
        h, nxt = _matmul_residual(act, w2d, h, mods, n_rows=n_live, gate_idx=8, scale=0.5, tm=tm,
                                  name="ffn2_down", **common, casts=next_casts)
        if nxt:
            w1gu = nxt[0]

    out = _final_rmsnorm(h, norm_final.reshape(1, D), bs_rows, tg)
    return out.reshape(B, S, D)
```

```python
import functools
import math

import numpy as np
import jax
import jax.numpy as jnp
from jax import lax
from jax.experimental import pallas as pl
from jax.experimental.pallas import tpu as pltpu

EPS = 1e-6
N_MOD = 9
GRID_W = 64
HEAD_DIM = 128
GDN_CONV = 5
SC_CONV = 3
POOL_WINDOWS = (2, 4, 8, 16)
CHUNK = 128
INV_BLOCK = 16
LANES = 128
SUBLANES = 8
BF16_SUBLANES = 16
VMEM_LIMIT = 56 * 1024 * 1024

BF16 = jnp.bfloat16
F32 = jnp.float32


def _cparams(*sem):
    return pltpu.CompilerParams(dimension_semantics=sem, vmem_limit_bytes=VMEM_LIMIT)


def _dot(a, b):
    return jnp.dot(a, b, preferred_element_type=F32)


def _dot_hi(a, b):
    return jnp.dot(a, b, preferred_element_type=F32, precision=lax.Precision.HIGHEST)


def _silu(v):
    return v * jax.nn.sigmoid(v)


def _mod_kernel(c_ref, w_ref, b_ref, o_ref):
    a = _silu(c_ref[...]).astype(BF16)
    o_ref[...] = _dot(a, w_ref[...].astype(BF16)) + b_ref[...]


def _modulation(c_all, w_mod, b_mod, tn=512):
    L, D, N = w_mod.shape
    return pl.pallas_call(
        _mod_kernel,
        grid=(L, N // tn),
        in_specs=[pl.BlockSpec((SUBLANES, D), lambda l, j: (0, 0)),
                  pl.BlockSpec((None, D, tn), lambda l, j: (l, 0, j)),
                  pl.BlockSpec((None, 1, tn), lambda l, j: (l, 0, j))],
        out_specs=pl.BlockSpec((None, SUBLANES, tn), lambda l, j: (l, 0, j)),
        out_shape=jax.ShapeDtypeStruct((L, SUBLANES, N), F32),
        compiler_params=_cparams("parallel", "parallel"),
        name="modulation",
    )(c_all, w_mod, b_mod)


def _two_part_specs(parts, block, n_first, col_of):
    first = pl.BlockSpec(block, lambda i, *j: (jnp.minimum(i, n_first - 1), *col_of(i < n_first, *j)))
    second = pl.BlockSpec(block, lambda i, *j: (jnp.maximum(i - n_first, 0), *col_of(i >= n_first, *j)))
    return [first, second]


def _adaln_kernel(*refs, shift_idx, scale_idx, n_first):
    if n_first is None:
        h_ref, nw_ref, mod_ref, o_ref = refs
        h = h_ref[...]
    else:
        x_ref, c_ref, nw_ref, mod_ref, o_ref = refs
        h = jnp.where(pl.program_id(0) < n_first, x_ref[...], c_ref[...])
    ms = jnp.mean(h * h, axis=-1, keepdims=True)
    y = h * lax.rsqrt(ms + EPS) * nw_ref[...]
    shift = mod_ref[0, shift_idx:shift_idx + 1, :]
    scale = mod_ref[0, scale_idx:scale_idx + 1, :]
    o_ref[...] = (y * (1.0 + scale) + shift).astype(o_ref.dtype)


def _adaln(h, norm_w, mods, *, n_rows, seq_rows, n_batch, shift_idx, scale_idx, tr, name):
    parts = h if isinstance(h, tuple) else (h,)
    D = parts[0].shape[1]
    mod_map = lambda i: (jnp.minimum((i * tr) // seq_rows, n_batch), 0, 0)
    if len(parts) == 2:
        n_first = parts[0].shape[0] // tr
        h_specs = _two_part_specs(parts, (tr, D), n_first, lambda active: (0,))
    else:
        n_first = None
        h_specs = [pl.BlockSpec((tr, D), lambda i: (i, 0))]
    return pl.pallas_call(
        functools.partial(_adaln_kernel, shift_idx=shift_idx, scale_idx=scale_idx, n_first=n_first),
        grid=(n_rows // tr,),
        in_specs=h_specs + [pl.BlockSpec((1, D), lambda i: (0, 0)),
                            pl.BlockSpec((1, N_MOD, D), mod_map)],
        out_specs=pl.BlockSpec((tr, D), lambda i: (i, 0)),
        out_shape=jax.ShapeDtypeStruct((n_rows, D), BF16),
        compiler_params=_cparams("parallel"),
        name=name,
    )(*parts, norm_w, mods)


def _cast_row_block(rows, steps):
    for rps in range(BF16_SUBLANES, rows + 1, BF16_SUBLANES):
        if rows % rps == 0 and rows // rps <= steps:
            return rps
    raise ValueError(f"cannot spread {rows} rows over {steps} steps")


def _cast_plan(casts, n_i, nj):
    in_specs, out_specs, out_shapes, args, plans = [], [], [], [], []
    for src, layer, ranges, tile in casts:
        _, rows, cols = src.shape
        rps = _cast_row_block(rows, n_i * nj)
        last = rows // rps - 1
        out_cols = sum(b - a for a, b in ranges)
        src_map = lambda i, j, last=last, layer=layer: (layer, jnp.minimum(i * nj + j, last), 0)
        in_specs.append(pl.BlockSpec((None, rps, cols), src_map))
        if tile is None:
            out_specs.append(pl.BlockSpec((rps, out_cols), lambda i, j, last=last: (jnp.minimum(i * nj + j, last), 0)))
            out_shapes.append(jax.ShapeDtypeStruct((rows, out_cols), BF16))
            pieces = tuple((None, a, b) for a, b in ranges)
        else:
            assert all((b - a) % tile == 0 for a, b in ranges)
            nt = out_cols // tile
            out_specs.append(pl.BlockSpec((nt, rps, tile),
                                          lambda i, j, last=last: (0, jnp.minimum(i * nj + j, last), 0)))
            out_shapes.append(jax.ShapeDtypeStruct((nt, rows, tile), BF16))
            pieces = tuple((None, s, s + tile) for a, b in ranges for s in range(a, b, tile))
            pieces = tuple((t, a, b) for t, (_, a, b) in enumerate(pieces))
        args.append(src)
        plans.append(pieces)
    return in_specs, out_specs, out_shapes, args, plans


def _run_casts(src_refs, dst_refs, plans):
    for src_ref, dst_ref, pieces in zip(src_refs, dst_refs, plans):
        off = 0
        for t, a, b in pieces:
            if t is None:
                dst_ref[:, off:off + b - a] = src_ref[:, a:b].astype(dst_ref.dtype)
                off += b - a
            else:
                dst_ref[t] = src_ref[:, a:b].astype(dst_ref.dtype)


def _row_tile(a_hbm, buf_ref, sem_ref):
    i, j = pl.program_id(0), pl.program_id(1)
    tm = buf_ref.shape[1]
    slot = lax.rem(i, 2)

    def fetch(tile, s):
        rows = pl.ds(pl.multiple_of(tile * tm, tm), tm)
        return pltpu.make_async_copy(a_hbm.at[rows, :], buf_ref.at[s], sem_ref.at[s])

    @pl.when(jnp.logical_and(i == 0, j == 0))
    def _():
        fetch(0, 0).start()

    @pl.when(j == 0)
    def _():
        fetch(i, slot).wait()

        @pl.when(i + 1 < pl.num_programs(0))
        def _():
            fetch(i + 1, 1 - slot).start()

    return buf_ref[slot]


def _row_tile_scratch(tm, k):
    return [pltpu.VMEM((2, tm, k), BF16), pltpu.SemaphoreType.DMA((2,))]


def _mm_kernel(a_hbm, *rest, swiglu, cast_ranges, w_rows_out):
    nc = len(cast_ranges)
    n_w = 2 if swiglu else 1
    w_refs, src_refs = rest[:n_w], rest[n_w:n_w + nc]
    o_ref, dst_refs = rest[n_w + nc], rest[n_w + nc + 1:-2]
    a = _row_tile(a_hbm, *rest[-2:])
    if swiglu:
        half = o_ref.shape[1] // 2
        for cs in (slice(0, half), slice(half, 2 * half)):
            g = _dot(a, w_refs[0][:, cs])
            u = _dot(a, w_refs[1][:, cs])
            o_ref[:, cs] = (_silu(g) * u).astype(o_ref.dtype)
    elif w_rows_out:
        acc = lax.dot_general(a, w_refs[0][...], (((1,), (1,)), ((), ())), preferred_element_type=F32)
        o_ref[...] = acc.astype(o_ref.dtype)
    else:
        o_ref[...] = _dot(a, w_refs[0][...]).astype(o_ref.dtype)
    _run_casts(src_refs, dst_refs, cast_ranges)


def _weight_specs(w, tn, swiglu, w_rows_out):
    if w_rows_out:
        assert not swiglu
        if isinstance(w, tuple):
            stack, layer = w
            return [pl.BlockSpec((None, tn, stack.shape[2]), lambda i, j: (layer, j, 0))], stack.shape[1] // tn, tn
        return [pl.BlockSpec((tn, w.shape[1]), lambda i, j: (j, 0))], w.shape[0] // tn, tn
    if w.ndim == 2:
        k, nt = w.shape[0], w.shape[1] // tn
        nj = nt // 2 if swiglu else nt
        specs = [pl.BlockSpec((k, tn), lambda i, j: (0, j))]
        if swiglu:
            specs.append(pl.BlockSpec((k, tn), lambda i, j: (0, j + nj)))
    else:
        nt, k, tn = w.shape
        nj = nt // 2 if swiglu else nt
        specs = [pl.BlockSpec((None, k, tn), lambda i, j: (j, 0, 0))]
        if swiglu:
            specs.append(pl.BlockSpec((None, k, tn), lambda i, j: (j + nj, 0, 0)))
    return specs, nj, tn


def _matmul(a, w, *, swiglu, out_dtype, tm, name, tn=None, casts=(), w_rows_out=False):
    R, K = a.shape
    w_specs, nj, tn = _weight_specs(w, tn, swiglu, w_rows_out)
    n_out = nj * tn
    in_specs = [pl.BlockSpec(memory_space=pl.ANY)] + w_specs
    args = [a] + [w[0] if isinstance(w, tuple) else w] * len(w_specs)
    c_in, c_out, c_shapes, c_args, c_ranges = _cast_plan(casts, R // tm, nj)
    res = pl.pallas_call(
        functools.partial(_mm_kernel, swiglu=swiglu, cast_ranges=c_ranges, w_rows_out=w_rows_out),
        grid=(R // tm, nj),
        in_specs=in_specs + c_in,
        out_specs=[pl.BlockSpec((tm, tn), lambda i, j: (i, j))] + c_out,
        out_shape=[jax.ShapeDtypeStruct((R, n_out), out_dtype)] + c_shapes,
        scratch_shapes=_row_tile_scratch(tm, K),
        compiler_params=_cparams("arbitrary", "arbitrary"),
        name=name,
    )(*args, *c_args)
    return res[0], res[1:]


def _w_in_prep_kernel(src_hbm, ba_ref, o_ref, oba_ref, buf_ref, sem_ref, *, tile, split_tile, skip):
    l, j = pl.program_id(0), pl.program_id(1)
    n_j = pl.num_programs(1)
    step = l * n_j + j
    slot = lax.rem(step, 2)

    def fetch(ll, jj, s):
        row0 = jj * tile + jnp.where(jj >= split_tile, skip, 0)
        rows = pl.ds(pl.multiple_of(row0, SUBLANES), tile)
        return pltpu.make_async_copy(src_hbm.at[ll, rows, :], buf_ref.at[s], sem_ref.at[s])

    @pl.when(step == 0)
    def _():
        fetch(0, 0, 0).start()

    fetch(l, j, slot).wait()

    @pl.when(step + 1 < pl.num_programs(0) * n_j)
    def _():
        wrap = j + 1 == n_j
        fetch(jnp.where(wrap, l + 1, l), jnp.where(wrap, 0, j + 1), 1 - slot).start()

    o_ref[...] = buf_ref[slot].astype(o_ref.dtype)
    oba_ref[...] = ba_ref[...].astype(oba_ref.dtype)


def _w_in_prep(w_in_t, *, tile, ba_row0, n_ba):
    L, d_in, K = w_in_t.shape
    n_main = d_in - n_ba
    assert ba_row0 % tile == 0 and n_main % tile == 0 and ba_row0 % n_ba == 0 and n_ba % SUBLANES == 0
    return pl.pallas_call(
        functools.partial(_w_in_prep_kernel, tile=tile, split_tile=ba_row0 // tile, skip=n_ba),
        grid=(L, n_main // tile),
        in_specs=[pl.BlockSpec(memory_space=pl.ANY),
                  pl.BlockSpec((None, n_ba, K), lambda l, j: (l, ba_row0 // n_ba, 0))],
        out_specs=[pl.BlockSpec((None, tile, K), lambda l, j: (l, j, 0)),
                   pl.BlockSpec((None, n_ba, K), lambda l, j: (l, 0, 0))],
        out_shape=[jax.ShapeDtypeStruct((L, n_main, K), BF16), jax.ShapeDtypeStruct((L, n_ba, K), BF16)],
        scratch_shapes=[pltpu.VMEM((2, tile, K), F32), pltpu.SemaphoreType.DMA((2,))],
        compiler_params=_cparams("arbitrary", "arbitrary"),
        name="w_in_prep",
    )(w_in_t, w_in_t)


def _mm_res_kernel(a_hbm, w_ref, *rest, gate_idx, scale, cast_ranges, n_first):
    nc = len(cast_ranges)
    a = _row_tile(a_hbm, *rest[-2:])
    rest = rest[:-2]
    if n_first is None:
        res = rest[0][...]
        rest = rest[1:]
    else:
        res = jnp.where(pl.program_id(0) < n_first, rest[0][...], rest[1][...])
        rest = rest[2:]
    mod_ref, src_refs, o_ref, dst_refs = rest[0], rest[1:1 + nc], rest[1 + nc], rest[2 + nc:]
    acc = _dot(a, w_ref[...])
    gate = mod_ref[0, gate_idx:gate_idx + 1, :]
    o_ref[...] = res + (scale * gate) * acc
    _run_casts(src_refs, dst_refs, cast_ranges)


def _matmul_residual(a, w, res, mods, *, n_rows, seq_rows, n_batch, gate_idx, scale, tm, name, casts=()):
    K = a.shape[1]
    nt, _, tn = w.shape
    D = nt * tn
    mod_map = lambda i, j: (jnp.minimum((i * tm) // seq_rows, n_batch), 0, j)
    parts = res if isinstance(res, tuple) else (res,)
    if len(parts) == 2:
        n_first = parts[0].shape[0] // tm
        res_specs = _two_part_specs(parts, (tm, tn), n_first, lambda active, j: (jnp.where(active, j, 0),))
    else:
        n_first = None
        res_specs = [pl.BlockSpec((tm, tn), lambda i, j: (i, j))]
    c_in, c_out, c_shapes, c_args, c_ranges = _cast_plan(casts, n_rows // tm, D // tn)
    out = pl.pallas_call(
        functools.partial(_mm_res_kernel, gate_idx=gate_idx, scale=scale, cast_ranges=c_ranges, n_first=n_first),
        grid=(n_rows // tm, D // tn),
        in_specs=[pl.BlockSpec(memory_space=pl.ANY),
                  pl.BlockSpec((None, K, tn), lambda i, j: (j, 0, 0))] + res_specs
                 + [pl.BlockSpec((1, N_MOD, tn), mod_map)] + c_in,
        out_specs=[pl.BlockSpec((tm, tn), lambda i, j: (i, j))] + c_out,
        out_shape=[jax.ShapeDtypeStruct((n_rows, D), F32)] + c_shapes,
        scratch_shapes=_row_tile_scratch(tm, K),
        compiler_params=_cparams("arbitrary", "arbitrary"),
        name=name,
    )(a, w, *parts, mods, *c_args)
    return out[0], out[1:]


def _rmsnorm_kernel(h_ref, w_ref, o_ref):
    h = h_ref[...]
    ms = jnp.mean(h * h, axis=-1, keepdims=True)
    o_ref[...] = h * lax.rsqrt(ms + EPS) * w_ref[...]


def _final_rmsnorm(h, w, rows, tm):
    D = h.shape[1]
    return pl.pallas_call(
        _rmsnorm_kernel,
        grid=(rows // tm,),
        in_specs=[pl.BlockSpec((tm, D), lambda i: (i, 0)),
                  pl.BlockSpec((1, D), lambda i: (0, 0))],
        out_specs=pl.BlockSpec((tm, D), lambda i: (i, 0)),
        out_shape=jax.ShapeDtypeStruct((rows, D), F32),
        compiler_params=_cparams("parallel"),
        name="final_rmsnorm",
    )(h, w)


def _segment_position(tile_rows, bs_rows, seq, ctx_len):
    row0 = pl.program_id(0) * tile_rows
    in_x = row0 < bs_rows
    off = jnp.where(in_x, lax.rem(row0, seq), lax.rem(row0 - bs_rows, ctx_len))
    seglen = jnp.where(in_x, seq, ctx_len)
    return in_x, off, seglen


def _qkv_conv_kernel(cur_ref, prev_ref, next_ref, w_ref, *rest, kind, tr, bs_rows, seq, ctx_len):
    if kind == 'k':
        o_ref, ot_ref, ext_ref = rest
    else:
        o_ref, ext_ref = rest
    _, off, seglen = _segment_position(tr, bs_rows, seq, ctx_len)
    is_start = off == 0
    is_end = off + tr == seglen
    pad = SUBLANES
    half = GDN_CONV // 2
    ext_ref[0:pad, :] = jnp.where(is_start, 0.0, prev_ref[...])
    ext_ref[pad:pad + tr, :] = cur_ref[...]
    ext_ref[pad + tr:pad + tr + pad, :] = jnp.where(is_end, 0.0, next_ref[...])
    ext = ext_ref[...]
    n_ext = tr + 2 * pad
    acc = None
    for j in range(GDN_CONV):
        shifted = ext if j == half else pltpu.roll(ext, (half - j) % n_ext, axis=0)
        term = shifted[pad:pad + tr, :] * w_ref[j:j + 1, :]
        acc = term if acc is None else acc + term
    y = _silu(acc)
    tc = y.shape[1]
    if kind == 'v':
        o_ref[...] = y.astype(o_ref.dtype)
        return
    mult = HEAD_DIM ** -0.5 if kind == 'q' else 1.0
    for hd in range(tc // HEAD_DIM):
        yh = y[:, hd * HEAD_DIM:(hd + 1) * HEAD_DIM]
        inv = lax.rsqrt(jnp.sum(yh * yh, axis=-1, keepdims=True) + EPS)
        yn = yh * inv
        if kind == 'q':
            yn = yn * mult
        o_ref[:, hd * HEAD_DIM:(hd + 1) * HEAD_DIM] = yn.astype(o_ref.dtype)
        if kind == 'k':
            ot_ref[hd * HEAD_DIM:(hd + 1) * HEAD_DIM, :] = yn.T.astype(ot_ref.dtype)


def _qkv_conv(px, conv_w, *, kind, col0, width, tr, tc, bs_rows, seq, ctx_len):
    R = px.shape[0]
    cb0 = col0 // tc
    rb = tr // SUBLANES
    last = R // SUBLANES - 1
    in_specs = [pl.BlockSpec((tr, tc), lambda i, j: (i, cb0 + j)),
                pl.BlockSpec((SUBLANES, tc), lambda i, j: (jnp.maximum(i * rb - 1, 0), cb0 + j)),
                pl.BlockSpec((SUBLANES, tc), lambda i, j: (jnp.minimum((i + 1) * rb, last), cb0 + j)),
                pl.BlockSpec((GDN_CONV, tc), lambda i, j: (0, j))]
    out_specs = [pl.BlockSpec((tr, tc), lambda i, j: (i, j))]
    out_shape = [jax.ShapeDtypeStruct((R, width), BF16)]
    if kind == 'k':
        out_specs.append(pl.BlockSpec((tc, tr), lambda i, j: (j, i)))
        out_shape.append(jax.ShapeDtypeStruct((width, R), BF16))
    res = pl.pallas_call(
        functools.partial(_qkv_conv_kernel, kind=kind, tr=tr, bs_rows=bs_rows, seq=seq, ctx_len=ctx_len),
        grid=(R // tr, width // tc),
        in_specs=in_specs,
        out_specs=out_specs,
        out_shape=out_shape,
        scratch_shapes=[pltpu.VMEM((tr + 2 * SUBLANES, tc), F32)],
        compiler_params=_cparams("parallel", "parallel"),
        name="gdn_conv_" + kind,
    )(px, px, px, conv_w)
    return res


def _gate_kernel(ba_ref, alog_ref, dtb_ref, cols_ref, rows_ref, *, tm, n_pairs):
    lane = lax.broadcasted_iota(jnp.int32, (CHUNK, LANES), 1)
    is_beta = (lane & 1) == 0
    is_bwd = (lane & 4) != 0
    ri = lax.broadcasted_iota(jnp.int32, (CHUNK, CHUNK), 0)
    ci = lax.broadcasted_iota(jnp.int32, (CHUNK, CHUNK), 1)
    tri_lo = jnp.where(ci <= ri, 1.0, 0.0)
    tri_up = jnp.where(ci >= ri, 1.0, 0.0)
    neg_a = -jnp.exp(alog_ref[...])
    for c in range(tm // CHUNK):
        sl = slice(c * CHUNK, (c + 1) * CHUNK)
        x = ba_ref[sl, :]
        beta = jax.nn.sigmoid(x)
        y = x + dtb_ref[...]
        softplus = jnp.maximum(y, 0.0) + jnp.log1p(jnp.exp(-jnp.abs(y)))
        g = jnp.where(is_beta, 0.0, neg_a * softplus)
        gc = jnp.where(is_bwd, _dot_hi(tri_up, g), _dot_hi(tri_lo, g))
        full = jnp.where(is_beta, beta, gc)
        for p in range(n_pairs):
            cols_ref[p, sl, :] = full[:, p * 8:(p + 1) * 8]
        rows_ref[:, sl] = full.T[0:n_pairs * 8, :]


def _gdn_gates(ba, alog_row, dtb_row, *, n_pairs, tm):
    R = ba.shape[0]
    return pl.pallas_call(
        functools.partial(_gate_kernel, tm=tm, n_pairs=n_pairs),
        grid=(R // tm,),
        in_specs=[pl.BlockSpec((tm, LANES), lambda i: (i, 0)),
                  pl.BlockSpec((1, LANES), lambda i: (0, 0)),
                  pl.BlockSpec((1, LANES), lambda i: (0, 0))],
        out_specs=[pl.BlockSpec((n_pairs, tm, 8), lambda i: (0, i, 0)),
                   pl.BlockSpec((n_pairs * 8, tm), lambda i: (0, i))],
        out_shape=[jax.ShapeDtypeStruct((n_pairs, R, 8), F32),
                   jax.ShapeDtypeStruct((n_pairs * 8, R), F32)],
        compiler_params=_cparams("parallel"),
        name="gdn_gates",
    )(ba, alog_row, dtb_row)


def _gdn_kernel(qx_ref, qc_ref, kx_ref, kc_ref, ktx_ref, ktc_ref, vx_ref, vc_ref,
                colx_ref, colc_ref, rowx_ref, rowc_ref, zx_ref, zc_ref, nw_ref, ox_ref, oc_ref,
                t_ref, a_ref, of_ref, ob_ref, sf_ref, sb_ref, *, seq, ctx_len, rb, n_pair, prep_chunks):
    C = CHUNK
    HD = HEAD_DIM
    n_xc = seq // C
    n_cc = ctx_len // C
    x_refs = (qx_ref, kx_ref, ktx_ref, vx_ref, colx_ref, rowx_ref)
    c_refs = (qc_ref, kc_ref, ktc_ref, vc_ref, colc_ref, rowc_ref)

    ri = lax.broadcasted_iota(jnp.int32, (C, C), 0)
    ci = lax.broadcasted_iota(jnp.int32, (C, C), 1)
    eye = (ri == ci).astype(F32)

    def lane_idx(d, s, kind):
        return d * 4 + s * 2 + kind

    block_diff = ri ^ ci

    def unit_triangular_inverses(nmats):
        base = int(math.log2(INV_BLOCK))
        n0s = [jnp.where((block_diff >> base) == 0, n, 0.0) for n in nmats]
        invs = [eye + n0 for n0 in n0s]
        pws = n0s
        for _ in range(base - 1):
            pwbs = [pw.astype(BF16) for pw in pws]
            pws = [_dot(b, b) for b in pwbs]
            invs = [inv + _dot(inv.astype(BF16), pw.astype(BF16)) for inv, pw in zip(invs, pws)]
        for level in range(base, int(math.log2(C))):
            offs = [jnp.where((block_diff >> level) == 1, n, 0.0).astype(BF16) for n in nmats]
            invbs = [inv.astype(BF16) for inv in invs]
            halves = [_dot(ib, off).astype(BF16) for ib, off in zip(invbs, offs)]
            invs = [inv + _dot(h, ib) for inv, h, ib in zip(invs, halves, invbs)]
        return invs

    def prepare(chunks):
        nmats, dests = [], []
        for refs, row0, slot in chunks:
            q_ref, k_ref, kt_ref, _, col_ref, row_ref = refs
            rows = pl.ds(row0, C)
            for u in range(n_pair):
                hs = slice(u * HD, (u + 1) * HD)
                qk = jnp.concatenate([q_ref[rows, hs], k_ref[rows, hs]], axis=0)
                gram = _dot(qk, kt_ref[hs, rows])
                qkt = gram[0:C, :]
                kkt = gram[C:2 * C, :]
                cols = col_ref[u, rows, :]
                for d in range(2):
                    incl = (ci <= ri) if d == 0 else (ci >= ri)
                    strict = (ci < ri) if d == 0 else (ci > ri)
                    for s in range(2):
                        kb, kg = lane_idx(d, s, 0), lane_idx(d, s, 1)
                        beta_col = cols[:, kb:kb + 1]
                        gc_col = cols[:, kg:kg + 1]
                        gc_row = row_ref[u * 8 + kg:u * 8 + kg + 1, rows]
                        decay = jnp.where(incl, jnp.exp(jnp.where(incl, gc_col - gc_row, 0.0)), 0.0)
                        nmats.append(jnp.where(strict, -(kkt * beta_col) * decay, 0.0))
                        a_ref[slot, u * 4 + d * 2 + s] = (qkt * decay).astype(BF16)
                        dests.append((slot, u * 4 + d * 2 + s))
        for (slot, idx), inv in zip(dests, unit_triangular_inverses(nmats)):
            t_ref[slot, idx] = inv.astype(BF16)

    def chain(tasks):
        units = [(d, u, refs, pl.ds(row0, C), slot, orow0)
                 for d, refs, row0, slot, orow0 in tasks for u in range(n_pair)]
        states, projs = [], []
        for d, u, refs, rows, _, _ in units:
            hs = slice(u * HD, (u + 1) * HD)
            state = (sf_ref if d == 0 else sb_ref)[u]
            qk = jnp.concatenate([refs[0][rows, hs], refs[1][rows, hs]], axis=0)
            states.append(state)
            projs.append(_dot(qk, state.astype(BF16)))
        scalars, rhss = [], []
        for (d, u, refs, rows, _, _), proj in zip(units, projs):
            cols = refs[4][u, rows, :]
            for s in range(2):
                kb, kg = lane_idx(d, s, 0), lane_idx(d, s, 1)
                beta_col = cols[:, kb:kb + 1]
                gc_col = cols[:, kg:kg + 1]
                g_last = gc_col[C - 1:C, :] if d == 0 else gc_col[0:1, :]
                e_col = jnp.exp(gc_col)
                vs = slice((2 * u + s) * HD, (2 * u + s + 1) * HD)
                v = refs[3][rows, vs].astype(F32)
                k_s = proj[C:2 * C, s * HD:(s + 1) * HD]
                rhss.append((beta_col * (v - e_col * k_s)).astype(BF16))
                scalars.append((gc_col, e_col, g_last))
        v_news = []
        for i, (d, u, _, _, slot, _) in enumerate(units):
            for s in range(2):
                v_news.append(_dot(t_ref[slot, u * 4 + d * 2 + s], rhss[2 * i + s]))
        for i, (d, u, refs, rows, slot, orow0) in enumerate(units):
            o_acc = of_ref if d == 0 else ob_ref
            evs, gls = [], []
            for s in range(2):
                gc_col, e_col, g_last = scalars[2 * i + s]
                v_new = v_news[2 * i + s]
                q_s = projs[i][0:C, s * HD:(s + 1) * HD]
                out = e_col * q_s + _dot(a_ref[slot, u * 4 + d * 2 + s], v_new.astype(BF16))
                o_acc[pl.ds(orow0, C), (2 * u + s) * HD:(2 * u + s + 1) * HD] = out
                evs.append(jnp.exp(g_last - gc_col) * v_new)
                gls.append(jnp.broadcast_to(jnp.exp(g_last), (HD, HD)))
            ev = jnp.concatenate(evs, axis=1).astype(BF16)
            new_state = states[i] * jnp.concatenate(gls, axis=1) + _dot(refs[2][u * HD:(u + 1) * HD, rows], ev)
            (sf_ref if d == 0 else sb_ref)[u] = new_state

    sf_ref[...] = jnp.zeros_like(sf_ref)
    sb_ref[...] = jnp.zeros_like(sb_ref)
    prepare([(c_refs, n * C, n) for n in range(n_cc)])

    def prep_body(it, carry):
        n0 = it * prep_chunks
        prepare([(x_refs, pl.multiple_of((n0 + j) * C, C), n_cc + n0 + j) for j in range(prep_chunks)])
        return carry
    lax.fori_loop(0, n_xc // prep_chunks, prep_body, 0)

    for n in range(n_cc):
        m = n_cc - 1 - n
        chain([(0, c_refs, n * C, n, seq + n * C), (1, c_refs, m * C, m, seq + m * C)])

    def chain_body(n, carry):
        m = n_xc - 1 - n
        rf = pl.multiple_of(n * C, C)
        rbk = pl.multiple_of(m * C, C)
        chain([(0, x_refs, rf, n_cc + n, rf), (1, x_refs, rbk, n_cc + m, rbk)])
        return carry
    lax.fori_loop(0, n_xc, chain_body, 0)

    def finish(z_ref, out_ref, base, n_blocks):
        def body(bi, carry):
            r0 = pl.multiple_of(bi * rb, rb)
            rows = pl.ds(r0, rb)
            acc_rows = pl.ds(pl.multiple_of(base + r0, rb), rb)
            o = of_ref[acc_rows, :] + ob_ref[acc_rows, :]
            z = z_ref[rows, :]
            for hd in range(2 * n_pair):
                hs = slice(hd * HD, (hd + 1) * HD)
                oh = o[:, hs]
                ms = jnp.mean(oh * oh, axis=-1, keepdims=True)
                y = oh * lax.rsqrt(ms + EPS) * nw_ref[...]
                out_ref[rows, hs] = (y * _silu(z[:, hs])).astype(out_ref.dtype)
            return carry
        lax.fori_loop(0, n_blocks, body, 0)

    finish(zx_ref, ox_ref, 0, seq // rb)
    finish(zc_ref, oc_ref, seq, ctx_len // rb)


def _gdn(qn, kn, knt, vn, cols, rows, px, z_col0, norm_w, *, n_batch, seq, ctx_len, out_cols):
    R = qn.shape[0]
    n_pairs = qn.shape[1] // HEAD_DIM
    C = CHUNK
    rb = min(256, ctx_len)
    cblk0 = n_batch * (seq // ctx_len)
    n_slots = (seq + ctx_len) // C
    n_pair = 2 if n_pairs % 2 == 0 else 1
    prep_chunks = 2 if (seq // C) % 2 == 0 else 1
    qw = n_pair * HEAD_DIM
    vw = 2 * qw
    zb0 = z_col0 // vw
    assert z_col0 % vw == 0

    x_rows = lambda w, c0=0: pl.BlockSpec((seq, w), lambda b, p: (b, c0 + p))
    c_rows = lambda w, c0=0: pl.BlockSpec((ctx_len, w), lambda b, p: (cblk0 + b, c0 + p))
    in_specs = [
        x_rows(qw), c_rows(qw),
        x_rows(qw), c_rows(qw),
        pl.BlockSpec((qw, seq), lambda b, p: (p, b)),
        pl.BlockSpec((qw, ctx_len), lambda b, p: (p, cblk0 + b)),
        x_rows(vw), c_rows(vw),
        pl.BlockSpec((n_pair, seq, 8), lambda b, p: (p, b, 0)),
        pl.BlockSpec((n_pair, ctx_len, 8), lambda b, p: (p, cblk0 + b, 0)),
        pl.BlockSpec((n_pair * 8, seq), lambda b, p: (p, b)),
        pl.BlockSpec((n_pair * 8, ctx_len), lambda b, p: (p, cblk0 + b)),
        x_rows(vw, zb0), c_rows(vw, zb0),
        pl.BlockSpec((1, HEAD_DIM), lambda b, p: (0, 0)),
    ]
    return pl.pallas_call(
        functools.partial(_gdn_kernel, seq=seq, ctx_len=ctx_len, rb=rb, n_pair=n_pair, prep_chunks=prep_chunks),
        grid=(n_batch, n_pairs // n_pair),
        in_specs=in_specs,
        out_specs=[pl.BlockSpec((seq, vw), lambda b, p: (b, p)),
                   pl.BlockSpec((ctx_len, vw), lambda b, p: (b, p))],
        out_shape=[jax.ShapeDtypeStruct((R, out_cols), BF16),
                   jax.ShapeDtypeStruct((n_batch * ctx_len, vn.shape[1]), BF16)],
        scratch_shapes=[pltpu.VMEM((n_slots, n_pair * 4, C, C), BF16),
                        pltpu.VMEM((n_slots, n_pair * 4, C, C), BF16),
                        pltpu.VMEM((seq + ctx_len, vw), F32),
                        pltpu.VMEM((seq + ctx_len, vw), F32),
                        pltpu.VMEM((n_pair, HEAD_DIM, 2 * HEAD_DIM), F32),
                        pltpu.VMEM((n_pair, HEAD_DIM, 2 * HEAD_DIM), F32)],
        compiler_params=_cparams("parallel", "parallel"),
        name="gdn_delta_rule",
    )(qn, qn, kn, kn, knt, knt, vn, vn, cols, cols, rows, rows, px, px, norm_w)


def _copy_kernel(src_ref, dst_hbm_ref, o_ref):
    del dst_hbm_ref
    o_ref[...] = src_ref[...]


def _place_context_rows(y_ctx, ymix, *, row_blk0, blk_rows, blk_cols):
    rows, cols = y_ctx.shape
    return pl.pallas_call(
        _copy_kernel,
        grid=(rows // blk_rows, cols // blk_cols),
        in_specs=[pl.BlockSpec((blk_rows, blk_cols), lambda i, j: (i, j)),
                  pl.BlockSpec(memory_space=pl.ANY)],
        out_specs=pl.BlockSpec((blk_rows, blk_cols), lambda i, j: (row_blk0 + i, j)),
        out_shape=jax.ShapeDtypeStruct(ymix.shape, ymix.dtype),
        input_output_aliases={1: 0},
        compiler_params=_cparams("parallel", "parallel"),
        name="place_context_rows",
    )(y_ctx, ymix)


def _shortconv_kernel(x_ref, b_ref, c_ref, w_ref, ymix_ref, o_ref, *, tr, bs_rows, seq, ctx_len):
    del ymix_ref
    in_x, off, _ = _segment_position(tr, bs_rows, seq, ctx_len)
    period = jnp.where(in_x, GRID_W, ctx_len)
    pos = (off + lax.broadcasted_iota(jnp.int32, (tr, 1), 0)) & (period - 1)
    v = c_ref[...] * x_ref[...]
    left = jnp.where(pos == 0, 0.0, pltpu.roll(v, 1, axis=0))
    right = jnp.where(pos == period - 1, 0.0, pltpu.roll(v, tr - 1, axis=0))
    y = left * w_ref[0:1, :] + v * w_ref[1:2, :] + right * w_ref[2:3, :]
    o_ref[...] = (b_ref[...] * y).astype(o_ref.dtype)


def _shortconv(px, conv_w, ymix, *, x_col0, width, out_col0, tr, tc, bs_rows, seq, ctx_len):
    R = px.shape[0]
    nb = width // tc
    xb0 = x_col0 // tc
    ob0 = out_col0 // tc
    return pl.pallas_call(
        functools.partial(_shortconv_kernel, tr=tr, bs_rows=bs_rows, seq=seq, ctx_len=ctx_len),
        grid=(R // tr, nb),
        in_specs=[pl.BlockSpec((tr, tc), lambda i, j: (i, xb0 + j)),
                  pl.BlockSpec((tr, tc), lambda i, j: (i, xb0 + nb + j)),
                  pl.BlockSpec((tr, tc), lambda i, j: (i, xb0 + 2 * nb + j)),
                  pl.BlockSpec((SC_CONV, tc), lambda i, j: (0, j)),
                  pl.BlockSpec(memory_space=pl.ANY)],
        out_specs=pl.BlockSpec((tr, tc), lambda i, j: (i, ob0 + j)),
        out_shape=jax.ShapeDtypeStruct(ymix.shape, ymix.dtype),
        input_output_aliases={4: 0},
        compiler_params=_cparams("parallel", "parallel"),
        name="short_conv",
    )(px, px, px, conv_w, ymix)


def _pool_kernel(u_ref, w_ref, sc_ref, ymix_ref, o_ref, pa_ref, pb_ref, *, n_tok, unit, pg):
    del ymix_ref
    n_pos = n_tok // unit
    reach = 8 * unit
    pad = -(-reach // SUBLANES) * SUBLANES
    mg = -(-4 * unit // SUBLANES) * SUBLANES
    tp = n_tok + 2 * pad
    zeros_mg = jnp.zeros((mg, pg), F32)
    zeros_pad = jnp.zeros((pad, pg), F32)
    for buf in (pa_ref, pb_ref):
        buf[0:mg, :] = zeros_mg
        buf[mg + tp:mg + tp + mg, :] = zeros_mg
    row = lax.broadcasted_iota(jnp.int32, (n_tok, 1), 0) // unit
    for g, win in enumerate(POOL_WINDOWS):
        cs = slice(g * pg, (g + 1) * pg)
        u = u_ref[:, cs]
        pa_ref[mg:mg + pad, :] = zeros_pad
        pa_ref[mg + pad:mg + pad + n_tok, :] = u
        pa_ref[mg + pad + n_tok:mg + tp, :] = zeros_pad
        src, dst = pa_ref, pb_ref
        dst[mg:mg + tp, :] = src[mg - unit:mg - unit + tp, :] + src[mg:mg + tp, :]
        src, dst = dst, src
        step, w = 1, 2
        while w < win:
            dst[mg:mg + tp, :] = (src[mg - step * unit:mg - step * unit + tp, :]
                                  + src[mg + step * unit:mg + step * unit + tp, :])
            src, dst = dst, src
            step, w = step * 2, w * 2
        total = src[mg + pad:mg + pad + n_tok, :]
        lo = jnp.maximum(row - win // 2, 0)
        hi = jnp.minimum(row - win // 2 + win, n_pos)
        count = (hi - lo).astype(F32)
        dlt = total / count - u
        y = _dot(dlt.astype(BF16), w_ref[g]) * sc_ref[:, cs]
        o_ref[:, cs] = y.astype(o_ref.dtype)


def _pool(px, pool_w, pool_scale, ymix, *, u_col0, out_col0, n_tok, unit, blk0, n_batch, name):
    width = pool_scale.shape[1]
    pg = width // len(POOL_WINDOWS)
    reach = 8 * unit
    pad = -(-reach // SUBLANES) * SUBLANES
    mg = -(-4 * unit // SUBLANES) * SUBLANES
    buf_rows = n_tok + 2 * pad + 2 * mg
    ub = u_col0 // width
    ob = out_col0 // width
    return pl.pallas_call(
        functools.partial(_pool_kernel, n_tok=n_tok, unit=unit, pg=pg),
        grid=(n_batch,),
        in_specs=[pl.BlockSpec((n_tok, width), lambda b: (blk0 + b, ub)),
                  pl.BlockSpec(pool_w.shape, lambda b: (0, 0, 0)),
                  pl.BlockSpec((1, width), lambda b: (0, 0)),
                  pl.BlockSpec(memory_space=pl.ANY)],
        out_specs=pl.BlockSpec((n_tok, width), lambda b: (blk0 + b, ob)),
        out_shape=jax.ShapeDtypeStruct(ymix.shape, ymix.dtype),
        input_output_aliases={3: 0},
        scratch_shapes=[pltpu.VMEM((buf_rows, pg), F32), pltpu.VMEM((buf_rows, pg), F32)],
        compiler_params=_cparams("parallel"),
        name=name,
    )(px, pool_w, pool_scale, ymix)


def _ba_permutation(n_heads):
    perm = np.zeros(4 * n_heads, np.int32)
    for p in range(n_heads // 2):
        for d in range(2):
            for s in range(2):
                for kind in range(2):
                    perm[p * 8 + d * 4 + s * 2 + kind] = d * 2 * n_heads + kind * n_heads + 2 * p + s
    return perm


def _gate_param_row(param, n_heads):
    row = jnp.zeros((LANES,), F32)
    idx, src_d, src_h = [], [], []
    for p in range(n_heads // 2):
        for d in range(2):
            for s in range(2):
                idx.append(p * 8 + d * 4 + s * 2 + 1)
                src_d.append(d)
                src_h.append(2 * p + s)
    row = row.at[np.array(idx)].set(param[np.array(src_d), np.array(src_h)])
    return row.reshape(1, LANES)


def _pick_tile(n, prefs):
    for t in prefs:
        if n % t == 0:
            return t
    raise ValueError(f"no tile for {n}")


def kernel(x, c, ctx, c_ctx, w_mod, b_mod, norm_ffn1, w_ffn1_gu, w_ffn1_down, norm_mix, w_in, conv_qkv, a_log,
           dt_bias, gdn_norm, conv_short, pool_w, pool_scale, w_out, norm_ffn2, w_ffn2_gu, w_ffn2_down, norm_final):
    B, S, D = x.shape
    CL = ctx.shape[1]
    L = w_mod.shape[0]
    n_vheads = a_log.shape[2]
    qk_dim = (n_vheads // 2) * HEAD_DIM
    v_dim = n_vheads * HEAD_DIM
    sc_dim = conv_short.shape[2]
    pool_dim = pool_scale.shape[1]
    n_ba = 4 * n_vheads
    n_pairs = n_vheads // 2
    bs_rows = B * S
    R = bs_rows + B * CL
    assert S % CHUNK == 0 and CL % CHUNK == 0 and S % CL == 0 and n_ba <= LANES

    tm = _pick_tile(math.gcd(S, R), (1024, 512, 256, 128))
    tg = _pick_tile(math.gcd(S, R), (512, 256, 128))
    tr = min(256, CL)
    assert (B * CL) % tm == 0
    tn = 512 if D % 512 == 0 else 256
    tn_ff = 256
    tc_qk = _pick_tile(qk_dim, (1024, 512, 256, 128))
    tc_v = math.gcd(v_dim, 2 * tc_qk)

    z_col0 = 2 * qk_dim + v_dim
    ba_col0 = z_col0 + v_dim
    sc_col0_src = ba_col0 + n_ba
    sc_col0 = ba_col0
    pool_col0 = sc_col0 + 3 * sc_dim
    ba_perm = _ba_permutation(n_vheads)
    onehot_t = np.zeros((LANES, n_ba), np.float32)
    onehot_t[np.arange(n_ba), ba_perm] = 1.0
    ba_onehot_t = jnp.asarray(onehot_t, dtype=BF16)
    w_main_t, w_ba_t_all = _w_in_prep(jnp.swapaxes(w_in, 1, 2), tile=tn, ba_row0=ba_col0, n_ba=n_ba)

    h = (x.reshape(bs_rows, D), ctx.reshape(B * CL, D))
    c_all = jnp.zeros((SUBLANES, D), F32).at[0:B].set(c).at[B].set(c_ctx)
    mods_all = _modulation(c_all, w_mod, b_mod.reshape(L, 1, N_MOD * D)).reshape(L, SUBLANES, N_MOD, D)

    common = dict(seq_rows=S, n_batch=B)
    seg = dict(bs_rows=bs_rows, seq=S, ctx_len=CL)
    whole = lambda w: [(0, w.shape[2])]
    w1gu = w_ffn1_gu[0].astype(BF16)
    for l in range(L):
        n_live = bs_rows if l == L - 1 else R
        mods = mods_all[l]

        hn = _adaln(h, norm_ffn1[l].reshape(1, D), mods, n_rows=R, shift_idx=0, scale_idx=1, tr=tg,
                    name="ffn1_adaln", **common)
        act, (w1d,) = _matmul(hn, w1gu, swiglu=True, out_dtype=BF16, tm=tm, tn=tn, name="ffn1_up",
                              casts=[(w_ffn1_down, l, whole(w_ffn1_down), tn_ff)])
        h, _ = _matmul_residual(act, w1d, h, mods, n_rows=R, gate_idx=2, scale=0.5, tm=tm, name="ffn1_down",
                                **common)
        w_ba_t = jnp.dot(ba_onehot_t, w_ba_t_all[l], preferred_element_type=F32).astype(BF16)

        hn = _adaln(h, norm_mix[l].reshape(1, D), mods, n_rows=R, shift_idx=3, scale_idx=4, tr=tg,
                    name="mix_adaln", **common)
        px, (w2gu, wo) = _matmul(hn, (w_main_t, l), swiglu=False, out_dtype=F32, tm=tm, tn=tn, name="in_proj",
                                 w_rows_out=True,
                                 casts=[(w_ffn2_gu, l, whole(w_ffn2_gu), tn), (w_out, l, whole(w_out), tn)])
        ba, _ = _matmul(hn, w_ba_t, swiglu=False, out_dtype=F32, tm=tm, tn=LANES, name="in_proj_ba",
                        w_rows_out=True)

        conv = dict(tr=tr, **seg)
        (qn,) = _qkv_conv(px, conv_qkv[l][:, 0:qk_dim], kind='q', col0=0, width=qk_dim, tc=tc_qk, **conv)
        kn, knt = _qkv_conv(px, conv_qkv[l][:, qk_dim:2 * qk_dim], kind='k', col0=qk_dim, width=qk_dim,
                            tc=tc_qk, **conv)
        (vn,) = _qkv_conv(px, conv_qkv[l][:, 2 * qk_dim:], kind='v', col0=2 * qk_dim, width=v_dim,
                          tc=tc_v, **conv)
        cols, rows = _gdn_gates(ba, _gate_param_row(a_log[l], n_vheads), _gate_param_row(dt_bias[l], n_vheads),
                                n_pairs=n_pairs, tm=tg)
        ymix, y_ctx = _gdn(qn, kn, knt, vn, cols, rows, px, z_col0, gdn_norm[l].reshape(1, HEAD_DIM),
                           n_batch=B, seq=S, ctx_len=CL, out_cols=v_dim + sc_dim + pool_dim)
        if l + 1 < L:
            ymix = _place_context_rows(y_ctx, ymix, row_blk0=bs_rows // CL, blk_rows=CL,
                                       blk_cols=math.gcd(v_dim, 4 * LANES))

        ymix = _shortconv(px, conv_short[l], ymix, x_col0=sc_col0, width=sc_dim, out_col0=v_dim,
                          tr=tr, tc=_pick_tile(sc_dim, (1024, 512, 256, 128)), **seg)
        pw = pool_w[l].astype(BF16)
        psc = pool_scale[l].reshape(1, pool_dim)
        ymix = _pool(px, pw, psc, ymix, u_col0=pool_col0, out_col0=v_dim + sc_dim, n_tok=S, unit=GRID_W,
                     blk0=0, n_batch=B, name="pool_latent")
        ymix = _pool(px, pw, psc, ymix, u_col0=pool_col0, out_col0=v_dim + sc_dim, n_tok=CL, unit=1,
                     blk0=bs_rows // CL, n_batch=B, name="pool_context")
        h, _ = _matmul_residual(ymix, wo, h, mods, n_rows=n_live, gate_idx=5, scale=1.0, tm=tm,
                                name="mix_out", **common)

        hn = _adaln(h, norm_ffn2[l].reshape(1, D), mods, n_rows=n_live, shift_idx=6, scale_idx=7, tr=tg,
                    name="ffn2_adaln", **common)
        act, (w2d,) = _matmul(hn, w2gu, swiglu=True, out_dtype=BF16, tm=tm, name="ffn2_up",
                              casts=[(w_ffn2_down, l, whole(w_ffn2_down), tn_ff)])
        next_casts = [(w_ffn1_gu, l + 1, whole(w_ffn1_gu), tn)] if l + 1 < L else []
        h, nxt = _matmul_residual(act, w2d, h, mods, n_rows=n_live, gate_idx=8, scale=0.5, tm=tm,
                                  name="ffn2_down", **common, casts=next_casts)
        if nxt:
            w1gu = nxt[0]

    out = _final_rmsnorm(h, norm_final.reshape(1, D), bs_rows, tg)
    return out.reshape(B, S, D)
```

```python
import functools
import math

import numpy as np
import jax
import jax.numpy as jnp
from jax import lax
from jax.experimental import pallas as pl
from jax.experimental.pallas import tpu as pltpu

EPS = 1e-6
N_MOD = 9
GRID_W = 64
HEAD_DIM = 128
GDN_CONV = 5
SC_CONV = 3
POOL_WINDOWS = (2, 4, 8, 16)
CHUNK = 128
INV_BLOCK = 16
LANES = 128
SUBLANES = 8
BF16_SUBLANES = 16
VMEM_LIMIT = 56 * 1024 * 1024

BF16 = jnp.bfloat16
F32 = jnp.float32


def _cparams(*sem):
    return pltpu.CompilerParams(dimension_semantics=sem, vmem_limit_bytes=VMEM_LIMIT)


def _dot(a, b):
    return jnp.dot(a, b, preferred_element_type=F32)


def _dot_hi(a, b):
    return jnp.dot(a, b, preferred_element_type=F32, precision=lax.Precision.HIGHEST)


def _silu(v):
    return v * jax.nn.sigmoid(v)


def _mod_kernel(c_ref, w_ref, b_ref, o_ref):
    a = _silu(c_ref[...]).astype(BF16)
    o_ref[...] = _dot(a, w_ref[...].astype(BF16)) + b_ref[...]


def _modulation(c_all, w_mod, b_mod, tn=1024):
    L, D, N = w_mod.shape
    return pl.pallas_call(
        _mod_kernel,
        grid=(L, N // tn),
        in_specs=[pl.BlockSpec((SUBLANES, D), lambda l, j: (0, 0)),
                  pl.BlockSpec((None, D, tn), lambda l, j: (l, 0, j)),
                  pl.BlockSpec((None, 1, tn), lambda l, j: (l, 0, j))],
        out_specs=pl.BlockSpec((None, SUBLANES, tn), lambda l, j: (l, 0, j)),
        out_shape=jax.ShapeDtypeStruct((L, SUBLANES, N), F32),
        compiler_params=_cparams("parallel", "parallel"),
        name="modulation",
    )(c_all, w_mod, b_mod)


def _two_part_specs(parts, block, n_first, col_of):
    first = pl.BlockSpec(block, lambda i, *j: (jnp.minimum(i, n_first - 1), *col_of(i < n_first, *j)))
    second = pl.BlockSpec(block, lambda i, *j: (jnp.maximum(i - n_first, 0), *col_of(i >= n_first, *j)))
    return [first, second]


def _adaln_kernel(*refs, shift_idx, scale_idx, n_first):
    if n_first is None:
        h_ref, nw_ref, mod_ref, o_ref = refs
        h = h_ref[...]
    else:
        x_ref, c_ref, nw_ref, mod_ref, o_ref = refs
        h = jnp.where(pl.program_id(0) < n_first, x_ref[...], c_ref[...])
    ms = jnp.mean(h * h, axis=-1, keepdims=True)
    y = h * lax.rsqrt(ms + EPS) * nw_ref[...]
    shift = mod_ref[0, shift_idx:shift_idx + 1, :]
    scale = mod_ref[0, scale_idx:scale_idx + 1, :]
    o_ref[...] = (y * (1.0 + scale) + shift).astype(o_ref.dtype)


def _adaln(h, norm_w, mods, *, n_rows, seq_rows, n_batch, shift_idx, scale_idx, tr, name):
    parts = h if isinstance(h, tuple) else (h,)
    D = parts[0].shape[1]
    mod_map = lambda i: (jnp.minimum((i * tr) // seq_rows, n_batch), 0, 0)
    if len(parts) == 2:
        n_first = parts[0].shape[0] // tr
        h_specs = _two_part_specs(parts, (tr, D), n_first, lambda active: (0,))
    else:
        n_first = None
        h_specs = [pl.BlockSpec((tr, D), lambda i: (i, 0))]
    return pl.pallas_call(
        functools.partial(_adaln_kernel, shift_idx=shift_idx, scale_idx=scale_idx, n_first=n_first),
        grid=(n_rows // tr,),
        in_specs=h_specs + [pl.BlockSpec((1, D), lambda i: (0, 0)),
                            pl.BlockSpec((1, N_MOD, D), mod_map)],
        out_specs=pl.BlockSpec((tr, D), lambda i: (i, 0)),
        out_shape=jax.ShapeDtypeStruct((n_rows, D), BF16),
        compiler_params=_cparams("parallel"),
        name=name,
    )(*parts, norm_w, mods)


def _cast_row_block(rows, steps):
    for rps in range(BF16_SUBLANES, rows + 1, BF16_SUBLANES):
        if rows % rps == 0 and rows // rps <= steps:
            return rps
    raise ValueError(f"cannot spread {rows} rows over {steps} steps")


def _cast_plan(casts, n_i, nj):
    in_specs, out_specs, out_shapes, args, plans = [], [], [], [], []
    for src, layer, ranges, tile in casts:
        _, rows, cols = src.shape
        rps = _cast_row_block(rows, n_i * nj)
        last = rows // rps - 1
        out_cols = sum(b - a for a, b in ranges)
        src_map = lambda i, j, last=last, layer=layer: (layer, jnp.minimum(i * nj + j, last), 0)
        in_specs.append(pl.BlockSpec((None, rps, cols), src_map))
        if tile is None:
            out_specs.append(pl.BlockSpec((rps, out_cols), lambda i, j, last=last: (jnp.minimum(i * nj + j, last), 0)))
            out_shapes.append(jax.ShapeDtypeStruct((rows, out_cols), BF16))
            pieces = tuple((None, a, b) for a, b in ranges)
        else:
            assert all((b - a) % tile == 0 for a, b in ranges)
            nt = out_cols // tile
            out_specs.append(pl.BlockSpec((nt, rps, tile),
                                          lambda i, j, last=last: (0, jnp.minimum(i * nj + j, last), 0)))
            out_shapes.append(jax.ShapeDtypeStruct((nt, rows, tile), BF16))
            pieces = tuple((None, s, s + tile) for a, b in ranges for s in range(a, b, tile))
            pieces = tuple((t, a, b) for t, (_, a, b) in enumerate(pieces))
        args.append(src)
        plans.append(pieces)
    return in_specs, out_specs, out_shapes, args, plans


def _run_casts(src_refs, dst_refs, plans):
    for src_ref, dst_ref, pieces in zip(src_refs, dst_refs, plans):
        off = 0
        for t, a, b in pieces:
            if t is None:
                dst_ref[:, off:off + b - a] = src_ref[:, a:b].astype(dst_ref.dtype)
                off += b - a
            else:
                dst_ref[t] = src_ref[:, a:b].astype(dst_ref.dtype)


def _row_tile(a_hbm, buf_ref, sem_ref):
    i, j = pl.program_id(0), pl.program_id(1)
    tm = buf_ref.shape[1]
    slot = lax.rem(i, 2)

    def fetch(tile, s):
        rows = pl.ds(pl.multiple_of(tile * tm, tm), tm)
        return pltpu.make_async_copy(a_hbm.at[rows, :], buf_ref.at[s], sem_ref.at[s])

    @pl.when(jnp.logical_and(i == 0, j == 0))
    def _():
        fetch(0, 0).start()

    @pl.when(j == 0)
    def _():
        fetch(i, slot).wait()

        @pl.when(i + 1 < pl.num_programs(0))
        def _():
            fetch(i + 1, 1 - slot).start()

    return buf_ref[slot]


def _row_tile_scratch(tm, k):
    return [pltpu.VMEM((2, tm, k), BF16), pltpu.SemaphoreType.DMA((2,))]


def _mm_kernel(a_hbm, *rest, swiglu, cast_ranges, w_rows_out):
    nc = len(cast_ranges)
    n_w = 2 if swiglu else 1
    w_refs, src_refs = rest[:n_w], rest[n_w:n_w + nc]
    o_ref, dst_refs = rest[n_w + nc], rest[n_w + nc + 1:-2]
    a = _row_tile(a_hbm, *rest[-2:])
    if swiglu:
        half = o_ref.shape[1] // 2
        for cs in (slice(0, half), slice(half, 2 * half)):
            g = _dot(a, w_refs[0][:, cs])
            u = _dot(a, w_refs[1][:, cs])
            o_ref[:, cs] = (_silu(g) * u).astype(o_ref.dtype)
    elif w_rows_out:
        acc = lax.dot_general(a, w_refs[0][...], (((1,), (1,)), ((), ())), preferred_element_type=F32)
        o_ref[...] = acc.astype(o_ref.dtype)
    else:
        o_ref[...] = _dot(a, w_refs[0][...]).astype(o_ref.dtype)
    _run_casts(src_refs, dst_refs, cast_ranges)


def _weight_specs(w, tn, swiglu, w_rows_out):
    if w_rows_out:
        assert not swiglu
        if isinstance(w, tuple):
            stack, layer = w
            return [pl.BlockSpec((None, tn, stack.shape[2]), lambda i, j: (layer, j, 0))], stack.shape[1] // tn, tn
        return [pl.BlockSpec((tn, w.shape[1]), lambda i, j: (j, 0))], w.shape[0] // tn, tn
    if w.ndim == 2:
        k, nt = w.shape[0], w.shape[1] // tn
        nj = nt // 2 if swiglu else nt
        specs = [pl.BlockSpec((k, tn), lambda i, j: (0, j))]
        if swiglu:
            specs.append(pl.BlockSpec((k, tn), lambda i, j: (0, j + nj)))
    else:
        nt, k, tn = w.shape
        nj = nt // 2 if swiglu else nt
        specs = [pl.BlockSpec((None, k, tn), lambda i, j: (j, 0, 0))]
        if swiglu:
            specs.append(pl.BlockSpec((None, k, tn), lambda i, j: (j + nj, 0, 0)))
    return specs, nj, tn


def _matmul(a, w, *, swiglu, out_dtype, tm, name, tn=None, casts=(), w_rows_out=False):
    R, K = a.shape
    w_specs, nj, tn = _weight_specs(w, tn, swiglu, w_rows_out)
    n_out = nj * tn
    in_specs = [pl.BlockSpec(memory_space=pl.ANY)] + w_specs
    args = [a] + [w[0] if isinstance(w, tuple) else w] * len(w_specs)
    c_in, c_out, c_shapes, c_args, c_ranges = _cast_plan(casts, R // tm, nj)
    res = pl.pallas_call(
        functools.partial(_mm_kernel, swiglu=swiglu, cast_ranges=c_ranges, w_rows_out=w_rows_out),
        grid=(R // tm, nj),
        in_specs=in_specs + c_in,
        out_specs=[pl.BlockSpec((tm, tn), lambda i, j: (i, j))] + c_out,
        out_shape=[jax.ShapeDtypeStruct((R, n_out), out_dtype)] + c_shapes,
        scratch_shapes=_row_tile_scratch(tm, K),
        compiler_params=_cparams("arbitrary", "arbitrary"),
        name=name,
    )(*args, *c_args)
    return res[0], res[1:]


def _w_in_prep_kernel(src_hbm, ba_ref, o_ref, oba_ref, buf_ref, sem_ref, *, tile, split_tile, skip):
    l, j = pl.program_id(0), pl.program_id(1)
    n_j = pl.num_programs(1)
    step = l * n_j + j
    slot = lax.rem(step, 2)

    def fetch(ll, jj, s):
        row0 = jj * tile + jnp.where(jj >= split_tile, skip, 0)
        rows = pl.ds(pl.multiple_of(row0, SUBLANES), tile)
        return pltpu.make_async_copy(src_hbm.at[ll, rows, :], buf_ref.at[s], sem_ref.at[s])

    @pl.when(step == 0)
    def _():
        fetch(0, 0, 0).start()

    fetch(l, j, slot).wait()

    @pl.when(step + 1 < pl.num_programs(0) * n_j)
    def _():
        wrap = j + 1 == n_j
        fetch(jnp.where(wrap, l + 1, l), jnp.where(wrap, 0, j + 1), 1 - slot).start()

    o_ref[...] = buf_ref[slot].astype(o_ref.dtype)
    oba_ref[...] = ba_ref[...].astype(oba_ref.dtype)


def _w_in_prep(w_in_t, *, tile, ba_row0, n_ba):
    L, d_in, K = w_in_t.shape
    n_main = d_in - n_ba
    assert ba_row0 % tile == 0 and n_main % tile == 0 and ba_row0 % n_ba == 0 and n_ba % SUBLANES == 0
    return pl.pallas_call(
        functools.partial(_w_in_prep_kernel, tile=tile, split_tile=ba_row0 // tile, skip=n_ba),
        grid=(L, n_main // tile),
        in_specs=[pl.BlockSpec(memory_space=pl.ANY),
                  pl.BlockSpec((None, n_ba, K), lambda l, j: (l, ba_row0 // n_ba, 0))],
        out_specs=[pl.BlockSpec((None, tile, K), lambda l, j: (l, j, 0)),
                   pl.BlockSpec((None, n_ba, K), lambda l, j: (l, 0, 0))],
        out_shape=[jax.ShapeDtypeStruct((L, n_main, K), BF16), jax.ShapeDtypeStruct((L, n_ba, K), BF16)],
        scratch_shapes=[pltpu.VMEM((2, tile, K), F32), pltpu.SemaphoreType.DMA((2,))],
        compiler_params=_cparams("arbitrary", "arbitrary"),
        name="w_in_prep",
    )(w_in_t, w_in_t)


def _mm_res_kernel(a_hbm, w_ref, *rest, gate_idx, scale, cast_ranges, n_first):
    nc = len(cast_ranges)
    a = _row_tile(a_hbm, *rest[-2:])
    rest = rest[:-2]
    if n_first is None:
        res = rest[0][...]
        rest = rest[1:]
    else:
        res = jnp.where(pl.program_id(0) < n_first, rest[0][...], rest[1][...])
        rest = rest[2:]
    mod_ref, src_refs, o_ref, dst_refs = rest[0], rest[1:1 + nc], rest[1 + nc], rest[2 + nc:]
    acc = _dot(a, w_ref[...])
    gate = mod_ref[0, gate_idx:gate_idx + 1, :]
    o_ref[...] = res + (scale * gate) * acc
    _run_casts(src_refs, dst_refs, cast_ranges)


def _matmul_residual(a, w, res, mods, *, n_rows, seq_rows, n_batch, gate_idx, scale, tm, name, casts=()):
    K = a.shape[1]
    nt, _, tn = w.shape
    D = nt * tn
    mod_map = lambda i, j: (jnp.minimum((i * tm) // seq_rows, n_batch), 0, j)
    parts = res if isinstance(res, tuple) else (res,)
    if len(parts) == 2:
        n_first = parts[0].shape[0] // tm
        res_specs = _two_part_specs(parts, (tm, tn), n_first, lambda active, j: (jnp.where(active, j, 0),))
    else:
        n_first = None
        res_specs = [pl.BlockSpec((tm, tn), lambda i, j: (i, j))]
    c_in, c_out, c_shapes, c_args, c_ranges = _cast_plan(casts, n_rows // tm, D // tn)
    out = pl.pallas_call(
        functools.partial(_mm_res_kernel, gate_idx=gate_idx, scale=scale, cast_ranges=c_ranges, n_first=n_first),
        grid=(n_rows // tm, D // tn),
        in_specs=[pl.BlockSpec(memory_space=pl.ANY),
                  pl.BlockSpec((None, K, tn), lambda i, j: (j, 0, 0))] + res_specs
                 + [pl.BlockSpec((1, N_MOD, tn), mod_map)] + c_in,
        out_specs=[pl.BlockSpec((tm, tn), lambda i, j: (i, j))] + c_out,
        out_shape=[jax.ShapeDtypeStruct((n_rows, D), F32)] + c_shapes,
        scratch_shapes=_row_tile_scratch(tm, K),
        compiler_params=_cparams("arbitrary", "arbitrary"),
        name=name,
    )(a, w, *parts, mods, *c_args)
    return out[0], out[1:]


def _rmsnorm_kernel(h_ref, w_ref, o_ref):
    h = h_ref[...]
    ms = jnp.mean(h * h, axis=-1, keepdims=True)
    o_ref[...] = h * lax.rsqrt(ms + EPS) * w_ref[...]


def _final_rmsnorm(h, w, rows, tm):
    D = h.shape[1]
    return pl.pallas_call(
        _rmsnorm_kernel,
        grid=(rows // tm,),
        in_specs=[pl.BlockSpec((tm, D), lambda i: (i, 0)),
                  pl.BlockSpec((1, D), lambda i: (0, 0))],
        out_specs=pl.BlockSpec((tm, D), lambda i: (i, 0)),
        out_shape=jax.ShapeDtypeStruct((rows, D), F32),
        compiler_params=_cparams("parallel"),
        name="final_rmsnorm",
    )(h, w)


def _segment_position(tile_rows, bs_rows, seq, ctx_len):
    row0 = pl.program_id(0) * tile_rows
    in_x = row0 < bs_rows
    off = jnp.where(in_x, lax.rem(row0, seq), lax.rem(row0 - bs_rows, ctx_len))
    seglen = jnp.where(in_x, seq, ctx_len)
    return in_x, off, seglen


def _qkv_conv_kernel(cur_ref, prev_ref, next_ref, w_ref, *rest, kind, tr, bs_rows, seq, ctx_len):
    if kind == 'k':
        o_ref, ot_ref, ext_ref = rest
    else:
        o_ref, ext_ref = rest
    _, off, seglen = _segment_position(tr, bs_rows, seq, ctx_len)
    is_start = off == 0
    is_end = off + tr == seglen
    pad = SUBLANES
    half = GDN_CONV // 2
    ext_ref[0:pad, :] = jnp.where(is_start, 0.0, prev_ref[...])
    ext_ref[pad:pad + tr, :] = cur_ref[...]
    ext_ref[pad + tr:pad + tr + pad, :] = jnp.where(is_end, 0.0, next_ref[...])
    ext = ext_ref[...]
    n_ext = tr + 2 * pad
    acc = None
    for j in range(GDN_CONV):
        shifted = ext if j == half else pltpu.roll(ext, (half - j) % n_ext, axis=0)
        term = shifted[pad:pad + tr, :] * w_ref[j:j + 1, :]
        acc = term if acc is None else acc + term
    y = _silu(acc)
    tc = y.shape[1]
    if kind == 'v':
        o_ref[...] = y.astype(o_ref.dtype)
        return
    mult = HEAD_DIM ** -0.5 if kind == 'q' else 1.0
    for hd in range(tc // HEAD_DIM):
        yh = y[:, hd * HEAD_DIM:(hd + 1) * HEAD_DIM]
        inv = lax.rsqrt(jnp.sum(yh * yh, axis=-1, keepdims=True) + EPS)
        yn = yh * inv
        if kind == 'q':
            yn = yn * mult
        o_ref[:, hd * HEAD_DIM:(hd + 1) * HEAD_DIM] = yn.astype(o_ref.dtype)
        if kind == 'k':
            ot_ref[hd * HEAD_DIM:(hd + 1) * HEAD_DIM, :] = yn.T.astype(ot_ref.dtype)


def _qkv_conv(px, conv_w, *, kind, col0, width, tr, tc, bs_rows, seq, ctx_len):
    R = px.shape[0]
    cb0 = col0 // tc
    rb = tr // SUBLANES
    last = R // SUBLANES - 1
    in_specs = [pl.BlockSpec((tr, tc), lambda i, j: (i, cb0 + j)),
                pl.BlockSpec((SUBLANES, tc), lambda i, j: (jnp.maximum(i * rb - 1, 0), cb0 + j)),
                pl.BlockSpec((SUBLANES, tc), lambda i, j: (jnp.minimum((i + 1) * rb, last), cb0 + j)),
                pl.BlockSpec((GDN_CONV, tc), lambda i, j: (0, j))]
    out_specs = [pl.BlockSpec((tr, tc), lambda i, j: (i, j))]
    out_shape = [jax.ShapeDtypeStruct((R, width), BF16)]
    if kind == 'k':
        out_specs.append(pl.BlockSpec((tc, tr), lambda i, j: (j, i)))
        out_shape.append(jax.ShapeDtypeStruct((width, R), BF16))
    res = pl.pallas_call(
        functools.partial(_qkv_conv_kernel, kind=kind, tr=tr, bs_rows=bs_rows, seq=seq, ctx_len=ctx_len),
        grid=(R // tr, width // tc),
        in_specs=in_specs,
        out_specs=out_specs,
        out_shape=out_shape,
        scratch_shapes=[pltpu.VMEM((tr + 2 * SUBLANES, tc), F32)],
        compiler_params=_cparams("parallel", "parallel"),
        name="gdn_conv_" + kind,
    )(px, px, px, conv_w)
    return res


def _gate_kernel(ba_ref, alog_ref, dtb_ref, cols_ref, rows_ref, *, tm, n_pairs):
    lane = lax.broadcasted_iota(jnp.int32, (CHUNK, LANES), 1)
    is_beta = (lane & 1) == 0
    is_bwd = (lane & 4) != 0
    ri = lax.broadcasted_iota(jnp.int32, (CHUNK, CHUNK), 0)
    ci = lax.broadcasted_iota(jnp.int32, (CHUNK, CHUNK), 1)
    tri_lo = jnp.where(ci <= ri, 1.0, 0.0)
    tri_up = jnp.where(ci >= ri, 1.0, 0.0)
    neg_a = -jnp.exp(alog_ref[...])
    for c in range(tm // CHUNK):
        sl = slice(c * CHUNK, (c + 1) * CHUNK)
        x = ba_ref[sl, :]
        beta = jax.nn.sigmoid(x)
        y = x + dtb_ref[...]
        softplus = jnp.maximum(y, 0.0) + jnp.log1p(jnp.exp(-jnp.abs(y)))
        g = jnp.where(is_beta, 0.0, neg_a * softplus)
        gc = jnp.where(is_bwd, _dot_hi(tri_up, g), _dot_hi(tri_lo, g))
        full = jnp.where(is_beta, beta, gc)
        for p in range(n_pairs):
            cols_ref[p, sl, :] = full[:, p * 8:(p + 1) * 8]
        rows_ref[:, sl] = full.T[0:n_pairs * 8, :]


def _gdn_gates(ba, alog_row, dtb_row, *, n_pairs, tm):
    R = ba.shape[0]
    return pl.pallas_call(
        functools.partial(_gate_kernel, tm=tm, n_pairs=n_pairs),
        grid=(R // tm,),
        in_specs=[pl.BlockSpec((tm, LANES), lambda i: (i, 0)),
                  pl.BlockSpec((1, LANES), lambda i: (0, 0)),
                  pl.BlockSpec((1, LANES), lambda i: (0, 0))],
        out_specs=[pl.BlockSpec((n_pairs, tm, 8), lambda i: (0, i, 0)),
                   pl.BlockSpec((n_pairs * 8, tm), lambda i: (0, i))],
        out_shape=[jax.ShapeDtypeStruct((n_pairs, R, 8), F32),
                   jax.ShapeDtypeStruct((n_pairs * 8, R), F32)],
        compiler_params=_cparams("parallel"),
        name="gdn_gates",
    )(ba, alog_row, dtb_row)


def _gdn_kernel(qx_ref, qc_ref, kx_ref, kc_ref, ktx_ref, ktc_ref, vx_ref, vc_ref,
                colx_ref, colc_ref, rowx_ref, rowc_ref, zx_ref, zc_ref, nw_ref, ox_ref, oc_ref,
                t_ref, a_ref, of_ref, ob_ref, sf_ref, sb_ref, *, seq, ctx_len, rb, n_pair, prep_chunks):
    C = CHUNK
    HD = HEAD_DIM
    n_xc = seq // C
    n_cc = ctx_len // C
    x_refs = (qx_ref, kx_ref, ktx_ref, vx_ref, colx_ref, rowx_ref)
    c_refs = (qc_ref, kc_ref, ktc_ref, vc_ref, colc_ref, rowc_ref)

    ri = lax.broadcasted_iota(jnp.int32, (C, C), 0)
    ci = lax.broadcasted_iota(jnp.int32, (C, C), 1)
    eye = (ri == ci).astype(F32)

    def lane_idx(d, s, kind):
        return d * 4 + s * 2 + kind

    block_diff = ri ^ ci

    def unit_triangular_inverses(nmats):
        base = int(math.log2(INV_BLOCK))
        n0s = [jnp.where((block_diff >> base) == 0, n, 0.0) for n in nmats]
        invs = [eye + n0 for n0 in n0s]
        pws = n0s
        for _ in range(base - 1):
            pwbs = [pw.astype(BF16) for pw in pws]
            pws = [_dot(b, b) for b in pwbs]
            invs = [inv + _dot(inv.astype(BF16), pw.astype(BF16)) for inv, pw in zip(invs, pws)]
        for level in range(base, int(math.log2(C))):
            offs = [jnp.where((block_diff >> level) == 1, n, 0.0).astype(BF16) for n in nmats]
            invbs = [inv.astype(BF16) for inv in invs]
            halves = [_dot(ib, off).astype(BF16) for ib, off in zip(invbs, offs)]
            invs = [inv + _dot(h, ib) for inv, h, ib in zip(invs, halves, invbs)]
        return invs

    def prepare(chunks):
        nmats, dests = [], []
        for refs, row0, slot in chunks:
            q_ref, k_ref, kt_ref, _, col_ref, row_ref = refs
            rows = pl.ds(row0, C)
            for u in range(n_pair):
                hs = slice(u * HD, (u + 1) * HD)
                qk = jnp.concatenate([q_ref[rows, hs], k_ref[rows, hs]], axis=0)
                gram = _dot(qk, kt_ref[hs, rows])
                qkt = gram[0:C, :]
                kkt = gram[C:2 * C, :]
                cols = col_ref[u, rows, :]
                for d in range(2):
                    incl = (ci <= ri) if d == 0 else (ci >= ri)
                    strict = (ci < ri) if d == 0 else (ci > ri)
                    for s in range(2):
                        kb, kg = lane_idx(d, s, 0), lane_idx(d, s, 1)
                        beta_col = cols[:, kb:kb + 1]
                        gc_col = cols[:, kg:kg + 1]
                        gc_row = row_ref[u * 8 + kg:u * 8 + kg + 1, rows]
                        decay = jnp.where(incl, jnp.exp(jnp.where(incl, gc_col - gc_row, 0.0)), 0.0)
                        nmats.append(jnp.where(strict, -(kkt * beta_col) * decay, 0.0))
                        a_ref[slot, u * 4 + d * 2 + s] = (qkt * decay).astype(BF16)
                        dests.append((slot, u * 4 + d * 2 + s))
        for (slot, idx), inv in zip(dests, unit_triangular_inverses(nmats)):
            t_ref[slot, idx] = inv.astype(BF16)

    def chain(tasks):
        units = [(d, u, refs, pl.ds(row0, C), slot, orow0)
                 for d, refs, row0, slot, orow0 in tasks for u in range(n_pair)]
        states, projs = [], []
        for d, u, refs, rows, _, _ in units:
            hs = slice(u * HD, (u + 1) * HD)
            state = (sf_ref if d == 0 else sb_ref)[u]
            qk = jnp.concatenate([refs[0][rows, hs], refs[1][rows, hs]], axis=0)
            states.append(state)
            projs.append(_dot(qk, state.astype(BF16)))
        scalars, rhss = [], []
        for (d, u, refs, rows, _, _), proj in zip(units, projs):
            cols = refs[4][u, rows, :]
            for s in range(2):
                kb, kg = lane_idx(d, s, 0), lane_idx(d, s, 1)
                beta_col = cols[:, kb:kb + 1]
                gc_col = cols[:, kg:kg + 1]
                g_last = gc_col[C - 1:C, :] if d == 0 else gc_col[0:1, :]
                e_col = jnp.exp(gc_col)
                vs = slice((2 * u + s) * HD, (2 * u + s + 1) * HD)
                v = refs[3][rows, vs].astype(F32)
                k_s = proj[C:2 * C, s * HD:(s + 1) * HD]
                rhss.append((beta_col * (v - e_col * k_s)).astype(BF16))
                scalars.append((gc_col, e_col, g_last))
        v_news = []
        for i, (d, u, _, _, slot, _) in enumerate(units):
            for s in range(2):
                v_news.append(_dot(t_ref[slot, u * 4 + d * 2 + s], rhss[2 * i + s]))
        for i, (d, u, refs, rows, slot, orow0) in enumerate(units):
            o_acc = of_ref if d == 0 else ob_ref
            evs, gls = [], []
            for s in range(2):
                gc_col, e_col, g_last = scalars[2 * i + s]
                v_new = v_news[2 * i + s]
                q_s = projs[i][0:C, s * HD:(s + 1) * HD]
                out = e_col * q_s + _dot(a_ref[slot, u * 4 + d * 2 + s], v_new.astype(BF16))
                o_acc[pl.ds(orow0, C), (2 * u + s) * HD:(2 * u + s + 1) * HD] = out
                evs.append(jnp.exp(g_last - gc_col) * v_new)
                gls.append(jnp.broadcast_to(jnp.exp(g_last), (HD, HD)))
            ev = jnp.concatenate(evs, axis=1).astype(BF16)
            new_state = states[i] * jnp.concatenate(gls, axis=1) + _dot(refs[2][u * HD:(u + 1) * HD, rows], ev)
            (sf_ref if d == 0 else sb_ref)[u] = new_state

    sf_ref[...] = jnp.zeros_like(sf_ref)
    sb_ref[...] = jnp.zeros_like(sb_ref)
    prepare([(c_refs, n * C, n) for n in range(n_cc)])

    def prep_body(it, carry):
        n0 = it * prep_chunks
        prepare([(x_refs, pl.multiple_of((n0 + j) * C, C), n_cc + n0 + j) for j in range(prep_chunks)])
        return carry
    lax.fori_loop(0, n_xc // prep_chunks, prep_body, 0)

    for n in range(n_cc):
        m = n_cc - 1 - n
        chain([(0, c_refs, n * C, n, seq + n * C), (1, c_refs, m * C, m, seq + m * C)])

    def chain_body(n, carry):
        m = n_xc - 1 - n
        rf = pl.multiple_of(n * C, C)
        rbk = pl.multiple_of(m * C, C)
        chain([(0, x_refs, rf, n_cc + n, rf), (1, x_refs, rbk, n_cc + m, rbk)])
        return carry
    lax.fori_loop(0, n_xc, chain_body, 0)

    def finish(z_ref, out_ref, base, n_blocks):
        def body(bi, carry):
            r0 = pl.multiple_of(bi * rb, rb)
            rows = pl.ds(r0, rb)
            acc_rows = pl.ds(pl.multiple_of(base + r0, rb), rb)
            o = of_ref[acc_rows, :] + ob_ref[acc_rows, :]
            z = z_ref[rows, :]
            for hd in range(2 * n_pair):
                hs = slice(hd * HD, (hd + 1) * HD)
                oh = o[:, hs]
                ms = jnp.mean(oh * oh, axis=-1, keepdims=True)
                y = oh * lax.rsqrt(ms + EPS) * nw_ref[...]
                out_ref[rows, hs] = (y * _silu(z[:, hs])).astype(out_ref.dtype)
            return carry
        lax.fori_loop(0, n_blocks, body, 0)

    finish(zx_ref, ox_ref, 0, seq // rb)
    finish(zc_ref, oc_ref, seq, ctx_len // rb)


def _gdn(qn, kn, knt, vn, cols, rows, px, z_col0, norm_w, *, n_batch, seq, ctx_len, out_cols):
    R = qn.shape[0]
    n_pairs = qn.shape[1] // HEAD_DIM
    C = CHUNK
    rb = min(256, ctx_len)
    cblk0 = n_batch * (seq // ctx_len)
    n_slots = (seq + ctx_len) // C
    n_pair = 2 if n_pairs % 2 == 0 else 1
    prep_chunks = 2 if (seq // C) % 2 == 0 else 1
    qw = n_pair * HEAD_DIM
    vw = 2 * qw
    zb0 = z_col0 // vw
    assert z_col0 % vw == 0

    x_rows = lambda w, c0=0: pl.BlockSpec((seq, w), lambda b, p: (b, c0 + p))
    c_rows = lambda w, c0=0: pl.BlockSpec((ctx_len, w), lambda b, p: (cblk0 + b, c0 + p))
    in_specs = [
        x_rows(qw), c_rows(qw),
        x_rows(qw), c_rows(qw),
        pl.BlockSpec((qw, seq), lambda b, p: (p, b)),
        pl.BlockSpec((qw, ctx_len), lambda b, p: (p, cblk0 + b)),
        x_rows(vw), c_rows(vw),
        pl.BlockSpec((n_pair, seq, 8), lambda b, p: (p, b, 0)),
        pl.BlockSpec((n_pair, ctx_len, 8), lambda b, p: (p, cblk0 + b, 0)),
        pl.BlockSpec((n_pair * 8, seq), lambda b, p: (p, b)),
        pl.BlockSpec((n_pair * 8, ctx_len), lambda b, p: (p, cblk0 + b)),
        x_rows(vw, zb0), c_rows(vw, zb0),
        pl.BlockSpec((1, HEAD_DIM), lambda b, p: (0, 0)),
    ]
    return pl.pallas_call(
        functools.partial(_gdn_kernel, seq=seq, ctx_len=ctx_len, rb=rb, n_pair=n_pair, prep_chunks=prep_chunks),
        grid=(n_batch, n_pairs // n_pair),
        in_specs=in_specs,
        out_specs=[pl.BlockSpec((seq, vw), lambda b, p: (b, p)),
                   pl.BlockSpec((ctx_len, vw), lambda b, p: (b, p))],
        out_shape=[jax.ShapeDtypeStruct((R, out_cols), BF16),
                   jax.ShapeDtypeStruct((n_batch * ctx_len, vn.shape[1]), BF16)],
        scratch_shapes=[pltpu.VMEM((n_slots, n_pair * 4, C, C), BF16),
                        pltpu.VMEM((n_slots, n_pair * 4, C, C), BF16),
                        pltpu.VMEM((seq + ctx_len, vw), F32),
                        pltpu.VMEM((seq + ctx_len, vw), F32),
                        pltpu.VMEM((n_pair, HEAD_DIM, 2 * HEAD_DIM), F32),
                        pltpu.VMEM((n_pair, HEAD_DIM, 2 * HEAD_DIM), F32)],
        compiler_params=_cparams("parallel", "parallel"),
        name="gdn_delta_rule",
    )(qn, qn, kn, kn, knt, knt, vn, vn, cols, cols, rows, rows, px, px, norm_w)


def _copy_kernel(src_ref, dst_hbm_ref, o_ref):
    del dst_hbm_ref
    o_ref[...] = src_ref[...]


def _place_context_rows(y_ctx, ymix, *, row_blk0, blk_rows, blk_cols):
    rows, cols = y_ctx.shape
    return pl.pallas_call(
        _copy_kernel,
        grid=(rows // blk_rows, cols // blk_cols),
        in_specs=[pl.BlockSpec((blk_rows, blk_cols), lambda i, j: (i, j)),
                  pl.BlockSpec(memory_space=pl.ANY)],
        out_specs=pl.BlockSpec((blk_rows, blk_cols), lambda i, j: (row_blk0 + i, j)),
        out_shape=jax.ShapeDtypeStruct(ymix.shape, ymix.dtype),
        input_output_aliases={1: 0},
        compiler_params=_cparams("parallel", "parallel"),
        name="place_context_rows",
    )(y_ctx, ymix)


def _shortconv_kernel(x_ref, b_ref, c_ref, w_ref, ymix_ref, o_ref, *, tr, bs_rows, seq, ctx_len):
    del ymix_ref
    in_x, off, _ = _segment_position(tr, bs_rows, seq, ctx_len)
    period = jnp.where(in_x, GRID_W, ctx_len)
    pos = (off + lax.broadcasted_iota(jnp.int32, (tr, 1), 0)) & (period - 1)
    v = c_ref[...] * x_ref[...]
    left = jnp.where(pos == 0, 0.0, pltpu.roll(v, 1, axis=0))
    right = jnp.where(pos == period - 1, 0.0, pltpu.roll(v, tr - 1, axis=0))
    y = left * w_ref[0:1, :] + v * w_ref[1:2, :] + right * w_ref[2:3, :]
    o_ref[...] = (b_ref[...] * y).astype(o_ref.dtype)


def _shortconv(px, conv_w, ymix, *, x_col0, width, out_col0, tr, tc, bs_rows, seq, ctx_len):
    R = px.shape[0]
    nb = width // tc
    xb0 = x_col0 // tc
    ob0 = out_col0 // tc
    return pl.pallas_call(
        functools.partial(_shortconv_kernel, tr=tr, bs_rows=bs_rows, seq=seq, ctx_len=ctx_len),
        grid=(R // tr, nb),
        in_specs=[pl.BlockSpec((tr, tc), lambda i, j: (i, xb0 + j)),
                  pl.BlockSpec((tr, tc), lambda i, j: (i, xb0 + nb + j)),
                  pl.BlockSpec((tr, tc), lambda i, j: (i, xb0 + 2 * nb + j)),
                  pl.BlockSpec((SC_CONV, tc), lambda i, j: (0, j)),
                  pl.BlockSpec(memory_space=pl.ANY)],
        out_specs=pl.BlockSpec((tr, tc), lambda i, j: (i, ob0 + j)),
        out_shape=jax.ShapeDtypeStruct(ymix.shape, ymix.dtype),
        input_output_aliases={4: 0},
        compiler_params=_cparams("parallel", "parallel"),
        name="short_conv",
    )(px, px, px, conv_w, ymix)


def _pool_kernel(u_ref, w_ref, sc_ref, ymix_ref, o_ref, pa_ref, pb_ref, *, n_tok, unit, pg):
    del ymix_ref
    n_pos = n_tok // unit
    reach = 8 * unit
    pad = -(-reach // SUBLANES) * SUBLANES
    mg = -(-4 * unit // SUBLANES) * SUBLANES
    tp = n_tok + 2 * pad
    zeros_mg = jnp.zeros((mg, pg), F32)
    zeros_pad = jnp.zeros((pad, pg), F32)
    for buf in (pa_ref, pb_ref):
        buf[0:mg, :] = zeros_mg
        buf[mg + tp:mg + tp + mg, :] = zeros_mg
    row = lax.broadcasted_iota(jnp.int32, (n_tok, 1), 0) // unit
    for g, win in enumerate(POOL_WINDOWS):
        cs = slice(g * pg, (g + 1) * pg)
        u = u_ref[:, cs]
        pa_ref[mg:mg + pad, :] = zeros_pad
        pa_ref[mg + pad:mg + pad + n_tok, :] = u
        pa_ref[mg + pad + n_tok:mg + tp, :] = zeros_pad
        src, dst = pa_ref, pb_ref
        dst[mg:mg + tp, :] = src[mg - unit:mg - unit + tp, :] + src[mg:mg + tp, :]
        src, dst = dst, src
        step, w = 1, 2
        while w < win:
            dst[mg:mg + tp, :] = (src[mg - step * unit:mg - step * unit + tp, :]
                                  + src[mg + step * unit:mg + step * unit + tp, :])
            src, dst = dst, src
            step, w = step * 2, w * 2
        total = src[mg + pad:mg + pad + n_tok, :]
        lo = jnp.maximum(row - win // 2, 0)
        hi = jnp.minimum(row - win // 2 + win, n_pos)
        count = (hi - lo).astype(F32)
        dlt = total / count - u
        y = _dot(dlt.astype(BF16), w_ref[g]) * sc_ref[:, cs]
        o_ref[:, cs] = y.astype(o_ref.dtype)


def _pool(px, pool_w, pool_scale, ymix, *, u_col0, out_col0, n_tok, unit, blk0, n_batch, name):
    width = pool_scale.shape[1]
    pg = width // len(POOL_WINDOWS)
    reach = 8 * unit
    pad = -(-reach // SUBLANES) * SUBLANES
    mg = -(-4 * unit // SUBLANES) * SUBLANES
    buf_rows = n_tok + 2 * pad + 2 * mg
    ub = u_col0 // width
    ob = out_col0 // width
    return pl.pallas_call(
        functools.partial(_pool_kernel, n_tok=n_tok, unit=unit, pg=pg),
        grid=(n_batch,),
        in_specs=[pl.BlockSpec((n_tok, width), lambda b: (blk0 + b, ub)),
                  pl.BlockSpec(pool_w.shape, lambda b: (0, 0, 0)),
                  pl.BlockSpec((1, width), lambda b: (0, 0)),
                  pl.BlockSpec(memory_space=pl.ANY)],
        out_specs=pl.BlockSpec((n_tok, width), lambda b: (blk0 + b, ob)),
        out_shape=jax.ShapeDtypeStruct(ymix.shape, ymix.dtype),
        input_output_aliases={3: 0},
        scratch_shapes=[pltpu.VMEM((buf_rows, pg), F32), pltpu.VMEM((buf_rows, pg), F32)],
        compiler_params=_cparams("parallel"),
        name=name,
    )(px, pool_w, pool_scale, ymix)


def _ba_permutation(n_heads):
    perm = np.zeros(4 * n_heads, np.int32)
    for p in range(n_heads // 2):
        for d in range(2):
            for s in range(2):
                for kind in range(2):
                    perm[p * 8 + d * 4 + s * 2 + kind] = d * 2 * n_heads + kind * n_heads + 2 * p + s
    return perm


def _gate_param_row(param, n_heads):
    row = jnp.zeros((LANES,), F32)
    idx, src_d, src_h = [], [], []
    for p in range(n_heads // 2):
        for d in range(2):
            for s in range(2):
                idx.append(p * 8 + d * 4 + s * 2 + 1)
                src_d.append(d)
                src_h.append(2 * p + s)
    row = row.at[np.array(idx)].set(param[np.array(src_d), np.array(src_h)])
    return row.reshape(1, LANES)


def _pick_tile(n, prefs):
    for t in prefs:
        if n % t == 0:
            return t
    raise ValueError(f"no tile for {n}")


def kernel(x, c, ctx, c_ctx, w_mod, b_mod, norm_ffn1, w_ffn1_gu, w_ffn1_down, norm_mix, w_in, conv_qkv, a_log,
           dt_bias, gdn_norm, conv_short, pool_w, pool_scale, w_out, norm_ffn2, w_ffn2_gu, w_ffn2_down, norm_final):
    B, S, D = x.shape
    CL = ctx.shape[1]
    L = w_mod.shape[0]
    n_vheads = a_log.shape[2]
    qk_dim = (n_vheads // 2) * HEAD_DIM
    v_dim = n_vheads * HEAD_DIM
    sc_dim = conv_short.shape[2]
    pool_dim = pool_scale.shape[1]
    n_ba = 4 * n_vheads
    n_pairs = n_vheads // 2
    bs_rows = B * S
    R = bs_rows + B * CL
    assert S % CHUNK == 0 and CL % CHUNK == 0 and S % CL == 0 and n_ba <= LANES

    tm = _pick_tile(math.gcd(S, R), (1024, 512, 256, 128))
    tg = _pick_tile(math.gcd(S, R), (512, 256, 128))
    tr = min(256, CL)
    assert (B * CL) % tm == 0
    tn = 512 if D % 512 == 0 else 256
    tn_ff = 256
    tc_qk = _pick_tile(qk_dim, (1024, 512, 256, 128))
    tc_v = math.gcd(v_dim, 2 * tc_qk)

    z_col0 = 2 * qk_dim + v_dim
    ba_col0 = z_col0 + v_dim
    sc_col0 = ba_col0
    pool_col0 = sc_col0 + 3 * sc_dim
    ba_perm = _ba_permutation(n_vheads)
    onehot_t = np.zeros((LANES, n_ba), np.float32)
    onehot_t[np.arange(n_ba), ba_perm] = 1.0
    ba_onehot_t = jnp.asarray(onehot_t, dtype=BF16)
    prep_tile = 2 * tn if all(n % (2 * tn) == 0 for n in (ba_col0, w_in.shape[2] - n_ba)) else tn
    w_main_t, w_ba_t_all = _w_in_prep(jnp.swapaxes(w_in, 1, 2), tile=prep_tile, ba_row0=ba_col0, n_ba=n_ba)

    h = (x.reshape(bs_rows, D), ctx.reshape(B * CL, D))
    c_all = jnp.zeros((SUBLANES, D), F32).at[0:B].set(c).at[B].set(c_ctx)
    mods_all = _modulation(c_all, w_mod, b_mod.reshape(L, 1, N_MOD * D)).reshape(L, SUBLANES, N_MOD, D)

    common = dict(seq_rows=S, n_batch=B)
    seg = dict(bs_rows=bs_rows, seq=S, ctx_len=CL)
    whole = lambda w: [(0, w.shape[2])]
    w1gu = w_ffn1_gu[0].astype(BF16)
    for l in range(L):
        n_live = bs_rows if l == L - 1 else R
        mods = mods_all[l]

        hn = _adaln(h, norm_ffn1[l].reshape(1, D), mods, n_rows=R, shift_idx=0, scale_idx=1, tr=tg,
                    name="ffn1_adaln", **common)
        act, (w1d,) = _matmul(hn, w1gu, swiglu=True, out_dtype=BF16, tm=tm, tn=tn, name="ffn1_up",
                              casts=[(w_ffn1_down, l, whole(w_ffn1_down), tn_ff)])
        h, _ = _matmul_residual(act, w1d, h, mods, n_rows=R, gate_idx=2, scale=0.5, tm=tm, name="ffn1_down",
                                **common)
        w_ba_t = jnp.dot(ba_onehot_t, w_ba_t_all[l], preferred_element_type=F32).astype(BF16)

        hn = _adaln(h, norm_mix[l].reshape(1, D), mods, n_rows=R, shift_idx=3, scale_idx=4, tr=tg,
                    name="mix_adaln", **common)
        px, (w2gu, wo) = _matmul(hn, (w_main_t, l), swiglu=False, out_dtype=F32, tm=tm, tn=tn, name="in_proj",
                                 w_rows_out=True,
                                 casts=[(w_ffn2_gu, l, whole(w_ffn2_gu), tn), (w_out, l, whole(w_out), tn)])
        ba, _ = _matmul(hn, w_ba_t, swiglu=False, out_dtype=F32, tm=tm, tn=LANES, name="in_proj_ba",
                        w_rows_out=True)

        conv = dict(tr=tr, **seg)
        (qn,) = _qkv_conv(px, conv_qkv[l][:, 0:qk_dim], kind='q', col0=0, width=qk_dim, tc=tc_qk, **conv)
        kn, knt = _qkv_conv(px, conv_qkv[l][:, qk_dim:2 * qk_dim], kind='k', col0=qk_dim, width=qk_dim,
                            tc=tc_qk, **conv)
        (vn,) = _qkv_conv(px, conv_qkv[l][:, 2 * qk_dim:], kind='v', col0=2 * qk_dim, width=v_dim,
                          tc=tc_v, **conv)
        cols, rows = _gdn_gates(ba, _gate_param_row(a_log[l], n_vheads), _gate_param_row(dt_bias[l], n_vheads),
                                n_pairs=n_pairs, tm=tg)
        ymix, y_ctx = _gdn(qn, kn, knt, vn, cols, rows, px, z_col0, gdn_norm[l].reshape(1, HEAD_DIM),
                           n_batch=B, seq=S, ctx_len=CL, out_cols=v_dim + sc_dim + pool_dim)
        if l + 1 < L:
            ymix = _place_context_rows(y_ctx, ymix, row_blk0=bs_rows // CL, blk_rows=CL,
                                       blk_cols=math.gcd(v_dim, 4 * LANES))

        ymix = _shortconv(px, conv_short[l], ymix, x_col0=sc_col0, width=sc_dim, out_col0=v_dim,
                          tr=tr, tc=_pick_tile(sc_dim, (1024, 512, 256, 128)), **seg)
        pw = pool_w[l].astype(BF16)
        psc = pool_scale[l].reshape(1, pool_dim)
        ymix = _pool(px, pw, psc, ymix, u_col0=pool_col0, out_col0=v_dim + sc_dim, n_tok=S, unit=GRID_W,
                     blk0=0, n_batch=B, name="pool_latent")
        ymix = _pool(px, pw, psc, ymix, u_col0=pool_col0, out_col0=v_dim + sc_dim, n_tok=CL, unit=1,
                     blk0=bs_rows // CL, n_batch=B, name="pool_context")
        h, _ = _matmul_residual(ymix, wo, h, mods, n_rows=n_live, gate_idx=5, scale=1.0, tm=tm,
                                name="mix_out", **common)

        hn = _adaln(h, norm_ffn2[l].reshape(1, D), mods, n_rows=n_live, shift_idx=6, scale_idx=7, tr=tg,
                    name="ffn2_adaln", **common)
        act, (w2d,) = _matmul(hn, w2gu, swiglu=True, out_dtype=BF16, tm=tm, name="ffn2_up",
                              casts=[(w_ffn2_down, l, whole(w_ffn2_down), tn_ff)])
        next_casts = [(w_ffn1_gu, l + 1, whole(w_ffn1_gu), tn)] if l + 1 < L else []
        h, nxt = _matmul_residual(act, w2d, h, mods, n_rows=n_live, gate_idx=8, scale=0.5, tm=tm,
                                  name="ffn2_down", **common, casts=next_casts)
        if nxt:
            w1gu = nxt[0]

    out = _final_rmsnorm(h, norm_final.reshape(1, D), bs_rows, tg)
    return out.reshape(B, S, D)
```
